```python
import jax, jax.numpy as jnp
from jax import lax
import numpy as np

D_MODEL = 4096
BATCH = 4
SEQ = 2048
DEPTH = 4

N_A_LAYERS = DEPTH // 2
N_B_LAYERS = DEPTH - N_A_LAYERS
D_FF = 6144
ROPE_THETA = 10000.0
EPS = 1e-6
NEG = -1e30
FORCE = 1e6
Q_BLOCK = 128

MLA_HEADS = 32
MLA_Q_LORA = 1024
MLA_KV_LORA = 512
MLA_NOPE = 128
MLA_ROPE = 64
MLA_V = 128
MLA_QK = MLA_NOPE + MLA_ROPE
MLA_IN = MLA_Q_LORA + MLA_KV_LORA + MLA_ROPE

NSA_HEADS = 32
NSA_GROUPS = 4
NSA_HPG = NSA_HEADS // NSA_GROUPS
NSA_DH = 128
N_BRANCH = 3
N_KV_PARTS = 2 * N_BRANCH
CMP_BLOCK = 32
CMP_STRIDE = 16
CMP_HIDDEN = 256
SLC_BLOCK = 64
SLC_TOPK = 16
SLC_Q_BLOCK = 32
WINDOW = 512
NSA_Q_WIDTH = NSA_HEADS * NSA_DH
NSA_IN = NSA_Q_WIDTH + N_BRANCH * NSA_HEADS

kernel_name = "yoco_mla_nsa_macaron_trunk"


def rms_norm(x, g):
    xf = x.astype(jnp.float32)
    y = xf * lax.rsqrt(jnp.mean(xf * xf, axis=-1, keepdims=True) + EPS)
    return (y * g.astype(jnp.float32)).astype(x.dtype)


def rope_cos_sin(positions, dim):
    inv = jnp.power(ROPE_THETA, -jnp.arange(0, dim, 2, dtype=jnp.float32) / dim)
    ang = positions.astype(jnp.float32)[..., None] * inv
    return jnp.cos(ang), jnp.sin(ang)


def apply_rope(x, cos, sin):
    nmid = x.ndim - 3
    shp = cos.shape[:2] + (1,) * nmid + cos.shape[-1:]
    c = cos.reshape(shp)
    s = sin.reshape(shp)
    x1, x2 = jnp.split(x.astype(jnp.float32), 2, axis=-1)
    return jnp.concatenate([x1 * c - x2 * s, x2 * c + x1 * s], axis=-1).astype(x.dtype)


def swiglu(x, wg, wu, wd):
    return (jax.nn.silu(x @ wg) * (x @ wu)) @ wd


def causal_dense_attention(q, k, v):
    B, S, H, Dq = q.shape
    nb = S // Q_BLOCK
    scale = Dq ** -0.5
    qb = q.reshape(B, nb, Q_BLOCK, H, Dq).transpose(1, 0, 2, 3, 4)
    kpos = jnp.arange(S)

    def one(args):
        qi, i = args
        s = jnp.einsum('bqhd,bkhd->bhqk', qi, k).astype(jnp.float32) * scale
        qpos = i * Q_BLOCK + jnp.arange(Q_BLOCK)
        mask = kpos[None, :] <= qpos[:, None]
        p = jax.nn.softmax(jnp.where(mask, s, NEG), axis=-1)
        return jnp.einsum('bhqk,bkhd->bqhd', p.astype(v.dtype), v)

    o = lax.map(one, (qb, jnp.arange(nb)))
    return o.transpose(1, 0, 2, 3, 4).reshape(B, S, H, v.shape[-1])


def mla_mixer(u, cos, sin, w_in, g_cq, g_ckv, w_uq, w_ukv, g_q, g_k, w_o):
    B, S, _ = u.shape
    c = u @ w_in
    cq, ckv, k_rope = jnp.split(c, [MLA_Q_LORA, MLA_Q_LORA + MLA_KV_LORA], axis=-1)
    q = (rms_norm(cq, g_cq) @ w_uq).reshape(B, S, MLA_HEADS, MLA_QK)
    kv = (rms_norm(ckv, g_ckv) @ w_ukv).reshape(B, S, MLA_HEADS, MLA_NOPE + MLA_V)
    k_nope, v = jnp.split(kv, [MLA_NOPE], axis=-1)
    k = jnp.concatenate([k_nope, jnp.broadcast_to(k_rope[:, :, None, :], (B, S, MLA_HEADS, MLA_ROPE))], axis=-1)
    q = rms_norm(q, g_q)
    k = rms_norm(k, g_k)
    q = jnp.concatenate([q[..., :MLA_NOPE], apply_rope(q[..., MLA_NOPE:], cos, sin)], axis=-1)
    k = jnp.concatenate([k[..., :MLA_NOPE], apply_rope(k[..., MLA_NOPE:], cos, sin)], axis=-1)
    o = causal_dense_attention(q, k, v)
    return o.reshape(B, S, MLA_HEADS * MLA_V) @ w_o


def nsa_shared_kv(h, cos, sin, kv_norm, kv_w, cmp_pos_k, cmp_pos_v, cmp_k_w1, cmp_k_b1, cmp_k_w2,
                  cmp_v_w1, cmp_v_b1, cmp_v_w2, g_k_cmp, g_k_slc, g_k_win):
    B, S, _ = h.shape
    y = rms_norm(h, kv_norm)
    kv = (y @ kv_w).reshape(B, S, N_KV_PARTS, NSA_GROUPS, NSA_DH)
    k_cmp_raw, v_cmp_raw, k_slc, v_slc, k_win, v_win = [kv[:, :, i] for i in range(N_KV_PARTS)]
    n_cmp = (S - CMP_BLOCK) // CMP_STRIDE + 1
    idx = np.arange(n_cmp)[:, None] * CMP_STRIDE + np.arange(CMP_BLOCK)[None, :]

    def compress(t, pos, w1, b1, w2):
        blk = t[:, idx] + pos[None, None, :, None, :]
        blk = blk.transpose(0, 1, 3, 2, 4).reshape(B, n_cmp, NSA_GROUPS, CMP_BLOCK * NSA_DH)
        return jax.nn.silu(blk @ w1 + b1) @ w2

    k_cmp = rms_norm(compress(k_cmp_raw, cmp_pos_k, cmp_k_w1, cmp_k_b1, cmp_k_w2), g_k_cmp)
    v_cmp = compress(v_cmp_raw, cmp_pos_v, cmp_v_w1, cmp_v_b1, cmp_v_w2)
    k_slc = apply_rope(rms_norm(k_slc, g_k_slc), cos, sin)
    k_win = apply_rope(rms_norm(k_win, g_k_win), cos, sin)
    return (k_cmp, v_cmp, k_slc, v_slc, k_win, v_win)


def cmp_to_slc_weights(n_cmp, n_slc):
    cs = np.arange(n_cmp)[:, None] * CMP_STRIDE
    ss = np.arange(n_slc)[None, :] * SLC_BLOCK
    ov = np.clip(np.minimum(cs + CMP_BLOCK, ss + SLC_BLOCK) - np.maximum(cs, ss), 0, None)
    return (ov / CMP_STRIDE).astype(np.float32)


def selected_attention(q, k, v, sel):
    B, S, G, HPG, DH = q.shape
    n_slc = S // SLC_BLOCK
    n_sel = sel.shape[-1]
    scale = DH ** -0.5
    kb = k.reshape(B, n_slc, SLC_BLOCK, G, DH).transpose(0, 3, 1, 2, 4)
    vb = v.reshape(B, n_slc, SLC_BLOCK, G, DH).transpose(0, 3, 1, 2, 4)
    nq = S // SLC_Q_BLOCK
    qb = q.reshape(B, nq, SLC_Q_BLOCK, G, HPG, DH).transpose(1, 0, 2, 3, 4, 5)
    selb = sel.reshape(B, G, nq, SLC_Q_BLOCK, n_sel).transpose(2, 0, 1, 3, 4)
    bi = jnp.arange(B)[:, None, None, None]
    gi = jnp.arange(G)[None, :, None, None]

    def one(args):
        qi, si, i = args
        kg = kb[bi, gi, si]
        vg = vb[bi, gi, si]
        s = jnp.einsum('bqghd,bgqnld->bghqnl', qi, kg).astype(jnp.float32) * scale
        qpos = i * SLC_Q_BLOCK + jnp.arange(SLC_Q_BLOCK)
        kpos = si[..., None] * SLC_BLOCK + jnp.arange(SLC_BLOCK)
        mask = (kpos <= qpos[None, None, :, None, None])[:, :, None]
        s = jnp.where(mask, s, NEG).reshape(B, G, HPG, SLC_Q_BLOCK, n_sel * SLC_BLOCK)
        p = jax.nn.softmax(s, axis=-1).reshape(B, G, HPG, SLC_Q_BLOCK, n_sel, SLC_BLOCK)
        return jnp.einsum('bghqnl,bgqnld->bqghd', p.astype(vg.dtype), vg)

    o = lax.map(one, (qb, selb, jnp.arange(nq)))
    return o.transpose(1, 0, 2, 3, 4, 5).reshape(B, S, G, HPG, DH)


def window_attention(q, k, v):
    B, S, G, HPG, DH = q.shape
    nb = S // Q_BLOCK
    span = Q_BLOCK + WINDOW
    scale = DH ** -0.5
    kp = jnp.pad(k, ((0, 0), (WINDOW, 0), (0, 0), (0, 0)))
    vp = jnp.pad(v, ((0, 0), (WINDOW, 0), (0, 0), (0, 0)))
    qb = q.reshape(B, nb, Q_BLOCK, G, HPG, DH).transpose(1, 0, 2, 3, 4, 5)

    def one(args):
        qi, i = args
        start = i * Q_BLOCK
        ki = lax.dynamic_slice_in_dim(kp, start, span, axis=1)
        vi = lax.dynamic_slice_in_dim(vp, start, span, axis=1)
        s = jnp.einsum('bqghd,bkgd->bghqk', qi, ki).astype(jnp.float32) * scale
        qpos = start + jnp.arange(Q_BLOCK)
        kpos = start - WINDOW + jnp.arange(span)
        diff = qpos[:, None] - kpos[None, :]
        mask = (diff >= 0) & (diff < WINDOW) & (kpos[None, :] >= 0)
        p = jax.nn.softmax(jnp.where(mask, s, NEG), axis=-1)
        return jnp.einsum('bghqk,bkgd->bqghd', p.astype(vi.dtype), vi)

    o = lax.map(one, (qb, jnp.arange(nb)))
    return o.transpose(1, 0, 2, 3, 4, 5).reshape(B, S, G, HPG, DH)


def nsa_mixer(u, cos, sin, shared, w_in, b_gate, g_q, w_o):
    k_cmp, v_cmp, k_slc, v_slc, k_win, v_win = shared
    B, S, _ = u.shape
    proj = u @ w_in
    q = proj[..., :NSA_Q_WIDTH].reshape(B, S, NSA_GROUPS, NSA_HPG, NSA_DH)
    gates = jax.nn.sigmoid(proj[..., NSA_Q_WIDTH:] + b_gate).reshape(B, S, NSA_GROUPS, NSA_HPG, N_BRANCH)
    q = rms_norm(q, g_q)
    q_rot = apply_rope(q, cos, sin)
    scale = NSA_DH ** -0.5
    spos = jnp.arange(S)
    n_cmp = k_cmp.shape[1]
    cmp_end = jnp.arange(n_cmp) * CMP_STRIDE + CMP_BLOCK - 1
    cmask = cmp_end[None, :] <= spos[:, None]
    sc = jnp.einsum('bsghd,bngd->bghsn', q, k_cmp).astype(jnp.float32) * scale
    p_cmp = jax.nn.softmax(jnp.where(cmask, sc, NEG), axis=-1) * cmask
    o_cmp = jnp.einsum('bghsn,bngd->bsghd', p_cmp.astype(v_cmp.dtype), v_cmp)
    n_slc = S // SLC_BLOCK
    n_sel = min(SLC_TOPK, n_slc)
    agg = jnp.asarray(cmp_to_slc_weights(n_cmp, n_slc))
    imp = jnp.einsum('bghsn,nj->bgsj', p_cmp, agg)
    jb = jnp.arange(n_slc)
    cur = spos // SLC_BLOCK
    valid = (jb[None, :] * SLC_BLOCK) <= spos[:, None]
    forced = (jb[None, :] == 0) | (jb[None, :] == cur[:, None]) | (jb[None, :] == cur[:, None] - 1)
    imp = jnp.where(forced, FORCE, jnp.where(valid, imp, -1.0))
    _, sel = lax.top_k(imp, n_sel)
    o_slc = selected_attention(q_rot, k_slc, v_slc, sel)
    o_win = window_attention(q_rot, k_win, v_win)
    o = gates[..., 0:1] * o_cmp + gates[..., 1:2] * o_slc + gates[..., 2:3] * o_win
    return o.reshape(B, S, NSA_Q_WIDTH) @ w_o


def setup_inputs(seed: int = 0) -> dict:
    key = jax.random.key(seed)
    ks = iter(jax.random.split(key, 64))

    def w(shape, fan_in):
        return jax.random.normal(next(ks), shape, jnp.float32) * (fan_in ** -0.5)

    def gain(shape):
        return 1.0 + 0.02 * jax.random.normal(next(ks), shape, jnp.float32)

    def small(shape, s=0.02):
        return s * jax.random.normal(next(ks), shape, jnp.float32)

    D, F = D_MODEL, D_FF
    return {
        'x': jax.random.normal(next(ks), (BATCH, SEQ, D), jnp.float32),
        'positions': jnp.broadcast_to(jnp.arange(SEQ, dtype=jnp.int32), (BATCH, SEQ)),
        'ffn1_norm': gain((DEPTH, D)),
        'ffn1_w_gate': w((DEPTH, D, F), D),
        'ffn1_w_up': w((DEPTH, D, F), D),
        'ffn1_w_down': w((DEPTH, F, D), F),
        'mix_norm': gain((DEPTH, D)),
        'ffn2_norm': gain((DEPTH, D)),
        'ffn2_w_gate': w((DEPTH, D, F), D),
        'ffn2_w_up': w((DEPTH, D, F), D),
        'ffn2_w_down': w((DEPTH, F, D), F),
        'mla_w_in': w((N_A_LAYERS, D, MLA_IN), D),
        'mla_g_cq': gain((N_A_LAYERS, MLA_Q_LORA)),
        'mla_g_ckv': gain((N_A_LAYERS, MLA_KV_LORA)),
        'mla_w_uq': w((N_A_LAYERS, MLA_Q_LORA, MLA_HEADS * MLA_QK), MLA_Q_LORA),
        'mla_w_ukv': w((N_A_LAYERS, MLA_KV_LORA, MLA_HEADS * (MLA_NOPE + MLA_V)), MLA_KV_LORA),
        'mla_g_q': gain((N_A_LAYERS, MLA_QK)),
        'mla_g_k': gain((N_A_LAYERS, MLA_QK)),
        'mla_w_o': w((N_A_LAYERS, MLA_HEADS * MLA_V, D), MLA_HEADS * MLA_V),
        'kv_norm': gain((D,)),
        'kv_w': w((D, N_KV_PARTS * NSA_GROUPS * NSA_DH), D),
        'cmp_pos_k': small((CMP_BLOCK, NSA_DH), 0.1),
        'cmp_pos_v': small((CMP_BLOCK, NSA_DH), 0.1),
        'cmp_k_w1': w((CMP_BLOCK * NSA_DH, CMP_HIDDEN), CMP_BLOCK * NSA_DH),
        'cmp_k_b1': small((CMP_HIDDEN,)),
        'cmp_k_w2': w((CMP_HIDDEN, NSA_DH), CMP_HIDDEN),
        'cmp_v_w1': w((CMP_BLOCK * NSA_DH, CMP_HIDDEN), CMP_BLOCK * NSA_DH),
        'cmp_v_b1': small((CMP_HIDDEN,)),
        'cmp_v_w2': w((CMP_HIDDEN, NSA_DH), CMP_HIDDEN),
        'g_k_cmp': gain((NSA_DH,)),
        'g_k_slc': gain((NSA_DH,)),
        'g_k_win': gain((NSA_DH,)),
        'nsa_w_in': w((N_B_LAYERS, D, NSA_IN), D),
        'nsa_b_gate': small((N_B_LAYERS, N_BRANCH * NSA_HEADS)),
        'nsa_g_q': gain((N_B_LAYERS, NSA_DH)),
        'nsa_w_o': w((N_B_LAYERS, NSA_Q_WIDTH, D), NSA_Q_WIDTH),
    }


def reference(x, positions, ffn1_norm, ffn1_w_gate, ffn1_w_up, ffn1_w_down, mix_norm, ffn2_norm,
              ffn2_w_gate, ffn2_w_up, ffn2_w_down, mla_w_in, mla_g_cq, mla_g_ckv, mla_w_uq, mla_w_ukv,
              mla_g_q, mla_g_k, mla_w_o, kv_norm, kv_w, cmp_pos_k, cmp_pos_v, cmp_k_w1, cmp_k_b1,
              cmp_k_w2, cmp_v_w1, cmp_v_b1, cmp_v_w2, g_k_cmp, g_k_slc, g_k_win, nsa_w_in, nsa_b_gate,
              nsa_g_q, nsa_w_o):
    cos_a, sin_a = rope_cos_sin(positions, MLA_ROPE)
    cos_b, sin_b = rope_cos_sin(positions, NSA_DH)
    h = x
    shared = None
    for layer in range(DEPTH):
        h = h + 0.5 * swiglu(rms_norm(h, ffn1_norm[layer]), ffn1_w_gate[layer], ffn1_w_up[layer], ffn1_w_down[layer])
        u = rms_norm(h, mix_norm[layer])
        if layer < N_A_LAYERS:
            a = layer
            h = h + mla_mixer(u, cos_a, sin_a, mla_w_in[a], mla_g_cq[a], mla_g_ckv[a], mla_w_uq[a],
                              mla_w_ukv[a], mla_g_q[a], mla_g_k[a], mla_w_o[a])
        else:
            b = layer - N_A_LAYERS
            h = h + nsa_mixer(u, cos_b, sin_b, shared, nsa_w_in[b], nsa_b_gate[b], nsa_g_q[b], nsa_w_o[b])
        h = h + 0.5 * swiglu(rms_norm(h, ffn2_norm[layer]), ffn2_w_gate[layer], ffn2_w_up[layer], ffn2_w_down[layer])
        if layer == N_A_LAYERS - 1:
            shared = nsa_shared_kv(h, cos_b, sin_b, kv_norm, kv_w, cmp_pos_k, cmp_pos_v, cmp_k_w1, cmp_k_b1,
                                   cmp_k_w2, cmp_v_w1, cmp_v_b1, cmp_v_w2, g_k_cmp, g_k_slc, g_k_win)
    return h
```

```python
import functools

import jax
import jax.numpy as jnp
from jax import lax
from jax.experimental import pallas as pl
from jax.experimental.pallas import tpu as pltpu

F32 = jnp.float32
BF16 = jnp.bfloat16

ROPE_THETA = 10000.0
EPS = 1e-6
NEG = -1e30
FORCE = 1e6
MLA_NOPE = 128
MLA_ROPE = 64
MLA_V = 128
MLA_QK = MLA_NOPE + MLA_ROPE
MLA_HEAD_PAD = 256
NSA_DH = 128
N_BRANCH = 3
CMP_BLOCK = 32
CMP_STRIDE = 16
SLC_BLOCK = 64
SLC_TOPK = 16
WINDOW = 512

LANES = 128
V7X_VMEM_BYTES = 64 * 1024 * 1024
VMEM_CAP = V7X_VMEM_BYTES - 8 * 1024 * 1024

ATTN_TQ = 256


def _vmem_limit(nbytes):
    return int(min(VMEM_CAP, max(32 * 1024 * 1024, nbytes + 8 * 1024 * 1024)))


def _params(sem, nbytes):
    return pltpu.CompilerParams(dimension_semantics=sem, vmem_limit_bytes=_vmem_limit(nbytes))


def _rope_table_kernel(pos_ref, inv_ref, mc_ref, ms_ref, c_ref, s_ref):
    ang = pos_ref[...] * inv_ref[...]
    c_ref[...] = jnp.cos(ang) * mc_ref[...]
    s_ref[...] = jnp.sin(ang) * ms_ref[...]


def _rope_tables(pos, inv, mask_c, mask_s):
    T = pos.shape[0]
    tm = min(T, 1024)
    row = pl.BlockSpec((1, LANES), lambda i: (0, 0))
    out = pl.BlockSpec((tm, LANES), lambda i: (i, 0))
    return pl.pallas_call(
        _rope_table_kernel,
        name="rope_tables",
        grid=(T // tm,),
        in_specs=[pl.BlockSpec((tm, 1), lambda i: (i, 0)), row, row, row],
        out_specs=[out, out],
        out_shape=[jax.ShapeDtypeStruct((T, LANES), F32)] * 2,
        compiler_params=_params(("arbitrary",), 0),
    )(pos, inv, mask_c, mask_s)


def _rope_nsa(x, c, s):
    return x * c + pltpu.roll(x, 64, 1) * s


def _rope_mla(x, c, s_lo, s_hi):
    return x * c + pltpu.roll(x, 96, 1) * s_lo + pltpu.roll(x, 32, 1) * s_hi


def _rmsnorm_kernel(x_ref, g_ref, o_ref):
    x = x_ref[...]
    r = lax.rsqrt(jnp.mean(x * x, axis=-1, keepdims=True) + EPS)
    o_ref[...] = (x * r * g_ref[...]).astype(o_ref.dtype)


def _rmsnorm(x, g, *, layer=None, width=None, col_block=0):
    T = x.shape[0]
    width = x.shape[1] if width is None else width
    tm = min(T, 512)
    if layer is None:
        g = g.reshape(1, width)
        g_spec = pl.BlockSpec((1, width), lambda i: (0, 0))
    else:
        g = g.reshape(g.shape[0], 1, width)
        g_spec = pl.BlockSpec((None, 1, width), lambda i: (layer, 0, 0))
    return pl.pallas_call(
        _rmsnorm_kernel,
        name="rmsnorm",
        grid=(T // tm,),
        in_specs=[pl.BlockSpec((tm, width), lambda i: (i, col_block)), g_spec],
        out_specs=pl.BlockSpec((tm, width), lambda i: (i, 0)),
        out_shape=jax.ShapeDtypeStruct((T, width), BF16),
        compiler_params=_params(("arbitrary",), 6 * tm * width * 4),
    )(x, g)


def _mm_kernel(n_w, n_extra, n_out, epilogue, x_ref, *refs):
    w_refs = refs[:n_w]
    extra = refs[n_w:n_w + n_extra]
    outs = refs[n_w + n_extra:n_w + n_extra + n_out]
    w_bf = refs[n_w + n_extra + n_out:]

    @pl.when(pl.program_id(1) == 0)
    def _():
        for w, s in zip(w_refs, w_bf):
            s[...] = w[...].astype(BF16)

    x = x_ref[...]
    accs = [jnp.dot(x, s[...], preferred_element_type=F32) for s in w_bf]
    epilogue(accs, extra, outs)


def _mm(x, weights, epilogue, extras, outs, *, n_cols, tn, tm=1024):
    M, K = x.shape
    tm = min(tm, M)
    assert M % tm == 0 and n_cols % tn == 0
    in_specs = [pl.BlockSpec((tm, K), lambda j, i: (i, 0))]
    args = [x]
    for arr, layer, off in weights:
        if layer is None:
            in_specs.append(pl.BlockSpec((K, tn), lambda j, i, off=off: (0, j + off)))
        else:
            in_specs.append(pl.BlockSpec((None, K, tn), lambda j, i, off=off, layer=layer: (layer, 0, j + off)))
        args.append(arr)
    nbytes = 2 * tm * K * 2 + len(weights) * (2 * K * tn * 4 + K * tn * 2 + 2 * tm * tn * 4)
    for arr, bs, im in extras:
        in_specs.append(pl.BlockSpec(bs, im))
        args.append(arr)
        nbytes += 2 * 4 * functools.reduce(lambda a, b: a * (b or 1), bs, 1)
    out_specs, out_shapes = [], []
    for sds, bs, im in outs:
        out_specs.append(pl.BlockSpec(bs, im))
        out_shapes.append(sds)
        nbytes += 2 * 4 * functools.reduce(lambda a, b: a * (b or 1), bs, 1)
    kern = functools.partial(_mm_kernel, len(weights), len(extras), len(outs), epilogue)
    res = pl.pallas_call(
        kern,
        name="mm_" + getattr(epilogue, "func", epilogue).__name__[len("_epi_"):],
        grid=(n_cols // tn, M // tm),
        in_specs=in_specs,
        out_specs=out_specs,
        out_shape=out_shapes,
        scratch_shapes=[pltpu.VMEM((K, tn), BF16) for _ in weights],
        compiler_params=_params(("arbitrary", "arbitrary"), nbytes),
    )(*args)
    return res


def _epi_plain(accs, extra, outs):
    outs[0][...] = accs[0].astype(outs[0].dtype)


def _epi_swiglu(accs, extra, outs):
    g, u = accs
    outs[0][...] = (g * (1.0 / (1.0 + jnp.exp(-g))) * u).astype(outs[0].dtype)


def _epi_residual(alpha, accs, extra, outs):
    outs[0][...] = extra[0][...] + alpha * accs[0]


def _mm_plain(x, w, layer, col_off, n_cols, tn, dtype):
    M = x.shape[0]
    tm = min(1024, M)
    out = (jax.ShapeDtypeStruct((M, n_cols), dtype), (tm, tn), lambda j, i: (i, j))
    return _mm(x, [(w, layer, col_off)], _epi_plain, [], [out], n_cols=n_cols, tn=tn, tm=tm)[0]


def _ffn(h, norm, wg, wu, wd, layer):
    M, D = h.shape
    F = wg.shape[-1]
    tm = min(1024, M)
    u = _rmsnorm(h, norm, layer=layer)
    act = _mm(u, [(wg, layer, 0), (wu, layer, 0)], _epi_swiglu, [],
              [(jax.ShapeDtypeStruct((M, F), BF16), (tm, 256), lambda j, i: (i, j))],
              n_cols=F, tn=256, tm=tm)[0]
    return _mm_residual(act, wd, layer, h, 0.5)


def _mm_residual(x, w, layer, res, alpha):
    M, N = res.shape
    tm = min(1024, M)
    tn = 256
    blk = ((tm, tn), lambda j, i: (i, j))
    return _mm(x, [(w, layer, 0)], functools.partial(_epi_residual, alpha), [(res,) + blk],
               [(jax.ShapeDtypeStruct((M, N), F32),) + blk], n_cols=N, tn=tn, tm=tm)[0]


def _epi_mla_q(n_heads, accs, extra, outs):
    a = accs[0]
    g = extra[0][...]
    c, s_lo, s_hi = extra[1][...], extra[2][...], extra[3][...]
    o = outs[0]
    for hh in range(n_heads):
        b0 = hh * MLA_HEAD_PAD
        nope = a[:, b0:b0 + LANES]
        rp = a[:, b0 + LANES:b0 + 2 * LANES]
        ss = jnp.sum(nope * nope, -1, keepdims=True) + jnp.sum(rp * rp, -1, keepdims=True)
        r = lax.rsqrt(ss / MLA_QK + EPS)
        o[:, b0:b0 + LANES] = (nope * r * g[:, :LANES]).astype(o.dtype)
        o[:, b0 + LANES:b0 + 2 * LANES] = _rope_mla(rp * r * g[:, LANES:], c, s_lo, s_hi).astype(o.dtype)


def _epi_mla_kv(n_heads, accs, extra, outs):
    a = accs[0]
    g = extra[0][...]
    c, s_lo, s_hi = extra[1][...], extra[2][...], extra[3][...]
    kr = extra[4][...]
    ss_r = jnp.sum(kr * kr, -1, keepdims=True)
    k_o, v_o = outs
    for hh in range(n_heads):
        b0 = hh * MLA_HEAD_PAD
        kn = a[:, b0:b0 + LANES]
        r = lax.rsqrt((jnp.sum(kn * kn, -1, keepdims=True) + ss_r) / MLA_QK + EPS)
        k_o[:, b0:b0 + LANES] = (kn * r * g[:, :LANES]).astype(k_o.dtype)
        k_o[:, b0 + LANES:b0 + 2 * LANES] = _rope_mla(kr * r * g[:, LANES:], c, s_lo, s_hi).astype(k_o.dtype)
        v_o[:, hh * MLA_V:(hh + 1) * MLA_V] = a[:, b0 + LANES:b0 + 2 * LANES].astype(v_o.dtype)


def _pad_gain_mla(g):
    return jnp.pad(g, (0, MLA_HEAD_PAD - MLA_QK)).reshape(1, MLA_HEAD_PAD)


def _mla_attn_kernel(S, tq, scale, q_ref, k_ref, v_ref, o_ref):
    for i in range(S // tq):
        r0 = i * tq
        nk = r0 + tq
        q = q_ref[r0:r0 + tq, :]
        k = k_ref[0:nk, :]
        s = lax.dot_general(q, k, (((1,), (1,)), ((), ())), preferred_element_type=F32) * scale
        row = r0 + lax.broadcasted_iota(jnp.int32, (tq, nk), 0)
        col = lax.broadcasted_iota(jnp.int32, (tq, nk), 1)
        s = jnp.where(col <= row, s, NEG)
        m = jnp.max(s, -1, keepdims=True)
        e = jnp.exp(s - m)
        l = jnp.sum(e, -1, keepdims=True)
        o = jnp.dot(e.astype(BF16), v_ref[0:nk, :], preferred_element_type=F32)
        o_ref[r0:r0 + tq, :] = (o / l).astype(o_ref.dtype)


def _mla_attention(q, k, v, B, S, H):
    T = B * S
    tq = min(ATTN_TQ, S)
    kern = functools.partial(_mla_attn_kernel, S, tq, MLA_QK ** -0.5)
    nbytes = 2 * S * (2 * MLA_HEAD_PAD + 2 * MLA_V) * 2 + 6 * tq * S * 4
    return pl.pallas_call(
        kern,
        name="mla_attn",
        grid=(B, H),
        in_specs=[pl.BlockSpec((S, MLA_HEAD_PAD), lambda b, h: (b, h)),
                  pl.BlockSpec((S, MLA_HEAD_PAD), lambda b, h: (b, h)),
                  pl.BlockSpec((S, MLA_V), lambda b, h: (b, h))],
        out_specs=pl.BlockSpec((S, MLA_V), lambda b, h: (b, h)),
        out_shape=jax.ShapeDtypeStruct((T, H * MLA_V), BF16),
        compiler_params=_params(("arbitrary", "arbitrary"), nbytes),
    )(q, k, v)


def _mla_layer(h, u, a, B, S, tabs, mla_w_in, mla_g_cq, mla_g_ckv, mla_w_uq, mla_w_ukv, mla_g_q, mla_g_k, mla_w_o):
    T, D = h.shape
    q_lora = mla_g_cq.shape[-1]
    kv_lora = mla_g_ckv.shape[-1]
    H = mla_w_ukv.shape[-1] // (MLA_NOPE + MLA_V)
    c_tab, slo_tab, shi_tab = tabs
    tm = min(1024, T)
    lat = q_lora + kv_lora
    n_in = -(-(lat + LANES) // 512) * 512
    w_in = jnp.pad(mla_w_in[a], ((0, 0), (0, n_in - mla_w_in.shape[-1])))
    c = _mm_plain(u, w_in, None, 0, n_in, 512, F32)
    cq = _rmsnorm(c, mla_g_cq[a], width=q_lora, col_block=0)
    ckv = _rmsnorm(c, mla_g_ckv[a], width=kv_lora, col_block=q_lora // kv_lora)
    w_uq = jnp.pad(mla_w_uq[a].reshape(q_lora, H, MLA_QK), ((0, 0), (0, 0), (0, MLA_HEAD_PAD - MLA_QK)))
    w_uq = w_uq.reshape(q_lora, H * MLA_HEAD_PAD)
    hp = 2 if H % 2 == 0 else 1
    tn = hp * MLA_HEAD_PAD
    tab_specs = [(t, (tm, LANES), lambda j, i: (i, 0)) for t in (c_tab, slo_tab, shi_tab)]
    gain = lambda g: (_pad_gain_mla(g), (1, MLA_HEAD_PAD), lambda j, i: (0, 0))
    q = _mm(cq, [(w_uq, None, 0)], functools.partial(_epi_mla_q, hp), [gain(mla_g_q[a])] + tab_specs,
            [(jax.ShapeDtypeStruct((T, H * MLA_HEAD_PAD), BF16), (tm, tn), lambda j, i: (i, j))],
            n_cols=H * MLA_HEAD_PAD, tn=tn, tm=tm)[0]
    kr_spec = (c, (tm, LANES), lambda j, i: (i, lat // LANES))
    k, v = _mm(ckv, [(mla_w_ukv, a, 0)], functools.partial(_epi_mla_kv, hp),
               [gain(mla_g_k[a])] + tab_specs + [kr_spec],
               [(jax.ShapeDtypeStruct((T, H * MLA_HEAD_PAD), BF16), (tm, tn), lambda j, i: (i, j)),
                (jax.ShapeDtypeStruct((T, H * MLA_V), BF16), (tm, hp * MLA_V), lambda j, i: (i, j))],
               n_cols=H * MLA_HEAD_PAD, tn=tn, tm=tm)
    o = _mla_attention(q, k, v, B, S, H)
    return _mm_residual(o, mla_w_o, a, h, 1.0)


def _epi_nsa_k(n_groups, accs, extra, outs):
    a = accs[0]
    g = extra[0][...]
    c, s = extra[1][...], extra[2][...]
    o = outs[0]
    for gg in range(n_groups):
        x = a[:, gg * LANES:(gg + 1) * LANES]
        r = lax.rsqrt(jnp.mean(x * x, -1, keepdims=True) + EPS)
        o[:, gg * LANES:(gg + 1) * LANES] = _rope_nsa(x * r * g, c, s).astype(o.dtype)


def _compress_kernel(n_cmp, k_ref, v_ref, pk_ref, pv_ref, kw1_ref, kb1_ref, kw2_ref,
                     vw1_ref, vb1_ref, vw2_ref, gk_ref, ko_ref, vo_ref):
    half = CMP_STRIDE * NSA_DH
    n_rows = ko_ref.shape[0]
    row = lax.broadcasted_iota(jnp.int32, (n_rows, 1), 0)

    def phi(t_ref, pos_ref, w1_ref, b1_ref, w2_ref):
        slabs = [t_ref[pl.ds(l, n_rows, stride=CMP_STRIDE), :] for l in range(CMP_STRIDE)]
        lo = jnp.concatenate([slabs[l] + pos_ref[:, l * NSA_DH:(l + 1) * NSA_DH]
                              for l in range(CMP_STRIDE)], axis=1).astype(BF16)
        hi = jnp.concatenate([slabs[l] + pos_ref[:, half + l * NSA_DH:half + (l + 1) * NSA_DH]
                              for l in range(CMP_STRIDE)], axis=1).astype(BF16)
        p_lo = jnp.dot(lo, w1_ref[0:half, :].astype(BF16), preferred_element_type=F32)
        p_hi = jnp.dot(hi, w1_ref[half:2 * half, :].astype(BF16), preferred_element_type=F32)
        pre = p_lo + pltpu.roll(p_hi, n_rows - 1, 0) + b1_ref[...]
        hid = pre * (1.0 / (1.0 + jnp.exp(-pre)))
        out = jnp.dot(hid.astype(BF16), w2_ref[...].astype(BF16), preferred_element_type=F32)
        return jnp.where(row < n_cmp, out, 0.0)

    kc = phi(k_ref, pk_ref, kw1_ref, kb1_ref, kw2_ref)
    r = lax.rsqrt(jnp.mean(kc * kc, -1, keepdims=True) + EPS)
    ko_ref[...] = (kc * r * gk_ref[...]).astype(ko_ref.dtype)
    vo_ref[...] = phi(v_ref, pv_ref, vw1_ref, vb1_ref, vw2_ref).astype(vo_ref.dtype)


def _compress(raw, B, S, G, cmp_pos_k, cmp_pos_v, k_w1, k_b1, k_w2, v_w1, v_b1, v_w2, g_k_cmp):
    n_cmp = (S - CMP_BLOCK) // CMP_STRIDE + 1
    n_rows = S // CMP_STRIDE
    hid = k_w1.shape[-1]
    full = lambda shape: pl.BlockSpec(shape, lambda b, g: (0,) * len(shape))
    w_specs = [full((CMP_BLOCK * NSA_DH, hid)), full((1, hid)), full((hid, NSA_DH))]
    out_spec = pl.BlockSpec((None, None, n_rows, NSA_DH), lambda b, g: (b, g, 0, 0))
    out_sds = jax.ShapeDtypeStruct((B, G, n_rows, NSA_DH), BF16)
    nbytes = 4 * (2 * CMP_BLOCK * NSA_DH * hid * 4) + 4 * S * NSA_DH * 4 + 8 * n_rows * CMP_BLOCK * NSA_DH * 4
    return pl.pallas_call(
        functools.partial(_compress_kernel, n_cmp),
        name="nsa_compress",
        grid=(B, G),
        in_specs=[pl.BlockSpec((S, NSA_DH), lambda b, g: (b, g)),
                  pl.BlockSpec((S, NSA_DH), lambda b, g: (b, G + g)),
                  full((1, CMP_BLOCK * NSA_DH)), full((1, CMP_BLOCK * NSA_DH))]
                 + w_specs + w_specs + [full((1, NSA_DH))],
        out_specs=[out_spec, out_spec],
        out_shape=[out_sds, out_sds],
        compiler_params=_params(("arbitrary", "arbitrary"), nbytes),
    )(raw, raw, cmp_pos_k.reshape(1, -1), cmp_pos_v.reshape(1, -1),
      k_w1, k_b1.reshape(1, -1), k_w2, v_w1, v_b1.reshape(1, -1), v_w2, g_k_cmp.reshape(1, -1))


def _nsa_shared_kv(h, B, S, tabs, kv_norm, kv_w, cmp_pos_k, cmp_pos_v, cmp_k_w1, cmp_k_b1, cmp_k_w2,
                   cmp_v_w1, cmp_v_b1, cmp_v_w2, g_k_cmp, g_k_slc, g_k_win):
    T = h.shape[0]
    G = kv_w.shape[-1] // (2 * N_BRANCH * NSA_DH)
    part = G * NSA_DH
    tm = min(1024, T)
    c_tab, s_tab = tabs
    y = _rmsnorm(h, kv_norm)
    raw = _mm_plain(y, kv_w, None, 0, 2 * part, part, F32)
    k_cmp, v_cmp = _compress(raw, B, S, G, cmp_pos_k, cmp_pos_v, cmp_k_w1, cmp_k_b1, cmp_k_w2,
                             cmp_v_w1, cmp_v_b1, cmp_v_w2, g_k_cmp)

    def k_branch(part_idx, gain):
        extras = [(gain.reshape(1, NSA_DH), (1, NSA_DH), lambda j, i: (0, 0)),
                  (c_tab, (tm, LANES), lambda j, i: (i, 0)), (s_tab, (tm, LANES), lambda j, i: (i, 0))]
        return _mm(y, [(kv_w, None, part_idx)], functools.partial(_epi_nsa_k, G), extras,
                   [(jax.ShapeDtypeStruct((T, part), BF16), (tm, part), lambda j, i: (i, j))],
                   n_cols=part, tn=part, tm=tm)[0]

    k_slc = k_branch(2, g_k_slc)
    v_slc = _mm_plain(y, kv_w, None, 3, part, part, BF16)
    k_win = k_branch(4, g_k_win)
    v_win = _mm_plain(y, kv_w, None, 5, part, part, BF16)
    return k_cmp, v_cmp, k_slc, v_slc, k_win, v_win


def _epi_nsa_q(n_heads, accs, extra, outs):
    a = accs[0]
    g = extra[0][...]
    c, s = extra[1][...], extra[2][...]
    q_o, qr_o = outs
    for hh in range(n_heads):
        x = a[:, hh * LANES:(hh + 1) * LANES]
        r = lax.rsqrt(jnp.mean(x * x, -1, keepdims=True) + EPS)
        qn = x * r * g
        q_o[hh] = qn.astype(q_o.dtype)
        qr_o[hh] = _rope_nsa(qn, c, s).astype(qr_o.dtype)


def _softmax_pv(s, v):
    m = jnp.max(s, -1, keepdims=True)
    e = jnp.exp(s - m)
    l = jnp.sum(e, -1, keepdims=True)
    return jnp.dot(e.astype(BF16), v, preferred_element_type=F32) / l


def _nsa_attn_kernel(S, tq, hpg, n_cmp, scale,
                     q_ref, qr_ref, gl_ref, bg_ref, kc_ref, vc_ref, ks_ref, vs_ref, kw_ref, vw_ref,
                     o_ref, ocmp_scr, obuf_scr):
    g_idx = pl.program_id(1)
    n_cr = kc_ref.shape[0]
    n_slc = S // SLC_BLOCK
    n_sel = min(SLC_TOPK, n_slc)
    k_cmp = kc_ref[...]
    v_cmp = vc_ref[...]
    lane = lax.broadcasted_iota(jnp.int32, (tq, LANES), 1)

    jj = lax.broadcasted_iota(jnp.int32, (n_slc, n_cr), 0)
    nn = lax.broadcasted_iota(jnp.int32, (n_slc, n_cr), 1)
    cs = nn * CMP_STRIDE
    ss = jj * SLC_BLOCK
    ov = jnp.maximum(jnp.minimum(cs + CMP_BLOCK, ss + SLC_BLOCK) - jnp.maximum(cs, ss), 0)
    agg_t = jnp.where(nn < n_cmp, ov.astype(F32) / CMP_STRIDE, 0.0)

    for i in range(S // tq):
        r0 = i * tq
        nk = r0 + tq
        rows = slice(r0, r0 + tq)

        spos_c = r0 + lax.broadcasted_iota(jnp.int32, (tq, n_cr), 0)
        ncol = lax.broadcasted_iota(jnp.int32, (tq, n_cr), 1)
        cmask = (ncol * CMP_STRIDE + (CMP_BLOCK - 1) <= spos_c) & (ncol < n_cmp)
        cbias = jnp.where(cmask, 0.0, NEG)
        cmask_f = cmask.astype(F32)

        def cmp_body(hh, psum):
            q = q_ref[hh, rows, :]
            sc = lax.dot_general(q, k_cmp, (((1,), (1,)), ((), ())), preferred_element_type=F32) * scale + cbias
            m = jnp.max(sc, -1, keepdims=True)
            e = jnp.exp(sc - m) * cmask_f
            l = jnp.sum(e, -1, keepdims=True)
            p = e / jnp.where(l > 0.0, l, 1.0)
            ocmp_scr[hh] = jnp.dot(p.astype(BF16), v_cmp, preferred_element_type=F32)
            return psum + p

        psum = lax.fori_loop(0, hpg, cmp_body, jnp.zeros((tq, n_cr), F32))

        imp = lax.dot_general(agg_t, psum, (((1,), (1,)), ((), ())), precision=lax.Precision.HIGHEST,
                              preferred_element_type=F32)
        jb = lax.broadcasted_iota(jnp.int32, (n_slc, tq), 0)
        sp = r0 + lax.broadcasted_iota(jnp.int32, (n_slc, tq), 1)
        cur = sp // SLC_BLOCK
        valid = jb * SLC_BLOCK <= sp
        forced = (jb == 0) | (jb == cur) | (jb == cur - 1)
        imp = jnp.where(forced, FORCE, jnp.where(valid, imp, -1.0))
        rank = jnp.zeros((n_slc, tq), jnp.int32)
        for jp in range(n_slc):
            other = imp[jp:jp + 1, :]
            ahead = (other > imp) | ((other == imp) & (jp < jb))
            rank = rank + ahead.astype(jnp.int32)
        sel_t = (rank < n_sel).astype(BF16)

        eb = lax.broadcasted_iota(jnp.int32, (n_slc, nk), 0)
        ek = lax.broadcasted_iota(jnp.int32, (n_slc, nk), 1)
        expand = (ek // SLC_BLOCK == eb).astype(BF16)
        sel_keys = lax.dot_general(sel_t, expand, (((0,), (0,)), ((), ())), preferred_element_type=F32)
        qrow = r0 + lax.broadcasted_iota(jnp.int32, (tq, nk), 0)
        kcol = lax.broadcasted_iota(jnp.int32, (tq, nk), 1)
        sbias = jnp.where((sel_keys > 0.5) & (kcol <= qrow), 0.0, NEG)

        w0 = max(0, r0 + tq - WINDOW - tq)
        w0 = (w0 // tq) * tq
        nw = nk - w0
        wrow = r0 + lax.broadcasted_iota(jnp.int32, (tq, nw), 0)
        wcol = w0 + lax.broadcasted_iota(jnp.int32, (tq, nw), 1)
        wdiff = wrow - wcol
        wbias = jnp.where((wdiff >= 0) & (wdiff < WINDOW), 0.0, NEG)

        z = gl_ref[rows, :] + bg_ref[...]
        gates = 1.0 / (1.0 + jnp.exp(-z))

        k_s = ks_ref[0:nk, :]
        v_s = vs_ref[0:nk, :]
        k_w = kw_ref[w0:nk, :]
        v_w = vw_ref[w0:nk, :]

        def main_body(hh, carry):
            qr = qr_ref[hh, rows, :]
            s_s = lax.dot_general(qr, k_s, (((1,), (1,)), ((), ())), preferred_element_type=F32) * scale + sbias
            o_s = _softmax_pv(s_s, v_s)
            s_w = lax.dot_general(qr, k_w, (((1,), (1,)), ((), ())), preferred_element_type=F32) * scale + wbias
            o_w = _softmax_pv(s_w, v_w)
            c0 = (g_idx * hpg + hh) * N_BRANCH
            gate = lambda br: jnp.sum(jnp.where(lane == c0 + br, gates, 0.0), -1, keepdims=True)
            o = gate(0) * ocmp_scr[hh] + gate(1) * o_s + gate(2) * o_w
            obuf_scr[hh] = o.astype(obuf_scr.dtype)
            return carry

        lax.fori_loop(0, hpg, main_body, 0)
        for hh in range(hpg):
            o_ref[rows, hh * NSA_DH:(hh + 1) * NSA_DH] = obuf_scr[hh]


def _nsa_attention(q, qr, gate_logits, b_gate, shared, B, S, G, hpg):
    k_cmp, v_cmp, k_slc, v_slc, k_win, v_win = shared
    T = B * S
    H = G * hpg
    tq = min(ATTN_TQ, S)
    n_cmp = (S - CMP_BLOCK) // CMP_STRIDE + 1
    n_cr = k_cmp.shape[2]
    kern = functools.partial(_nsa_attn_kernel, S, tq, hpg, n_cmp, NSA_DH ** -0.5)
    head_spec = pl.BlockSpec((hpg, S, NSA_DH), lambda b, g: (g, b, 0))
    cmp_spec = pl.BlockSpec((None, None, n_cr, NSA_DH), lambda b, g: (b, g, 0, 0))
    kv_spec = pl.BlockSpec((S, NSA_DH), lambda b, g: (b, g))
    nbytes = (2 * 2 * hpg * S * NSA_DH * 2 + 2 * S * LANES * 4 + 8 * S * NSA_DH * 2
              + 2 * S * hpg * NSA_DH * 2 + 10 * tq * S * 4)
    return pl.pallas_call(
        kern,
        name="nsa_attn",
        grid=(B, G),
        in_specs=[head_spec, head_spec,
                  pl.BlockSpec((S, LANES), lambda b, g: (b, 0)),
                  pl.BlockSpec((1, LANES), lambda b, g: (0, 0)),
                  cmp_spec, cmp_spec, kv_spec, kv_spec, kv_spec, kv_spec],
        out_specs=pl.BlockSpec((S, hpg * NSA_DH), lambda b, g: (b, g)),
        out_shape=jax.ShapeDtypeStruct((T, H * NSA_DH), BF16),
        scratch_shapes=[pltpu.VMEM((hpg, tq, NSA_DH), F32), pltpu.VMEM((hpg, tq, NSA_DH), BF16)],
        compiler_params=_params(("arbitrary", "arbitrary"), nbytes),
    )(q, qr, gate_logits, b_gate, k_cmp, v_cmp, k_slc, v_slc, k_win, v_win)


def _nsa_layer(h, u, b, B, S, G, tabs, shared, nsa_w_in, nsa_b_gate, nsa_g_q, nsa_w_o):
    T, D = h.shape
    n_gate = nsa_b_gate.shape[-1]
    H = n_gate // N_BRANCH
    hpg = H // G
    q_width = H * NSA_DH
    assert n_gate <= LANES
    c_tab, s_tab = tabs
    tm = min(1024, T)
    hp = min(4, H)
    tn = hp * NSA_DH
    extras = [(nsa_g_q[b].reshape(1, NSA_DH), (1, NSA_DH), lambda j, i: (0, 0)),
              (c_tab, (tm, LANES), lambda j, i: (i, 0)), (s_tab, (tm, LANES), lambda j, i: (i, 0))]
    head_out = (jax.ShapeDtypeStruct((H, T, NSA_DH), BF16), (hp, tm, NSA_DH), lambda j, i: (j, i, 0))
    q, qr = _mm(u, [(nsa_w_in, b, 0)], functools.partial(_epi_nsa_q, hp), extras, [head_out, head_out],
                n_cols=q_width, tn=tn, tm=tm)
    w_gate = jnp.pad(nsa_w_in[b][:, q_width:], ((0, 0), (0, LANES - n_gate)))
    gate_logits = _mm_plain(u, w_gate, None, 0, LANES, LANES, F32)
    b_gate = jnp.pad(nsa_b_gate[b], (0, LANES - n_gate)).reshape(1, LANES)
    o = _nsa_attention(q, qr, gate_logits, b_gate, shared, B, S, G, hpg)
    return _mm_residual(o, nsa_w_o, b, h, 1.0)


def _lane_table(values):
    return jnp.asarray(values, F32).reshape(1, LANES)


def kernel(x, positions, ffn1_norm, ffn1_w_gate, ffn1_w_up, ffn1_w_down, mix_norm, ffn2_norm, ffn2_w_gate, ffn2_w_up, ffn2_w_down, mla_w_in, mla_g_cq, mla_g_ckv, mla_w_uq, mla_w_ukv, mla_g_q, mla_g_k, mla_w_o, kv_norm, kv_w, cmp_pos_k, cmp_pos_v, cmp_k_w1, cmp_k_b1, cmp_k_w2, cmp_v_w1, cmp_v_b1, cmp_v_w2, g_k_cmp, g_k_slc, g_k_win, nsa_w_in, nsa_b_gate, nsa_g_q, nsa_w_o):
    B, S, D = x.shape
    T = B * S
    depth = ffn1_norm.shape[0]
    n_a = mla_w_in.shape[0]
    G = kv_w.shape[-1] // (2 * N_BRANCH * NSA_DH)

    pos = positions.reshape(T, 1).astype(F32)
    half_a = MLA_ROPE // 2
    inv_a = jnp.power(ROPE_THETA, -jnp.arange(0, MLA_ROPE, 2, dtype=F32) / MLA_ROPE)
    zeros_a = jnp.zeros((LANES - MLA_ROPE,), F32)
    ones_h = jnp.ones((half_a,), F32)
    zeros_h = jnp.zeros((half_a,), F32)
    inv_a_l = _lane_table(jnp.concatenate([inv_a, inv_a, zeros_a]))
    mc_a = _lane_table(jnp.concatenate([ones_h, ones_h, zeros_a]))
    c_a, slo_a = _rope_tables(pos, inv_a_l, mc_a, _lane_table(jnp.concatenate([-ones_h, zeros_h, zeros_a])))
    _, shi_a = _rope_tables(pos, inv_a_l, mc_a, _lane_table(jnp.concatenate([zeros_h, ones_h, zeros_a])))
    inv_b = jnp.power(ROPE_THETA, -jnp.arange(0, NSA_DH, 2, dtype=F32) / NSA_DH)
    ones_b = jnp.ones((NSA_DH // 2,), F32)
    c_b, s_b = _rope_tables(pos, _lane_table(jnp.concatenate([inv_b, inv_b])),
                            _lane_table(jnp.concatenate([ones_b, ones_b])),
                            _lane_table(jnp.concatenate([-ones_b, ones_b])))
    tabs_a = (c_a, slo_a, shi_a)
    tabs_b = (c_b, s_b)

    h = x.reshape(T, D)
    shared = None
    for layer in range(depth):
        h = _ffn(h, ffn1_norm, ffn1_w_gate, ffn1_w_up, ffn1_w_down, layer)
        u = _rmsnorm(h, mix_norm, layer=layer)
        if layer < n_a:
            h = _mla_layer(h, u, layer, B, S, tabs_a, mla_w_in, mla_g_cq, mla_g_ckv, mla_w_uq, mla_w_ukv,
                           mla_g_q, mla_g_k, mla_w_o)
        else:
            h = _nsa_layer(h, u, layer - n_a, B, S, G, tabs_b, shared, nsa_w_in, nsa_b_gate, nsa_g_q, nsa_w_o)
        h = _ffn(h, ffn2_norm, ffn2_w_gate, ffn2_w_up, ffn2_w_down, layer)
        if layer == n_a - 1:
            shared = _nsa_shared_kv(h, B, S, tabs_b, kv_norm, kv_w, cmp_pos_k, cmp_pos_v, cmp_k_w1, cmp_k_b1,
                                    cmp_k_w2, cmp_v_w1, cmp_v_b1, cmp_v_w2, g_k_cmp, g_k_slc, g_k_win)
    return h.reshape(B, S, D)
```

```python
import functools
import math

import jax
import jax.numpy as jnp
from jax import lax
from jax.experimental import pallas as pl
from jax.experimental.pallas import tpu as pltpu

F32 = jnp.float32
BF16 = jnp.bfloat16

ROPE_THETA = 10000.0
EPS = 1e-6
NEG = -1e30
FORCE = 1e6
MLA_NOPE = 128
MLA_ROPE = 64
MLA_V = 128
MLA_QK = MLA_NOPE + MLA_ROPE
MLA_HEAD_PAD = 256
NSA_DH = 128
N_BRANCH = 3
CMP_BLOCK = 32
CMP_STRIDE = 16
SLC_BLOCK = 64
SLC_TOPK = 16
WINDOW = 512
LOG2E = math.log2(math.e)

LANES = 128
MXU_DIM = 256
V7X_VMEM_BYTES = 64 * 1024 * 1024
VMEM_CAP = V7X_VMEM_BYTES - 8 * 1024 * 1024

ATTN_TQ = 256


def _vmem_limit(nbytes):
    return int(min(VMEM_CAP, max(32 * 1024 * 1024, nbytes + 8 * 1024 * 1024)))


def _params(sem, nbytes):
    return pltpu.CompilerParams(dimension_semantics=sem, vmem_limit_bytes=_vmem_limit(nbytes))


def _rope_table_kernel(pos_ref, inv_ref, mc_ref, ms_ref, c_ref, s_ref):
    ang = pos_ref[...] * inv_ref[...]
    c_ref[...] = jnp.cos(ang) * mc_ref[...]
    s_ref[...] = jnp.sin(ang) * ms_ref[...]


def _rope_tables(pos, inv, mask_c, mask_s):
    T = pos.shape[0]
    tm = min(T, 1024)
    row = pl.BlockSpec((1, LANES), lambda i: (0, 0))
    out = pl.BlockSpec((tm, LANES), lambda i: (i, 0))
    return pl.pallas_call(
        _rope_table_kernel,
        name="rope_tables",
        grid=(T // tm,),
        in_specs=[pl.BlockSpec((tm, 1), lambda i: (i, 0)), row, row, row],
        out_specs=[out, out],
        out_shape=[jax.ShapeDtypeStruct((T, LANES), F32)] * 2,
        compiler_params=_params(("arbitrary",), 0),
    )(pos, inv, mask_c, mask_s)


def _rope_nsa(x, c, s):
    return x * c + pltpu.roll(x, 64, 1) * s


def _rope_mla(x, c, s):
    return x * c + pltpu.roll(x, 32, 1) * s


def _mxu_row_sum(sq, sel):
    hi = sq.astype(BF16)
    lo = (sq - hi.astype(F32)).astype(BF16)
    return jnp.dot(hi, sel, preferred_element_type=F32) + jnp.dot(lo, sel, preferred_element_type=F32)


def _sel_matrix(shape, pred):
    r = lax.broadcasted_iota(jnp.int32, shape, 0)
    c = lax.broadcasted_iota(jnp.int32, shape, 1)
    return pred(r, c).astype(BF16)


def _rmsnorm_kernel(x_ref, g_ref, o_ref):
    x = x_ref[...]
    r = lax.rsqrt(jnp.mean(x * x, axis=-1, keepdims=True) + EPS)
    o_ref[...] = (x * r * g_ref[...]).astype(o_ref.dtype)


def _rmsnorm(x, g, *, layer=None, width=None, col_block=0):
    T = x.shape[0]
    width = x.shape[1] if width is None else width
    tm = min(T, 512)
    if layer is None:
        g = g.reshape(1, width)
        g_spec = pl.BlockSpec((1, width), lambda i: (0, 0))
    else:
        g = g.reshape(g.shape[0], 1, width)
        g_spec = pl.BlockSpec((None, 1, width), lambda i: (layer, 0, 0))
    return pl.pallas_call(
        _rmsnorm_kernel,
        name="rmsnorm",
        grid=(T // tm,),
        in_specs=[pl.BlockSpec((tm, width), lambda i: (i, col_block)), g_spec],
        out_specs=pl.BlockSpec((tm, width), lambda i: (i, 0)),
        out_shape=jax.ShapeDtypeStruct((T, width), BF16),
        compiler_params=_params(("arbitrary",), 6 * tm * width * 4),
    )(x, g)


def _mm_kernel(n_w, n_extra, n_out, cw, epilogue, x_ref, *refs):
    w_refs = refs[:n_w]
    extra = refs[n_w:n_w + n_extra]
    outs = refs[n_w + n_extra:n_w + n_extra + n_out]
    w_bf = refs[n_w + n_extra + n_out:]

    @pl.when(pl.program_id(1) == 0)
    def _():
        for w, s in zip(w_refs, w_bf):
            s[...] = w[...].astype(BF16)

    x = x_ref[...]

    def chunk(c):
        return [jnp.dot(x, s[:, c * cw:(c + 1) * cw], preferred_element_type=F32) for s in w_bf]

    epilogue(chunk, extra, outs)


def _mm(x, weights, epilogue, extras, outs, *, n_cols, tn, tm=1024, cw=None):
    M, K = x.shape
    tm = min(tm, M)
    cw = tn if cw is None else cw
    assert M % tm == 0 and n_cols % tn == 0 and tn % cw == 0
    in_specs = [pl.BlockSpec((tm, K), lambda j, i: (i, 0))]
    args = [x]
    for arr, layer, off in weights:
        if layer is None:
            in_specs.append(pl.BlockSpec((K, tn), lambda j, i, off=off: (0, j + off)))
        else:
            in_specs.append(pl.BlockSpec((None, K, tn), lambda j, i, off=off, layer=layer: (layer, 0, j + off)))
        args.append(arr)
    nbytes = 2 * tm * K * 2 + len(weights) * (2 * K * tn * 4 + K * tn * 2 + 2 * tm * cw * 4)
    for arr, bs, im in extras:
        in_specs.append(pl.BlockSpec(bs, im))
        args.append(arr)
        nbytes += 2 * 4 * functools.reduce(lambda a, b: a * (b or 1), bs, 1)
    out_specs, out_shapes = [], []
    for sds, bs, im in outs:
        out_specs.append(pl.BlockSpec(bs, im))
        out_shapes.append(sds)
        nbytes += 2 * sds.dtype.itemsize * functools.reduce(lambda a, b: a * (b or 1), bs, 1)
    kern = functools.partial(_mm_kernel, len(weights), len(extras), len(outs), cw, epilogue)
    res = pl.pallas_call(
        kern,
        name="mm_" + getattr(epilogue, "func", epilogue).__name__[len("_epi_"):],
        grid=(n_cols // tn, M // tm),
        in_specs=in_specs,
        out_specs=out_specs,
        out_shape=out_shapes,
        scratch_shapes=[pltpu.VMEM((K, tn), BF16) for _ in weights],
        compiler_params=_params(("arbitrary", "arbitrary"), nbytes),
    )(*args)
    return res


def _epi_plain(chunk, extra, outs):
    outs[0][...] = chunk(0)[0].astype(outs[0].dtype)


def _epi_swiglu(chunk, extra, outs):
    g, u = chunk(0)
    outs[0][...] = (g * (1.0 / (1.0 + jnp.exp(-g))) * u).astype(outs[0].dtype)


def _epi_residual(alpha, chunk, extra, outs):
    outs[0][...] = extra[0][...] + alpha * chunk(0)[0]


def _mm_plain(x, w, layer, col_off, n_cols, tn, dtype):
    M = x.shape[0]
    tm = min(1024, M)
    out = (jax.ShapeDtypeStruct((M, n_cols), dtype), (tm, tn), lambda j, i: (i, j))
    return _mm(x, [(w, layer, col_off)], _epi_plain, [], [out], n_cols=n_cols, tn=tn, tm=tm)[0]


def _ffn(h, norm, wg, wu, wd, layer):
    M, D = h.shape
    F = wg.shape[-1]
    tm = min(1024, M)
    u = _rmsnorm(h, norm, layer=layer)
    act = _mm(u, [(wg, layer, 0), (wu, layer, 0)], _epi_swiglu, [],
              [(jax.ShapeDtypeStruct((M, F), BF16), (tm, 256), lambda j, i: (i, j))],
              n_cols=F, tn=256, tm=tm)[0]
    return _mm_residual(act, wd, layer, h, 0.5)


def _mm_residual(x, w, layer, res, alpha):
    M, N = res.shape
    K = x.shape[1]
    tn = 512 if N % 512 == 0 else 256
    tm = min(1024 if K <= 4096 else 512, M)
    blk = ((tm, tn), lambda j, i: (i, j))
    return _mm(x, [(w, layer, 0)], functools.partial(_epi_residual, alpha), [(res,) + blk],
               [(jax.ShapeDtypeStruct((M, N), F32),) + blk], n_cols=N, tn=tn, tm=tm)[0]


def _epi_mla_q(n_heads, out_scale, chunk, extra, outs):
    g = extra[0][...] * out_scale
    c, s = extra[1][...], extra[2][...]
    o = outs[0]
    sel = _sel_matrix((MLA_HEAD_PAD, MLA_HEAD_PAD), lambda r, col: r < MLA_QK)
    acc = chunk(0)[0]
    for hh in range(n_heads):
        b0 = hh * MLA_HEAD_PAD
        a = acc[:, b0:b0 + MLA_HEAD_PAD]
        r = lax.rsqrt(_mxu_row_sum(a * a, sel) / MLA_QK + EPS)
        an = a * r * g
        o[:, b0:b0 + LANES] = an[:, :LANES].astype(o.dtype)
        o[:, b0 + LANES:b0 + 2 * LANES] = _rope_mla(an[:, LANES:], c, s).astype(o.dtype)


def _epi_mla_kv(n_heads, chunk, extra, outs):
    g = extra[0][...]
    c, s = extra[1][...], extra[2][...]
    kr = extra[3][...]
    kr2 = kr * kr
    k_o, v_o = outs
    sel = _sel_matrix((MLA_HEAD_PAD, LANES), lambda r, col: r < MLA_QK)
    acc = chunk(0)[0]
    for hh in range(n_heads):
        b0 = hh * MLA_HEAD_PAD
        a = acc[:, b0:b0 + MLA_HEAD_PAD]
        kn = a[:, :LANES]
        ss = _mxu_row_sum(jnp.concatenate([kn * kn, kr2], axis=1), sel)
        r = lax.rsqrt(ss / MLA_QK + EPS)
        k_o[:, b0:b0 + LANES] = (kn * r * g[:, :LANES]).astype(k_o.dtype)
        k_o[:, b0 + LANES:b0 + 2 * LANES] = _rope_mla(kr * r * g[:, LANES:], c, s).astype(k_o.dtype)
        v_o[:, hh * MLA_V:(hh + 1) * MLA_V] = a[:, LANES:].astype(v_o.dtype)


def _pad_gain_mla(g):
    return jnp.concatenate([g, g[MLA_NOPE:]]).reshape(1, MLA_HEAD_PAD)


def _mla_attn_kernel(S, tq, q_ref, k_ref, v_ref, o_ref):
    row = lax.broadcasted_iota(jnp.int32, (tq, tq), 0)
    col = lax.broadcasted_iota(jnp.int32, (tq, tq), 1)
    tri = jnp.where(col <= row, 0.0, NEG)
    nt = (((1,), (1,)), ((), ()))
    for i in range(S // tq):
        r0 = i * tq
        q = q_ref[r0:r0 + tq, :]
        s_d = lax.dot_general(q, k_ref[r0:r0 + tq, :], nt, preferred_element_type=F32) + tri
        m = jnp.max(s_d, -1, keepdims=True)
        if i > 0:
            s_o = lax.dot_general(q, k_ref[0:r0, :], nt, preferred_element_type=F32)
            m = jnp.maximum(m, jnp.max(s_o, -1, keepdims=True))
        e_d = jnp.exp2(s_d - m)
        l = jnp.sum(e_d, -1, keepdims=True)
        o = jnp.dot(e_d.astype(BF16), v_ref[r0:r0 + tq, :], preferred_element_type=F32)
        if i > 0:
            e_o = jnp.exp2(s_o - m)
            l = l + jnp.sum(e_o, -1, keepdims=True)
            o = o + jnp.dot(e_o.astype(BF16), v_ref[0:r0, :], preferred_element_type=F32)
        o_ref[r0:r0 + tq, :] = (o / l).astype(o_ref.dtype)


def _mla_attention(q, k, v, B, S, H):
    T = B * S
    tq = min(ATTN_TQ, S)
    kern = functools.partial(_mla_attn_kernel, S, tq)
    nbytes = 2 * S * (2 * MLA_HEAD_PAD + 2 * MLA_V) * 2 + 6 * tq * S * 4
    return pl.pallas_call(
        kern,
        name="mla_attn",
        grid=(B, H),
        in_specs=[pl.BlockSpec((S, MLA_HEAD_PAD), lambda b, h: (b, h)),
                  pl.BlockSpec((S, MLA_HEAD_PAD), lambda b, h: (b, h)),
                  pl.BlockSpec((S, MLA_V), lambda b, h: (b, h))],
        out_specs=pl.BlockSpec((S, MLA_V), lambda b, h: (b, h)),
        out_shape=jax.ShapeDtypeStruct((T, H * MLA_V), BF16),
        compiler_params=_params(("arbitrary", "arbitrary"), nbytes),
    )(q, k, v)


def _mla_layer(h, u, a, B, S, tabs, mla_w_in, mla_g_cq, mla_g_ckv, mla_w_uq, mla_w_ukv, mla_g_q, mla_g_k, mla_w_o):
    T, D = h.shape
    q_lora = mla_g_cq.shape[-1]
    kv_lora = mla_g_ckv.shape[-1]
    H = mla_w_ukv.shape[-1] // (MLA_NOPE + MLA_V)
    c_tab, s_tab = tabs
    tm = min(1024, T)
    lat = q_lora + kv_lora
    n_in = -(-(lat + LANES) // 512) * 512
    w_in = mla_w_in[a]
    w_in = jnp.concatenate([w_in, w_in[:, lat:], jnp.zeros((D, n_in - lat - LANES), F32)], axis=1)
    c = _mm_plain(u, w_in, None, 0, n_in, 512, F32)
    cq = _rmsnorm(c, mla_g_cq[a], width=q_lora, col_block=0)
    ckv = _rmsnorm(c, mla_g_ckv[a], width=kv_lora, col_block=q_lora // kv_lora)
    w_uq = mla_w_uq[a].reshape(q_lora, H, MLA_QK)
    w_uq = jnp.concatenate([w_uq, w_uq[:, :, MLA_NOPE:]], axis=2).reshape(q_lora, H * MLA_HEAD_PAD)
    hp = 2 if H % 2 == 0 else 1
    tn = hp * MLA_HEAD_PAD
    tab_specs = [(t, (tm, LANES), lambda j, i: (i, 0)) for t in (c_tab, s_tab)]
    gain = lambda g: (_pad_gain_mla(g), (1, MLA_HEAD_PAD), lambda j, i: (0, 0))
    q = _mm(cq, [(w_uq, None, 0)], functools.partial(_epi_mla_q, hp, MLA_QK ** -0.5 * LOG2E),
            [gain(mla_g_q[a])] + tab_specs,
            [(jax.ShapeDtypeStruct((T, H * MLA_HEAD_PAD), BF16), (tm, tn), lambda j, i: (i, j))],
            n_cols=H * MLA_HEAD_PAD, tn=tn, tm=tm)[0]
    kr_spec = (c, (tm, LANES), lambda j, i: (i, lat // LANES))
    k, v = _mm(ckv, [(mla_w_ukv, a, 0)], functools.partial(_epi_mla_kv, hp),
               [gain(mla_g_k[a])] + tab_specs + [kr_spec],
               [(jax.ShapeDtypeStruct((T, H * MLA_HEAD_PAD), BF16), (tm, tn), lambda j, i: (i, j)),
                (jax.ShapeDtypeStruct((T, H * MLA_V), BF16), (tm, hp * MLA_V), lambda j, i: (i, j))],
               n_cols=H * MLA_HEAD_PAD, tn=tn, tm=tm)
    o = _mla_attention(q, k, v, B, S, H)
    return _mm_residual(o, mla_w_o, a, h, 1.0)


def _nsa_head_norm(a, g2):
    sel = _sel_matrix((MXU_DIM, MXU_DIM), lambda r, col: r // NSA_DH == col // NSA_DH)
    r = lax.rsqrt(_mxu_row_sum(a * a, sel) / NSA_DH + EPS)
    return a * r * g2


def _epi_nsa_k(n_pairs, chunk, extra, outs):
    g = extra[0][...]
    g2 = jnp.concatenate([g, g], axis=1)
    c, s = extra[1][...], extra[2][...]
    o = outs[0]
    acc = chunk(0)[0]
    for pp in range(n_pairs):
        kn = _nsa_head_norm(acc[:, pp * MXU_DIM:(pp + 1) * MXU_DIM], g2)
        for hh in range(2):
            col = (2 * pp + hh) * NSA_DH
            o[:, col:col + NSA_DH] = _rope_nsa(kn[:, hh * NSA_DH:(hh + 1) * NSA_DH], c, s).astype(o.dtype)


def _compress_kernel(n_cmp, k_ref, v_ref, pk_ref, pv_ref, kw1_ref, kb1_ref, kw2_ref,
                     vw1_ref, vb1_ref, vw2_ref, gk_ref, ko_ref, vo_ref):
    half = CMP_STRIDE * NSA_DH
    n_rows = ko_ref.shape[0]
    row = lax.broadcasted_iota(jnp.int32, (n_rows, 1), 0)

    def phi(t_ref, pos_ref, w1_ref, b1_ref, w2_ref):
        slabs = [t_ref[pl.ds(l, n_rows, stride=CMP_STRIDE), :] for l in range(CMP_STRIDE)]
        lo = jnp.concatenate([slabs[l] + pos_ref[:, l * NSA_DH:(l + 1) * NSA_DH]
                              for l in range(CMP_STRIDE)], axis=1).astype(BF16)
        hi = jnp.concatenate([slabs[l] + pos_ref[:, half + l * NSA_DH:half + (l + 1) * NSA_DH]
                              for l in range(CMP_STRIDE)], axis=1).astype(BF16)
        p_lo = jnp.dot(lo, w1_ref[0:half, :].astype(BF16), preferred_element_type=F32)
        p_hi = jnp.dot(hi, w1_ref[half:2 * half, :].astype(BF16), preferred_element_type=F32)
        pre = p_lo + pltpu.roll(p_hi, n_rows - 1, 0) + b1_ref[...]
        hid = pre * (1.0 / (1.0 + jnp.exp(-pre)))
        out = jnp.dot(hid.astype(BF16), w2_ref[...].astype(BF16), preferred_element_type=F32)
        return jnp.where(row < n_cmp, out, 0.0)

    kc = phi(k_ref, pk_ref, kw1_ref, kb1_ref, kw2_ref)
    r = lax.rsqrt(jnp.mean(kc * kc, -1, keepdims=True) + EPS)
    ko_ref[...] = (kc * r * gk_ref[...]).astype(ko_ref.dtype)
    vo_ref[...] = phi(v_ref, pv_ref, vw1_ref, vb1_ref, vw2_ref).astype(vo_ref.dtype)


def _compress(raw, B, S, G, cmp_pos_k, cmp_pos_v, k_w1, k_b1, k_w2, v_w1, v_b1, v_w2, g_k_cmp):
    n_cmp = (S - CMP_BLOCK) // CMP_STRIDE + 1
    n_rows = S // CMP_STRIDE
    hid = k_w1.shape[-1]
    full = lambda shape: pl.BlockSpec(shape, lambda b, g: (0,) * len(shape))
    w_specs = [full((CMP_BLOCK * NSA_DH, hid)), full((1, hid)), full((hid, NSA_DH))]
    out_spec = pl.BlockSpec((None, None, n_rows, NSA_DH), lambda b, g: (b, g, 0, 0))
    out_sds = jax.ShapeDtypeStruct((B, G, n_rows, NSA_DH), BF16)
    nbytes = 4 * (2 * CMP_BLOCK * NSA_DH * hid * 4) + 4 * S * NSA_DH * 4 + 8 * n_rows * CMP_BLOCK * NSA_DH * 4
    return pl.pallas_call(
        functools.partial(_compress_kernel, n_cmp),
        name="nsa_compress",
        grid=(B, G),
        in_specs=[pl.BlockSpec((S, NSA_DH), lambda b, g: (b, g)),
                  pl.BlockSpec((S, NSA_DH), lambda b, g: (b, G + g)),
                  full((1, CMP_BLOCK * NSA_DH)), full((1, CMP_BLOCK * NSA_DH))]
                 + w_specs + w_specs + [full((1, NSA_DH))],
        out_specs=[out_spec, out_spec],
        out_shape=[out_sds, out_sds],
        compiler_params=_params(("arbitrary", "arbitrary"), nbytes),
    )(raw, raw, cmp_pos_k.reshape(1, -1), cmp_pos_v.reshape(1, -1),
      k_w1, k_b1.reshape(1, -1), k_w2, v_w1, v_b1.reshape(1, -1), v_w2, g_k_cmp.reshape(1, -1))


def _nsa_shared_kv(h, B, S, tabs, kv_norm, kv_w, cmp_pos_k, cmp_pos_v, cmp_k_w1, cmp_k_b1, cmp_k_w2,
                   cmp_v_w1, cmp_v_b1, cmp_v_w2, g_k_cmp, g_k_slc, g_k_win):
    T = h.shape[0]
    G = kv_w.shape[-1] // (2 * N_BRANCH * NSA_DH)
    assert G % 2 == 0
    part = G * NSA_DH
    tm = min(1024, T)
    c_tab, s_tab = tabs
    y = _rmsnorm(h, kv_norm)
    raw = _mm_plain(y, kv_w, None, 0, 2 * part, part, F32)
    k_cmp, v_cmp = _compress(raw, B, S, G, cmp_pos_k, cmp_pos_v, cmp_k_w1, cmp_k_b1, cmp_k_w2,
                             cmp_v_w1, cmp_v_b1, cmp_v_w2, g_k_cmp)

    def k_branch(part_idx, gain):
        extras = [(gain.reshape(1, NSA_DH), (1, NSA_DH), lambda j, i: (0, 0)),
                  (c_tab, (tm, LANES), lambda j, i: (i, 0)), (s_tab, (tm, LANES), lambda j, i: (i, 0))]
        return _mm(y, [(kv_w, None, part_idx)], functools.partial(_epi_nsa_k, G // 2), extras,
                   [(jax.ShapeDtypeStruct((T, part), BF16), (tm, part), lambda j, i: (i, j))],
                   n_cols=part, tn=part, tm=tm)[0]

    k_slc = k_branch(2, g_k_slc)
    v_slc = _mm_plain(y, kv_w, None, 3, part, part, BF16)
    k_win = k_branch(4, g_k_win)
    v_win = _mm_plain(y, kv_w, None, 5, part, part, BF16)
    return k_cmp, v_cmp, k_slc, v_slc, k_win, v_win


def _epi_nsa_q(n_pairs, out_scale, chunk, extra, outs):
    g = extra[0][...] * out_scale
    g2 = jnp.concatenate([g, g], axis=1)
    c, s = extra[1][...], extra[2][...]
    q_o, qr_o = outs
    acc = chunk(0)[0]
    for pp in range(n_pairs):
        qn = _nsa_head_norm(acc[:, pp * MXU_DIM:(pp + 1) * MXU_DIM], g2)
        for hh in range(2):
            x = qn[:, hh * NSA_DH:(hh + 1) * NSA_DH]
            q_o[2 * pp + hh] = x.astype(q_o.dtype)
            qr_o[2 * pp + hh] = _rope_nsa(x, c, s).astype(qr_o.dtype)


def _softmax2_pv(s, v):
    m = jnp.max(s, -1, keepdims=True)
    e = jnp.exp2(s - m)
    l = jnp.sum(e, -1, keepdims=True)
    return jnp.dot(e.astype(BF16), v, preferred_element_type=F32) / l


def _nsa_attn_kernel(S, tq, hpg, n_cmp,
                     q_ref, qr_ref, gl_ref, bg_ref, kc_ref, vc_ref, ks_ref, vs_ref, kw_ref, vw_ref,
                     o_ref, ocmp_scr, obuf_scr):
    g_idx = pl.program_id(1)
    n_cr = kc_ref.shape[0]
    n_slc = S // SLC_BLOCK
    n_sel = min(SLC_TOPK, n_slc)
    k_cmp = kc_ref[...]
    v_cmp = vc_ref[...]
    lane = lax.broadcasted_iota(jnp.int32, (tq, LANES), 1)
    nt = (((1,), (1,)), ((), ()))

    jj = lax.broadcasted_iota(jnp.int32, (n_slc, n_cr), 0)
    nn = lax.broadcasted_iota(jnp.int32, (n_slc, n_cr), 1)
    cs = nn * CMP_STRIDE
    ss = jj * SLC_BLOCK
    ov = jnp.maximum(jnp.minimum(cs + CMP_BLOCK, ss + SLC_BLOCK) - jnp.maximum(cs, ss), 0)
    agg_t = jnp.where(nn < n_cmp, ov.astype(F32) / CMP_STRIDE, 0.0)

    for i in range(S // tq):
        r0 = i * tq
        nk = r0 + tq
        rows = slice(r0, r0 + tq)

        spos_c = r0 + lax.broadcasted_iota(jnp.int32, (tq, n_cr), 0)
        ncol = lax.broadcasted_iota(jnp.int32, (tq, n_cr), 1)
        cmask = (ncol * CMP_STRIDE + (CMP_BLOCK - 1) <= spos_c) & (ncol < n_cmp)
        cbias = jnp.where(cmask, 0.0, NEG)
        cmask_f = cmask.astype(F32)
        q_all = q_ref[:, rows, :].reshape(hpg * tq, NSA_DH)
        sc = lax.dot_general(q_all, k_cmp, nt, preferred_element_type=F32).reshape(hpg, tq, n_cr) + cbias
        m = jnp.max(sc, -1, keepdims=True)
        e = jnp.exp2(sc - m) * cmask_f
        l = jnp.sum(e, -1, keepdims=True)
        p = e / jnp.where(l > 0.0, l, 1.0)
        ocmp_scr[...] = jnp.dot(p.reshape(hpg * tq, n_cr).astype(BF16), v_cmp,
                                preferred_element_type=F32).reshape(hpg, tq, NSA_DH)
        psum = jnp.sum(p, axis=0)

        imp = lax.dot_general(agg_t, psum, nt, precision=lax.Precision.HIGHEST,
                              preferred_element_type=F32)
        jb = lax.broadcasted_iota(jnp.int32, (n_slc, tq), 0)
        sp = r0 + lax.broadcasted_iota(jnp.int32, (n_slc, tq), 1)
        cur = sp // SLC_BLOCK
        valid = jb * SLC_BLOCK <= sp
        forced = (jb == 0) | (jb == cur) | (jb == cur - 1)
        imp = jnp.where(forced, FORCE, jnp.where(valid, imp, -1.0))
        rank = jnp.zeros((n_slc, tq), jnp.int32)
        for jp in range(n_slc):
            other = imp[jp:jp + 1, :]
            ahead = (other > imp) | ((other == imp) & (jp < jb))
            rank = rank + ahead.astype(jnp.int32)
        sel_t = (rank < n_sel).astype(BF16)

        eb = lax.broadcasted_iota(jnp.int32, (n_slc, nk), 0)
        ek = lax.broadcasted_iota(jnp.int32, (n_slc, nk), 1)
        expand = (ek // SLC_BLOCK == eb).astype(BF16)
        sel_keys = lax.dot_general(sel_t, expand, (((0,), (0,)), ((), ())), preferred_element_type=F32)
        qrow = r0 + lax.broadcasted_iota(jnp.int32, (tq, nk), 0)
        kcol = lax.broadcasted_iota(jnp.int32, (tq, nk), 1)
        sbias = jnp.where((sel_keys > 0.5) & (kcol <= qrow), 0.0, NEG)

        w0 = (max(0, r0 - WINDOW) // tq) * tq
        nw = nk - w0
        wrow = r0 + lax.broadcasted_iota(jnp.int32, (tq, nw), 0)
        wcol = w0 + lax.broadcasted_iota(jnp.int32, (tq, nw), 1)
        wdiff = wrow - wcol
        wbias = jnp.where((wdiff >= 0) & (wdiff < WINDOW), 0.0, NEG)

        z = gl_ref[rows, :] + bg_ref[...]
        gates = 1.0 / (1.0 + jnp.exp(-z))

        k_s = ks_ref[0:nk, :]
        v_s = vs_ref[0:nk, :]
        k_w = kw_ref[w0:nk, :]
        v_w = vw_ref[w0:nk, :]

        def main_body(hh, carry):
            qr = qr_ref[hh, rows, :]
            o_s = _softmax2_pv(lax.dot_general(qr, k_s, nt, preferred_element_type=F32) + sbias, v_s)
            o_w = _softmax2_pv(lax.dot_general(qr, k_w, nt, preferred_element_type=F32) + wbias, v_w)
            c0 = (g_idx * hpg + hh) * N_BRANCH
            gate = lambda br: jnp.sum(jnp.where(lane == c0 + br, gates, 0.0), -1, keepdims=True)
            o = gate(0) * ocmp_scr[hh] + gate(1) * o_s + gate(2) * o_w
            obuf_scr[hh] = o.astype(obuf_scr.dtype)
            return carry

        lax.fori_loop(0, hpg, main_body, 0, unroll=2)
        for hh in range(hpg):
            o_ref[rows, hh * NSA_DH:(hh + 1) * NSA_DH] = obuf_scr[hh]


def _nsa_attention(q, qr, gate_logits, b_gate, shared, B, S, G, hpg):
    k_cmp, v_cmp, k_slc, v_slc, k_win, v_win = shared
    T = B * S
    H = G * hpg
    tq = min(ATTN_TQ, S)
    n_cmp = (S - CMP_BLOCK) // CMP_STRIDE + 1
    n_cr = k_cmp.shape[2]
    kern = functools.partial(_nsa_attn_kernel, S, tq, hpg, n_cmp)
    head_spec = pl.BlockSpec((hpg, S, NSA_DH), lambda b, g: (g, b, 0))
    cmp_spec = pl.BlockSpec((None, None, n_cr, NSA_DH), lambda b, g: (b, g, 0, 0))
    kv_spec = pl.BlockSpec((S, NSA_DH), lambda b, g: (b, g))
    nbytes = (2 * 2 * hpg * S * NSA_DH * 2 + 2 * S * LANES * 4 + 8 * S * NSA_DH * 2
              + 2 * S * hpg * NSA_DH * 2 + 10 * tq * S * 4)
    return pl.pallas_call(
        kern,
        name="nsa_attn",
        grid=(B, G),
        in_specs=[head_spec, head_spec,
                  pl.BlockSpec((S, LANES), lambda b, g: (b, 0)),
                  pl.BlockSpec((1, LANES), lambda b, g: (0, 0)),
                  cmp_spec, cmp_spec, kv_spec, kv_spec, kv_spec, kv_spec],
        out_specs=pl.BlockSpec((S, hpg * NSA_DH), lambda b, g: (b, g)),
        out_shape=jax.ShapeDtypeStruct((T, H * NSA_DH), BF16),
        scratch_shapes=[pltpu.VMEM((hpg, tq, NSA_DH), F32), pltpu.VMEM((hpg, tq, NSA_DH), BF16)],
        compiler_params=_params(("arbitrary", "arbitrary"), nbytes),
    )(q, qr, gate_logits, b_gate, k_cmp, v_cmp, k_slc, v_slc, k_win, v_win)


def _nsa_layer(h, u, b, B, S, G, tabs, shared, nsa_w_in, nsa_b_gate, nsa_g_q, nsa_w_o):
    T, D = h.shape
    n_gate = nsa_b_gate.shape[-1]
    H = n_gate // N_BRANCH
    hpg = H // G
    q_width = H * NSA_DH
    assert n_gate <= LANES and H % 2 == 0
    c_tab, s_tab = tabs
    tm = min(1024, T)
    hp = min(4, H)
    tn = hp * NSA_DH
    extras = [(nsa_g_q[b].reshape(1, NSA_DH), (1, NSA_DH), lambda j, i: (0, 0)),
              (c_tab, (tm, LANES), lambda j, i: (i, 0)), (s_tab, (tm, LANES), lambda j, i: (i, 0))]
    head_out = (jax.ShapeDtypeStruct((H, T, NSA_DH), BF16), (hp, tm, NSA_DH), lambda j, i: (j, i, 0))
    q, qr = _mm(u, [(nsa_w_in, b, 0)], functools.partial(_epi_nsa_q, hp // 2, NSA_DH ** -0.5 * LOG2E),
                extras, [head_out, head_out], n_cols=q_width, tn=tn, tm=tm)
    w_gate = jnp.pad(nsa_w_in[b][:, q_width:], ((0, 0), (0, LANES - n_gate)))
    gate_logits = _mm_plain(u, w_gate, None, 0, LANES, LANES, F32)
    b_gate = jnp.pad(nsa_b_gate[b], (0, LANES - n_gate)).reshape(1, LANES)
    o = _nsa_attention(q, qr, gate_logits, b_gate, shared, B, S, G, hpg)
    return _mm_residual(o, nsa_w_o, b, h, 1.0)


def _lane_table(values):
    return jnp.asarray(values, F32).reshape(1, LANES)


def kernel(x, positions, ffn1_norm, ffn1_w_gate, ffn1_w_up, ffn1_w_down, mix_norm, ffn2_norm, ffn2_w_gate, ffn2_w_up, ffn2_w_down, mla_w_in, mla_g_cq, mla_g_ckv, mla_w_uq, mla_w_ukv, mla_g_q, mla_g_k, mla_w_o, kv_norm, kv_w, cmp_pos_k, cmp_pos_v, cmp_k_w1, cmp_k_b1, cmp_k_w2, cmp_v_w1, cmp_v_b1, cmp_v_w2, g_k_cmp, g_k_slc, g_k_win, nsa_w_in, nsa_b_gate, nsa_g_q, nsa_w_o):
    B, S, D = x.shape
    T = B * S
    depth = ffn1_norm.shape[0]
    n_a = mla_w_in.shape[0]
    G = kv_w.shape[-1] // (2 * N_BRANCH * NSA_DH)

    pos = positions.reshape(T, 1).astype(F32)
    half_a = MLA_ROPE // 2
    inv_a = jnp.power(ROPE_THETA, -jnp.arange(0, MLA_ROPE, 2, dtype=F32) / MLA_ROPE)
    zeros_a = jnp.zeros((LANES - MLA_ROPE,), F32)
    ones_h = jnp.ones((half_a,), F32)
    tabs_a = _rope_tables(pos, _lane_table(jnp.concatenate([inv_a, inv_a, zeros_a])),
                          _lane_table(jnp.concatenate([ones_h, ones_h, zeros_a])),
                          _lane_table(jnp.concatenate([-ones_h, ones_h, zeros_a])))
    inv_b = jnp.power(ROPE_THETA, -jnp.arange(0, NSA_DH, 2, dtype=F32) / NSA_DH)
    ones_b = jnp.ones((NSA_DH // 2,), F32)
    tabs_b = _rope_tables(pos, _lane_table(jnp.concatenate([inv_b, inv_b])),
                          _lane_table(jnp.concatenate([ones_b, ones_b])),
                          _lane_table(jnp.concatenate([-ones_b, ones_b])))

    h = x.reshape(T, D)
    shared = None
    for layer in range(depth):
        h = _ffn(h, ffn1_norm, ffn1_w_gate, ffn1_w_up, ffn1_w_down, layer)
        u = _rmsnorm(h, mix_norm, layer=layer)
        if layer < n_a:
            h = _mla_layer(h, u, layer, B, S, tabs_a, mla_w_in, mla_g_cq, mla_g_ckv, mla_w_uq, mla_w_ukv,
                           mla_g_q, mla_g_k, mla_w_o)
        else:
            h = _nsa_layer(h, u, layer - n_a, B, S, G, tabs_b, shared, nsa_w_in, nsa_b_gate, nsa_g_q, nsa_w_o)
        h = _ffn(h, ffn2_norm, ffn2_w_gate, ffn2_w_up, ffn2_w_down, layer)
        if layer == n_a - 1:
            shared = _nsa_shared_kv(h, B, S, tabs_b, kv_norm, kv_w, cmp_pos_k, cmp_pos_v, cmp_k_w1, cmp_k_b1,
                                    cmp_k_w2, cmp_v_w1, cmp_v_b1, cmp_v_w2, g_k_cmp, g_k_slc, g_k_win)
    return h.reshape(B, S, D)
```

```python
import functools
import math

import jax
import jax.numpy as jnp
from jax import lax
from jax.experimental import pallas as pl
from jax.experimental.pallas import tpu as pltpu

F32 = jnp.float32
BF16 = jnp.bfloat16

ROPE_THETA = 10000.0
EPS = 1e-6
NEG = -1e30
FORCE = 1e6
MLA_NOPE = 128
MLA_ROPE = 64
MLA_V = 128
MLA_QK = MLA_NOPE + MLA_ROPE
MLA_HEAD_PAD = 256
NSA_DH = 128
N_BRANCH = 3
CMP_BLOCK = 32
CMP_STRIDE = 16
SLC_BLOCK = 64
SLC_TOPK = 16
WINDOW = 512
LOG2E = math.log2(math.e)

LANES = 128
MXU_DIM = 256
V7X_VMEM_BYTES = 64 * 1024 * 1024
VMEM_CAP = V7X_VMEM_BYTES - 8 * 1024 * 1024

ATTN_TQ = 256


def _vmem_limit(nbytes):
    return int(min(VMEM_CAP, max(32 * 1024 * 1024, nbytes + 8 * 1024 * 1024)))


def _params(sem, nbytes):
    return pltpu.CompilerParams(dimension_semantics=sem, vmem_limit_bytes=_vmem_limit(nbytes))


def _rope_table_kernel(pos_ref, inv_ref, mc_ref, ms_ref, c_ref, s_ref):
    ang = pos_ref[...] * inv_ref[...]
    c_ref[...] = jnp.cos(ang) * mc_ref[...]
    s_ref[...] = jnp.sin(ang) * ms_ref[...]


def _rope_tables(pos, inv, mask_c, mask_s):
    T = pos.shape[0]
    tm = min(T, 1024)
    row = pl.BlockSpec((1, LANES), lambda i: (0, 0))
    out = pl.BlockSpec((tm, LANES), lambda i: (i, 0))
    return pl.pallas_call(
        _rope_table_kernel,
        name="rope_tables",
        grid=(T // tm,),
        in_specs=[pl.BlockSpec((tm, 1), lambda i: (i, 0)), row, row, row],
        out_specs=[out, out],
        out_shape=[jax.ShapeDtypeStruct((T, LANES), F32)] * 2,
        compiler_params=_params(("arbitrary",), 0),
    )(pos, inv, mask_c, mask_s)


def _rope_nsa(x, c, s):
    return x * c + pltpu.roll(x, 64, 1) * s


def _rope_mla(x, c, s):
    return x * c + pltpu.roll(x, 32, 1) * s


def _mxu_row_sum(sq, sel):
    hi = sq.astype(BF16)
    lo = (sq - hi.astype(F32)).astype(BF16)
    return jnp.dot(hi, sel, preferred_element_type=F32) + jnp.dot(lo, sel, preferred_element_type=F32)


def _sel_matrix(shape, pred):
    r = lax.broadcasted_iota(jnp.int32, shape, 0)
    c = lax.broadcasted_iota(jnp.int32, shape, 1)
    return pred(r, c).astype(BF16)


def _rmsnorm_kernel(x_ref, g_ref, o_ref):
    x = x_ref[...]
    r = lax.rsqrt(jnp.mean(x * x, axis=-1, keepdims=True) + EPS)
    o_ref[...] = (x * r * g_ref[...]).astype(o_ref.dtype)


def _rmsnorm(x, g, *, layer=None):
    T, width = x.shape
    tm = min(T, 512)
    if layer is None:
        g = g.reshape(1, width)
        g_spec = pl.BlockSpec((1, width), lambda i: (0, 0))
    else:
        g = g.reshape(g.shape[0], 1, width)
        g_spec = pl.BlockSpec((None, 1, width), lambda i: (layer, 0, 0))
    return pl.pallas_call(
        _rmsnorm_kernel,
        name="rmsnorm",
        grid=(T // tm,),
        in_specs=[pl.BlockSpec((tm, width), lambda i: (i, 0)), g_spec],
        out_specs=pl.BlockSpec((tm, width), lambda i: (i, 0)),
        out_shape=jax.ShapeDtypeStruct((T, width), BF16),
        compiler_params=_params(("arbitrary",), 6 * tm * width * 4),
    )(x, g)


def _scale_rows(acc, rs):
    return jnp.concatenate([acc[:, k * LANES:(k + 1) * LANES] * rs for k in range(acc.shape[1] // LANES)], axis=1)


def _mm_kernel(n_w, has_rs, n_extra, n_out, epilogue, x_ref, *refs):
    refs = list(refs)
    rs_ref = refs.pop(0) if has_rs else None
    w_refs = refs[:n_w]
    extra = refs[n_w:n_w + n_extra]
    outs = refs[n_w + n_extra:n_w + n_extra + n_out]
    w_bf = refs[n_w + n_extra + n_out:2 * n_w + n_extra + n_out]
    scr = refs[2 * n_w + n_extra + n_out:]

    @pl.when(pl.program_id(1) == 0)
    def _():
        for w, s in zip(w_refs, w_bf):
            s[...] = w[...].astype(BF16)

    x = x_ref[...]
    accs = [jnp.dot(x, s[...], preferred_element_type=F32) for s in w_bf]
    if has_rs:
        rs = rs_ref[...]
        accs = [_scale_rows(a, rs) for a in accs]
    epilogue(accs, extra, outs, scr)


def _mm(act, weights, epilogue, extras, outs, *, n_cols, tn, tm=1024, scratch=()):
    x, row_scale = act
    M, K = x.shape
    tm = min(tm, M)
    assert M % tm == 0 and n_cols % tn == 0
    in_specs = [pl.BlockSpec((tm, K), lambda j, i: (i, 0))]
    args = [x]
    nbytes = 2 * tm * K * 2 + len(weights) * (2 * K * tn * 4 + K * tn * 2 + 2 * tm * tn * 4)
    if row_scale is not None:
        in_specs.append(pl.BlockSpec((tm, LANES), lambda j, i: (i, 0)))
        args.append(row_scale)
        nbytes += 2 * tm * LANES * 4
    for arr, layer, off in weights:
        if layer is None:
            in_specs.append(pl.BlockSpec((K, tn), lambda j, i, off=off: (0, j + off)))
        else:
            in_specs.append(pl.BlockSpec((None, K, tn), lambda j, i, off=off, layer=layer: (layer, 0, j + off)))
        args.append(arr)
    for arr, bs, im in extras:
        in_specs.append(pl.BlockSpec(bs, im))
        args.append(arr)
        nbytes += 2 * 4 * functools.reduce(lambda a, b: a * (b or 1), bs, 1)
    out_specs, out_shapes = [], []
    for sds, bs, im in outs:
        out_specs.append(pl.BlockSpec(bs, im))
        out_shapes.append(sds)
        nbytes += 2 * sds.dtype.itemsize * functools.reduce(lambda a, b: a * (b or 1), bs, 1)
    for shape, dtype in scratch:
        nbytes += jnp.dtype(dtype).itemsize * functools.reduce(lambda a, b: a * b, shape, 1)
    kern = functools.partial(_mm_kernel, len(weights), row_scale is not None, len(extras), len(outs), epilogue)
    res = pl.pallas_call(
        kern,
        name="mm_" + getattr(epilogue, "func", epilogue).__name__[len("_epi_"):],
        grid=(n_cols // tn, M // tm),
        in_specs=in_specs,
        out_specs=out_specs,
        out_shape=out_shapes,
        scratch_shapes=[pltpu.VMEM((K, tn), BF16) for _ in weights] + [pltpu.VMEM(s, d) for s, d in scratch],
        compiler_params=_params(("arbitrary", "arbitrary"), nbytes),
    )(*args)
    return res


def _epi_plain(accs, extra, outs, scr):
    outs[0][...] = accs[0].astype(outs[0].dtype)


def _epi_swiglu(accs, extra, outs, scr):
    g, u = accs
    outs[0][...] = (g * (1.0 / (1.0 + jnp.exp(-g))) * u).astype(outs[0].dtype)


def _epi_residual(alpha, accs, extra, outs, scr):
    outs[0][...] = extra[0][...] + alpha * accs[0]


def _epi_residual_norm(alpha, width, accs, extra, outs, scr):
    res_ref, g_ref = extra
    h_o, hg_o, rs_o = outs
    ss = scr[0]
    j = pl.program_id(0)
    i = pl.program_id(1)
    h = res_ref[...] + alpha * accs[0]
    h_o[...] = h
    hg_o[...] = (h * g_ref[...]).astype(hg_o.dtype)
    sq = h * h
    part = sq[:, :LANES]
    for k in range(1, sq.shape[1] // LANES):
        part = part + sq[:, k * LANES:(k + 1) * LANES]

    @pl.when(j == 0)
    def _():
        ss[i] = part

    @pl.when(j > 0)
    def _():
        ss[i] = ss[i] + part

    @pl.when(j == pl.num_programs(0) - 1)
    def _():
        tot = jnp.sum(ss[i], -1, keepdims=True)
        rs_o[...] = jnp.broadcast_to(lax.rsqrt(tot / width + EPS), rs_o.shape)


def _mm_plain(act, w, layer, col_off, n_cols, tn, dtype):
    M = act[0].shape[0]
    tm = min(1024, M)
    out = (jax.ShapeDtypeStruct((M, n_cols), dtype), (tm, tn), lambda j, i: (i, j))
    return _mm(act, [(w, layer, col_off)], _epi_plain, [], [out], n_cols=n_cols, tn=tn, tm=tm)[0]


def _mm_residual(x, w, layer, res, alpha, next_norm=None):
    M, N = res.shape
    K = x.shape[1]
    tn = 512 if N % 512 == 0 else 256
    tm = min(1024 if K <= 4096 else 512, M)
    blk = ((tm, tn), lambda j, i: (i, j))
    h_out = (jax.ShapeDtypeStruct((M, N), F32),) + blk
    if next_norm is None:
        return _mm((x, None), [(w, layer, 0)], functools.partial(_epi_residual, alpha), [(res,) + blk],
                   [h_out], n_cols=N, tn=tn, tm=tm)[0], None
    gains, g_layer = next_norm
    g_spec = (gains.reshape(gains.shape[0], 1, N), (None, 1, tn), lambda j, i: (g_layer, 0, j))
    last_j = N // tn - 1
    rs_out = (jax.ShapeDtypeStruct((M, LANES), F32), (tm, LANES), lambda j, i: (jnp.where(j == last_j, i, 0), 0))
    h, hg, rs = _mm((x, None), [(w, layer, 0)], functools.partial(_epi_residual_norm, alpha, N),
                    [(res,) + blk, g_spec], [h_out, (jax.ShapeDtypeStruct((M, N), BF16),) + blk, rs_out],
                    n_cols=N, tn=tn, tm=tm, scratch=[((M // tm, tm, LANES), F32)])
    return h, (hg, rs)


def _ffn(h, act, wg, wu, wd, layer, next_norm):
    M, D = h.shape
    F = wg.shape[-1]
    tm = min(1024, M)
    hid = _mm(act, [(wg, layer, 0), (wu, layer, 0)], _epi_swiglu, [],
              [(jax.ShapeDtypeStruct((M, F), BF16), (tm, 256), lambda j, i: (i, j))],
              n_cols=F, tn=256, tm=tm)[0]
    return _mm_residual(hid, wd, layer, h, 0.5, next_norm)


def _epi_mla_q(n_heads, out_scale, accs, extra, outs, scr):
    g = extra[0][...] * out_scale
    c, s = extra[1][...], extra[2][...]
    o = outs[0]
    sel = _sel_matrix((MLA_HEAD_PAD, MLA_HEAD_PAD), lambda r, col: r < MLA_QK)
    acc = accs[0]
    for hh in range(n_heads):
        b0 = hh * MLA_HEAD_PAD
        a = acc[:, b0:b0 + MLA_HEAD_PAD]
        r = lax.rsqrt(_mxu_row_sum(a * a, sel) / MLA_QK + EPS)
        an = a * r * g
        o[:, b0:b0 + LANES] = an[:, :LANES].astype(o.dtype)
        o[:, b0 + LANES:b0 + 2 * LANES] = _rope_mla(an[:, LANES:], c, s).astype(o.dtype)


def _epi_mla_kv(n_heads, accs, extra, outs, scr):
    g = extra[0][...]
    c, s = extra[1][...], extra[2][...]
    kr = extra[3][...]
    kr2 = kr * kr
    k_o, v_o = outs
    sel = _sel_matrix((MLA_HEAD_PAD, LANES), lambda r, col: r < MLA_QK)
    acc = accs[0]
    for hh in range(n_heads):
        b0 = hh * MLA_HEAD_PAD
        a = acc[:, b0:b0 + MLA_HEAD_PAD]
        kn = a[:, :LANES]
        ss = _mxu_row_sum(jnp.concatenate([kn * kn, kr2], axis=1), sel)
        r = lax.rsqrt(ss / MLA_QK + EPS)
        k_o[:, b0:b0 + LANES] = (kn * r * g[:, :LANES]).astype(k_o.dtype)
        k_o[:, b0 + LANES:b0 + 2 * LANES] = _rope_mla(kr * r * g[:, LANES:], c, s).astype(k_o.dtype)
        v_o[:, hh * MLA_V:(hh + 1) * MLA_V] = a[:, LANES:].astype(v_o.dtype)


def _pad_gain_mla(g):
    return jnp.concatenate([g, g[MLA_NOPE:]]).reshape(1, MLA_HEAD_PAD)


def _mla_attn_kernel(S, tq, q_ref, k_ref, v_ref, o_ref):
    row = lax.broadcasted_iota(jnp.int32, (tq, tq), 0)
    col = lax.broadcasted_iota(jnp.int32, (tq, tq), 1)
    tri = jnp.where(col <= row, 0.0, NEG)
    nt = (((1,), (1,)), ((), ()))
    for i in range(S // tq):
        r0 = i * tq
        q = q_ref[r0:r0 + tq, :]
        s_d = lax.dot_general(q, k_ref[r0:r0 + tq, :], nt, preferred_element_type=F32) + tri
        m = jnp.max(s_d, -1, keepdims=True)
        if i > 0:
            s_o = lax.dot_general(q, k_ref[0:r0, :], nt, preferred_element_type=F32)
            m = jnp.maximum(m, jnp.max(s_o, -1, keepdims=True))
        e_d = jnp.exp2(s_d - m)
        l = jnp.sum(e_d, -1, keepdims=True)
        o = jnp.dot(e_d.astype(BF16), v_ref[r0:r0 + tq, :], preferred_element_type=F32)
        if i > 0:
            e_o = jnp.exp2(s_o - m)
            l = l + jnp.sum(e_o, -1, keepdims=True)
            o = o + jnp.dot(e_o.astype(BF16), v_ref[0:r0, :], preferred_element_type=F32)
        o_ref[r0:r0 + tq, :] = (o / l).astype(o_ref.dtype)


def _mla_attention(q, k, v, B, S, H):
    T = B * S
    tq = min(ATTN_TQ, S)
    kern = functools.partial(_mla_attn_kernel, S, tq)
    nbytes = 2 * S * (2 * MLA_HEAD_PAD + 2 * MLA_V) * 2 + 6 * tq * S * 4
    return pl.pallas_call(
        kern,
        name="mla_attn",
        grid=(B, H),
        in_specs=[pl.BlockSpec((S, MLA_HEAD_PAD), lambda b, h: (b, h)),
                  pl.BlockSpec((S, MLA_HEAD_PAD), lambda b, h: (b, h)),
                  pl.BlockSpec((S, MLA_V), lambda b, h: (b, h))],
        out_specs=pl.BlockSpec((S, MLA_V), lambda b, h: (b, h)),
        out_shape=jax.ShapeDtypeStruct((T, H * MLA_V), BF16),
        compiler_params=_params(("arbitrary", "arbitrary"), nbytes),
    )(q, k, v)


def _mla_layer(h, act, a, B, S, tabs, mla_w_in, mla_g_cq, mla_g_ckv, mla_w_uq, mla_w_ukv, mla_g_q, mla_g_k, mla_w_o,
               next_norm):
    T, D = h.shape
    q_lora = mla_g_cq.shape[-1]
    kv_lora = mla_g_ckv.shape[-1]
    H = mla_w_ukv.shape[-1] // (MLA_NOPE + MLA_V)
    c_tab, s_tab = tabs
    tm = min(1024, T)
    lat = q_lora + kv_lora
    assert q_lora % kv_lora == 0
    cq = _mm_plain(act, mla_w_in, a, 0, q_lora, min(512, q_lora), F32)
    ckv = _mm_plain(act, mla_w_in, a, q_lora // kv_lora, kv_lora, kv_lora, F32)
    w_kr = lax.slice(mla_w_in, (a, 0, lat), (a + 1, D, lat + MLA_ROPE))[0]
    kr = _mm_plain(act, jnp.concatenate([w_kr, w_kr], axis=1), None, 0, LANES, LANES, F32)
    cq = _rmsnorm(cq, mla_g_cq[a])
    ckv = _rmsnorm(ckv, mla_g_ckv[a])
    w_uq = mla_w_uq[a].reshape(q_lora, H, MLA_QK)
    w_uq = jnp.concatenate([w_uq, w_uq[:, :, MLA_NOPE:]], axis=2).reshape(q_lora, H * MLA_HEAD_PAD)
    hp = 2 if H % 2 == 0 else 1
    tn = hp * MLA_HEAD_PAD
    tab_specs = [(t, (tm, LANES), lambda j, i: (i, 0)) for t in (c_tab, s_tab)]
    gain = lambda g: (_pad_gain_mla(g), (1, MLA_HEAD_PAD), lambda j, i: (0, 0))
    q = _mm((cq, None), [(w_uq, None, 0)], functools.partial(_epi_mla_q, hp, MLA_QK ** -0.5 * LOG2E),
            [gain(mla_g_q[a])] + tab_specs,
            [(jax.ShapeDtypeStruct((T, H * MLA_HEAD_PAD), BF16), (tm, tn), lambda j, i: (i, j))],
            n_cols=H * MLA_HEAD_PAD, tn=tn, tm=tm)[0]
    kr_spec = (kr, (tm, LANES), lambda j, i: (i, 0))
    k, v = _mm((ckv, None), [(mla_w_ukv, a, 0)], functools.partial(_epi_mla_kv, hp),
               [gain(mla_g_k[a])] + tab_specs + [kr_spec],
               [(jax.ShapeDtypeStruct((T, H * MLA_HEAD_PAD), BF16), (tm, tn), lambda j, i: (i, j)),
                (jax.ShapeDtypeStruct((T, H * MLA_V), BF16), (tm, hp * MLA_V), lambda j, i: (i, j))],
               n_cols=H * MLA_HEAD_PAD, tn=tn, tm=tm)
    o = _mla_attention(q, k, v, B, S, H)
    return _mm_residual(o, mla_w_o, a, h, 1.0, next_norm)


def _nsa_head_norm(a, g2):
    sel = _sel_matrix((MXU_DIM, MXU_DIM), lambda r, col: r // NSA_DH == col // NSA_DH)
    r = lax.rsqrt(_mxu_row_sum(a * a, sel) / NSA_DH + EPS)
    return a * r * g2


def _epi_nsa_k(n_pairs, accs, extra, outs, scr):
    g = extra[0][...]
    g2 = jnp.concatenate([g, g], axis=1)
    c, s = extra[1][...], extra[2][...]
    o = outs[0]
    acc = accs[0]
    for pp in range(n_pairs):
        kn = _nsa_head_norm(acc[:, pp * MXU_DIM:(pp + 1) * MXU_DIM], g2)
        for hh in range(2):
            col = (2 * pp + hh) * NSA_DH
            o[:, col:col + NSA_DH] = _rope_nsa(kn[:, hh * NSA_DH:(hh + 1) * NSA_DH], c, s).astype(o.dtype)


def _compress_kernel(n_cmp, k_ref, v_ref, pk_ref, pv_ref, kw1_ref, kb1_ref, kw2_ref,
                     vw1_ref, vb1_ref, vw2_ref, gk_ref, ko_ref, vo_ref):
    half = CMP_STRIDE * NSA_DH
    n_rows = ko_ref.shape[0]
    row = lax.broadcasted_iota(jnp.int32, (n_rows, 1), 0)

    def phi(t_ref, pos_ref, w1_ref, b1_ref, w2_ref):
        slabs = [t_ref[pl.ds(l, n_rows, stride=CMP_STRIDE), :] for l in range(CMP_STRIDE)]
        lo = jnp.concatenate([slabs[l] + pos_ref[:, l * NSA_DH:(l + 1) * NSA_DH]
                              for l in range(CMP_STRIDE)], axis=1).astype(BF16)
        hi = jnp.concatenate([slabs[l] + pos_ref[:, half + l * NSA_DH:half + (l + 1) * NSA_DH]
                              for l in range(CMP_STRIDE)], axis=1).astype(BF16)
        p_lo = jnp.dot(lo, w1_ref[0:half, :].astype(BF16), preferred_element_type=F32)
        p_hi = jnp.dot(hi, w1_ref[half:2 * half, :].astype(BF16), preferred_element_type=F32)
        pre = p_lo + pltpu.roll(p_hi, n_rows - 1, 0) + b1_ref[...]
        hid = pre * (1.0 / (1.0 + jnp.exp(-pre)))
        out = jnp.dot(hid.astype(BF16), w2_ref[...].astype(BF16), preferred_element_type=F32)
        return jnp.where(row < n_cmp, out, 0.0)

    kc = phi(k_ref, pk_ref, kw1_ref, kb1_ref, kw2_ref)
    r = lax.rsqrt(jnp.mean(kc * kc, -1, keepdims=True) + EPS)
    ko_ref[...] = (kc * r * gk_ref[...]).astype(ko_ref.dtype)
    vo_ref[...] = phi(v_ref, pv_ref, vw1_ref, vb1_ref, vw2_ref).astype(vo_ref.dtype)


def _compress(raw, B, S, G, cmp_pos_k, cmp_pos_v, k_w1, k_b1, k_w2, v_w1, v_b1, v_w2, g_k_cmp):
    n_cmp = (S - CMP_BLOCK) // CMP_STRIDE + 1
    n_rows = S // CMP_STRIDE
    hid = k_w1.shape[-1]
    full = lambda shape: pl.BlockSpec(shape, lambda b, g: (0,) * len(shape))
    w_specs = [full((CMP_BLOCK * NSA_DH, hid)), full((1, hid)), full((hid, NSA_DH))]
    out_spec = pl.BlockSpec((None, None, n_rows, NSA_DH), lambda b, g: (b, g, 0, 0))
    out_sds = jax.ShapeDtypeStruct((B, G, n_rows, NSA_DH), BF16)
    nbytes = 4 * (2 * CMP_BLOCK * NSA_DH * hid * 4) + 4 * S * NSA_DH * 4 + 8 * n_rows * CMP_BLOCK * NSA_DH * 4
    return pl.pallas_call(
        functools.partial(_compress_kernel, n_cmp),
        name="nsa_compress",
        grid=(B, G),
        in_specs=[pl.BlockSpec((S, NSA_DH), lambda b, g: (b, g)),
                  pl.BlockSpec((S, NSA_DH), lambda b, g: (b, G + g)),
                  full((1, CMP_BLOCK * NSA_DH)), full((1, CMP_BLOCK * NSA_DH))]
                 + w_specs + w_specs + [full((1, NSA_DH))],
        out_specs=[out_spec, out_spec],
        out_shape=[out_sds, out_sds],
        compiler_params=_params(("arbitrary", "arbitrary"), nbytes),
    )(raw, raw, cmp_pos_k.reshape(1, -1), cmp_pos_v.reshape(1, -1),
      k_w1, k_b1.reshape(1, -1), k_w2, v_w1, v_b1.reshape(1, -1), v_w2, g_k_cmp.reshape(1, -1))


def _nsa_shared_kv(h, B, S, tabs, kv_norm, kv_w, cmp_pos_k, cmp_pos_v, cmp_k_w1, cmp_k_b1, cmp_k_w2,
                   cmp_v_w1, cmp_v_b1, cmp_v_w2, g_k_cmp, g_k_slc, g_k_win):
    T = h.shape[0]
    G = kv_w.shape[-1] // (2 * N_BRANCH * NSA_DH)
    assert G % 2 == 0
    part = G * NSA_DH
    tm = min(1024, T)
    c_tab, s_tab = tabs
    y = (_rmsnorm(h, kv_norm), None)
    raw = _mm_plain(y, kv_w, None, 0, 2 * part, part, F32)
    k_cmp, v_cmp = _compress(raw, B, S, G, cmp_pos_k, cmp_pos_v, cmp_k_w1, cmp_k_b1, cmp_k_w2,
                             cmp_v_w1, cmp_v_b1, cmp_v_w2, g_k_cmp)

    def k_branch(part_idx, gain):
        extras = [(gain.reshape(1, NSA_DH), (1, NSA_DH), lambda j, i: (0, 0)),
                  (c_tab, (tm, LANES), lambda j, i: (i, 0)), (s_tab, (tm, LANES), lambda j, i: (i, 0))]
        return _mm(y, [(kv_w, None, part_idx)], functools.partial(_epi_nsa_k, G // 2), extras,
                   [(jax.ShapeDtypeStruct((T, part), BF16), (tm, part), lambda j, i: (i, j))],
                   n_cols=part, tn=part, tm=tm)[0]

    k_slc = k_branch(2, g_k_slc)
    v_slc = _mm_plain(y, kv_w, None, 3, part, part, BF16)
    k_win = k_branch(4, g_k_win)
    v_win = _mm_plain(y, kv_w, None, 5, part, part, BF16)
    return k_cmp, v_cmp, k_slc, v_slc, k_win, v_win


def _epi_nsa_q(n_pairs, out_scale, accs, extra, outs, scr):
    g = extra[0][...] * out_scale
    g2 = jnp.concatenate([g, g], axis=1)
    c, s = extra[1][...], extra[2][...]
    q_o, qr_o = outs
    acc = accs[0]
    for pp in range(n_pairs):
        qn = _nsa_head_norm(acc[:, pp * MXU_DIM:(pp + 1) * MXU_DIM], g2)
        for hh in range(2):
            x = qn[:, hh * NSA_DH:(hh + 1) * NSA_DH]
            q_o[2 * pp + hh] = x.astype(q_o.dtype)
            qr_o[2 * pp + hh] = _rope_nsa(x, c, s).astype(qr_o.dtype)


def _softmax2_pv(s, v):
    m = jnp.max(s, -1, keepdims=True)
    e = jnp.exp2(s - m)
    l = jnp.sum(e, -1, keepdims=True)
    return jnp.dot(e.astype(BF16), v, preferred_element_type=F32) / l


def _nsa_attn_kernel(S, tq, hpg, n_cmp,
                     q_ref, qr_ref, gl_ref, bg_ref, kc_ref, vc_ref, ks_ref, vs_ref, kw_ref, vw_ref,
                     o_ref, ocmp_scr, obuf_scr):
    g_idx = pl.program_id(1)
    n_cr = kc_ref.shape[0]
    n_slc = S // SLC_BLOCK
    n_sel = min(SLC_TOPK, n_slc)
    k_cmp = kc_ref[...]
    v_cmp = vc_ref[...]
    lane = lax.broadcasted_iota(jnp.int32, (tq, LANES), 1)
    nt = (((1,), (1,)), ((), ()))

    jj = lax.broadcasted_iota(jnp.int32, (n_slc, n_cr), 0)
    nn = lax.broadcasted_iota(jnp.int32, (n_slc, n_cr), 1)
    cs = nn * CMP_STRIDE
    ss = jj * SLC_BLOCK
    ov = jnp.maximum(jnp.minimum(cs + CMP_BLOCK, ss + SLC_BLOCK) - jnp.maximum(cs, ss), 0)
    agg_t = jnp.where(nn < n_cmp, ov.astype(F32) / CMP_STRIDE, 0.0)

    for i in range(S // tq):
        r0 = i * tq
        nk = r0 + tq
        rows = slice(r0, r0 + tq)

        spos_c = r0 + lax.broadcasted_iota(jnp.int32, (tq, n_cr), 0)
        ncol = lax.broadcasted_iota(jnp.int32, (tq, n_cr), 1)
        cmask = (ncol * CMP_STRIDE + (CMP_BLOCK - 1) <= spos_c) & (ncol < n_cmp)
        cbias = jnp.where(cmask, 0.0, NEG)
        cmask_f = cmask.astype(F32)
        q_all = q_ref[:, rows, :].reshape(hpg * tq, NSA_DH)
        sc = lax.dot_general(q_all, k_cmp, nt, preferred_element_type=F32).reshape(hpg, tq, n_cr) + cbias
        m = jnp.max(sc, -1, keepdims=True)
        e = jnp.exp2(sc - m) * cmask_f
        l = jnp.sum(e, -1, keepdims=True)
        p = e / jnp.where(l > 0.0, l, 1.0)
        ocmp_scr[...] = jnp.dot(p.reshape(hpg * tq, n_cr).astype(BF16), v_cmp,
                                preferred_element_type=F32).reshape(hpg, tq, NSA_DH)
        psum = jnp.sum(p, axis=0)

        imp = lax.dot_general(agg_t, psum, nt, precision=lax.Precision.HIGHEST,
                              preferred_element_type=F32)
        jb = lax.broadcasted_iota(jnp.int32, (n_slc, tq), 0)
        sp = r0 + lax.broadcasted_iota(jnp.int32, (n_slc, tq), 1)
        cur = sp // SLC_BLOCK
        valid = jb * SLC_BLOCK <= sp
        forced = (jb == 0) | (jb == cur) | (jb == cur - 1)
        imp = jnp.where(forced, FORCE, jnp.where(valid, imp, -1.0))
        rank = jnp.zeros((n_slc, tq), jnp.int32)
        for jp in range(n_slc):
            other = imp[jp:jp + 1, :]
            ahead = (other > imp) | ((other == imp) & (jp < jb))
            rank = rank + ahead.astype(jnp.int32)
        sel_t = (rank < n_sel).astype(BF16)

        eb = lax.broadcasted_iota(jnp.int32, (n_slc, nk), 0)
        ek = lax.broadcasted_iota(jnp.int32, (n_slc, nk), 1)
        expand = (ek // SLC_BLOCK == eb).astype(BF16)
        sel_keys = lax.dot_general(sel_t, expand, (((0,), (0,)), ((), ())), preferred_element_type=F32)
        qrow = r0 + lax.broadcasted_iota(jnp.int32, (tq, nk), 0)
        kcol = lax.broadcasted_iota(jnp.int32, (tq, nk), 1)
        sbias = jnp.where((sel_keys > 0.5) & (kcol <= qrow), 0.0, NEG)

        w0 = (max(0, r0 - WINDOW) // tq) * tq
        nw = nk - w0
        wrow = r0 + lax.broadcasted_iota(jnp.int32, (tq, nw), 0)
        wcol = w0 + lax.broadcasted_iota(jnp.int32, (tq, nw), 1)
        wdiff = wrow - wcol
        wbias = jnp.where((wdiff >= 0) & (wdiff < WINDOW), 0.0, NEG)

        z = gl_ref[rows, :] + bg_ref[...]
        gates = 1.0 / (1.0 + jnp.exp(-z))

        k_s = ks_ref[0:nk, :]
        v_s = vs_ref[0:nk, :]
        k_w = kw_ref[w0:nk, :]
        v_w = vw_ref[w0:nk, :]

        def main_body(hh, carry):
            qr = qr_ref[hh, rows, :]
            o_s = _softmax2_pv(lax.dot_general(qr, k_s, nt, preferred_element_type=F32) + sbias, v_s)
            o_w = _softmax2_pv(lax.dot_general(qr, k_w, nt, preferred_element_type=F32) + wbias, v_w)
            c0 = (g_idx * hpg + hh) * N_BRANCH
            gate = lambda br: jnp.sum(jnp.where(lane == c0 + br, gates, 0.0), -1, keepdims=True)
            o = gate(0) * ocmp_scr[hh] + gate(1) * o_s + gate(2) * o_w
            obuf_scr[hh] = o.astype(obuf_scr.dtype)
            return carry

        lax.fori_loop(0, hpg, main_body, 0, unroll=2)
        for hh in range(hpg):
            o_ref[rows, hh * NSA_DH:(hh + 1) * NSA_DH] = obuf_scr[hh]


def _nsa_attention(q, qr, gate_logits, b_gate, shared, B, S, G, hpg):
    k_cmp, v_cmp, k_slc, v_slc, k_win, v_win = shared
    T = B * S
    H = G * hpg
    tq = min(ATTN_TQ, S)
    n_cmp = (S - CMP_BLOCK) // CMP_STRIDE + 1
    n_cr = k_cmp.shape[2]
    kern = functools.partial(_nsa_attn_kernel, S, tq, hpg, n_cmp)
    head_spec = pl.BlockSpec((hpg, S, NSA_DH), lambda b, g: (g, b, 0))
    cmp_spec = pl.BlockSpec((None, None, n_cr, NSA_DH), lambda b, g: (b, g, 0, 0))
    kv_spec = pl.BlockSpec((S, NSA_DH), lambda b, g: (b, g))
    nbytes = (2 * 2 * hpg * S * NSA_DH * 2 + 2 * S * LANES * 4 + 8 * S * NSA_DH * 2
              + 2 * S * hpg * NSA_DH * 2 + 10 * tq * S * 4)
    return pl.pallas_call(
        kern,
        name="nsa_attn",
        grid=(B, G),
        in_specs=[head_spec, head_spec,
                  pl.BlockSpec((S, LANES), lambda b, g: (b, 0)),
                  pl.BlockSpec((1, LANES), lambda b, g: (0, 0)),
                  cmp_spec, cmp_spec, kv_spec, kv_spec, kv_spec, kv_spec],
        out_specs=pl.BlockSpec((S, hpg * NSA_DH), lambda b, g: (b, g)),
        out_shape=jax.ShapeDtypeStruct((T, H * NSA_DH), BF16),
        scratch_shapes=[pltpu.VMEM((hpg, tq, NSA_DH), F32), pltpu.VMEM((hpg, tq, NSA_DH), BF16)],
        compiler_params=_params(("arbitrary", "arbitrary"), nbytes),
    )(q, qr, gate_logits, b_gate, k_cmp, v_cmp, k_slc, v_slc, k_win, v_win)


def _nsa_layer(h, u, b, B, S, G, tabs, shared, nsa_w_in, nsa_b_gate, nsa_g_q, nsa_w_o, next_norm):
    T, D = h.shape
    n_gate = nsa_b_gate.shape[-1]
    H = n_gate // N_BRANCH
    hpg = H // G
    q_width = H * NSA_DH
    assert n_gate <= LANES and H % 2 == 0
    c_tab, s_tab = tabs
    tm = min(1024, T)
    hp = min(4, H)
    tn = hp * NSA_DH
    extras = [(nsa_g_q[b].reshape(1, NSA_DH), (1, NSA_DH), lambda j, i: (0, 0)),
              (c_tab, (tm, LANES), lambda j, i: (i, 0)), (s_tab, (tm, LANES), lambda j, i: (i, 0))]
    head_out = (jax.ShapeDtypeStruct((H, T, NSA_DH), BF16), (hp, tm, NSA_DH), lambda j, i: (j, i, 0))
    q, qr = _mm(u, [(nsa_w_in, b, 0)], functools.partial(_epi_nsa_q, hp // 2, NSA_DH ** -0.5 * LOG2E),
                extras, [head_out, head_out], n_cols=q_width, tn=tn, tm=tm)
    w_gate = lax.slice(nsa_w_in, (b, 0, q_width), (b + 1, D, q_width + n_gate))[0]
    w_gate = jnp.pad(w_gate, ((0, 0), (0, LANES - n_gate)))
    gate_logits = _mm_plain(u, w_gate, None, 0, LANES, LANES, F32)
    b_gate = jnp.pad(nsa_b_gate[b], (0, LANES - n_gate)).reshape(1, LANES)
    o = _nsa_attention(q, qr, gate_logits, b_gate, shared, B, S, G, hpg)
    return _mm_residual(o, nsa_w_o, b, h, 1.0, next_norm)


def _lane_table(values):
    return jnp.asarray(values, F32).reshape(1, LANES)


def kernel(x, positions, ffn1_norm, ffn1_w_gate, ffn1_w_up, ffn1_w_down, mix_norm, ffn2_norm, ffn2_w_gate, ffn2_w_up, ffn2_w_down, mla_w_in, mla_g_cq, mla_g_ckv, mla_w_uq, mla_w_ukv, mla_g_q, mla_g_k, mla_w_o, kv_norm, kv_w, cmp_pos_k, cmp_pos_v, cmp_k_w1, cmp_k_b1, cmp_k_w2, cmp_v_w1, cmp_v_b1, cmp_v_w2, g_k_cmp, g_k_slc, g_k_win, nsa_w_in, nsa_b_gate, nsa_g_q, nsa_w_o):
    B, S, D = x.shape
    T = B * S
    depth = ffn1_norm.shape[0]
    n_a = mla_w_in.shape[0]
    G = kv_w.shape[-1] // (2 * N_BRANCH * NSA_DH)

    pos = positions.reshape(T, 1).astype(F32)
    half_a = MLA_ROPE // 2
    inv_a = jnp.power(ROPE_THETA, -jnp.arange(0, MLA_ROPE, 2, dtype=F32) / MLA_ROPE)
    zeros_a = jnp.zeros((LANES - MLA_ROPE,), F32)
    ones_h = jnp.ones((half_a,), F32)
    tabs_a = _rope_tables(pos, _lane_table(jnp.concatenate([inv_a, inv_a, zeros_a])),
                          _lane_table(jnp.concatenate([ones_h, ones_h, zeros_a])),
                          _lane_table(jnp.concatenate([-ones_h, ones_h, zeros_a])))
    inv_b = jnp.power(ROPE_THETA, -jnp.arange(0, NSA_DH, 2, dtype=F32) / NSA_DH)
    ones_b = jnp.ones((NSA_DH // 2,), F32)
    tabs_b = _rope_tables(pos, _lane_table(jnp.concatenate([inv_b, inv_b])),
                          _lane_table(jnp.concatenate([ones_b, ones_b])),
                          _lane_table(jnp.concatenate([-ones_b, ones_b])))

    h = x.reshape(T, D)
    shared = None
    act = (_rmsnorm(h, ffn1_norm, layer=0), None)
    for layer in range(depth):
        h, act = _ffn(h, act, ffn1_w_gate, ffn1_w_up, ffn1_w_down, layer, (mix_norm, layer))
        if layer < n_a:
            h, act = _mla_layer(h, act, layer, B, S, tabs_a, mla_w_in, mla_g_cq, mla_g_ckv, mla_w_uq, mla_w_ukv,
                                mla_g_q, mla_g_k, mla_w_o, (ffn2_norm, layer))
        else:
            h, act = _nsa_layer(h, act, layer - n_a, B, S, G, tabs_b, shared, nsa_w_in, nsa_b_gate, nsa_g_q,
                                nsa_w_o, (ffn2_norm, layer))
        h, act = _ffn(h, act, ffn2_w_gate, ffn2_w_up, ffn2_w_down, layer,
                      (ffn1_norm, layer + 1) if layer + 1 < depth else None)
        if layer == n_a - 1:
            shared = _nsa_shared_kv(h, B, S, tabs_b, kv_norm, kv_w, cmp_pos_k, cmp_pos_v, cmp_k_w1, cmp_k_b1,
                                    cmp_k_w2, cmp_v_w1, cmp_v_b1, cmp_v_w2, g_k_cmp, g_k_slc, g_k_win)
    return h.reshape(B, S, D)
```

```python
import functools
import math

import jax
import jax.numpy as jnp
from jax import lax
from jax.experimental import pallas as pl
from jax.experimental.pallas import tpu as pltpu

F32 = jnp.float32
BF16 = jnp.bfloat16

ROPE_THETA = 10000.0
EPS = 1e-6
NEG = -1e30
FORCE = 1e6
MLA_NOPE = 128
MLA_ROPE = 64
MLA_V = 128
MLA_QK = MLA_NOPE + MLA_ROPE
MLA_HEAD_PAD = 256
NSA_DH = 128
N_BRANCH = 3
CMP_BLOCK = 32
CMP_STRIDE = 16
SLC_BLOCK = 64
SLC_TOPK = 16
WINDOW = 512
LOG2E = math.log2(math.e)

LANES = 128
MXU_DIM = 256
V7X_VMEM_BYTES = 64 * 1024 * 1024
VMEM_CAP = V7X_VMEM_BYTES - 8 * 1024 * 1024

ATTN_TQ = 256
MLA_SCORES_AHEAD = 2


def _vmem_limit(nbytes):
    return int(min(VMEM_CAP, max(32 * 1024 * 1024, nbytes + 8 * 1024 * 1024)))


def _params(sem, nbytes):
    return pltpu.CompilerParams(dimension_semantics=sem, vmem_limit_bytes=_vmem_limit(nbytes))


def _rope_table_kernel(pos_ref, inv_ref, mc_ref, ms_ref, c_ref, s_ref):
    ang = pos_ref[...] * inv_ref[...]
    c_ref[...] = jnp.cos(ang) * mc_ref[...]
    s_ref[...] = jnp.sin(ang) * ms_ref[...]


def _rope_tables(pos, inv, mask_c, mask_s):
    T = pos.shape[0]
    tm = min(T, 1024)
    row = pl.BlockSpec((1, LANES), lambda i: (0, 0))
    out = pl.BlockSpec((tm, LANES), lambda i: (i, 0))
    return pl.pallas_call(
        _rope_table_kernel,
        name="rope_tables",
        grid=(T // tm,),
        in_specs=[pl.BlockSpec((tm, 1), lambda i: (i, 0)), row, row, row],
        out_specs=[out, out],
        out_shape=[jax.ShapeDtypeStruct((T, LANES), F32)] * 2,
        compiler_params=_params(("arbitrary",), 0),
    )(pos, inv, mask_c, mask_s)


def _rope_nsa(x, c, s):
    return x * c + pltpu.roll(x, 64, 1) * s


def _rope_mla(x, c, s):
    return x * c + pltpu.roll(x, 32, 1) * s


def _mxu_row_sum(sq, sel):
    hi = sq.astype(BF16)
    lo = (sq - hi.astype(F32)).astype(BF16)
    return jnp.dot(hi, sel, preferred_element_type=F32) + jnp.dot(lo, sel, preferred_element_type=F32)


def _sel_matrix(shape, pred):
    r = lax.broadcasted_iota(jnp.int32, shape, 0)
    c = lax.broadcasted_iota(jnp.int32, shape, 1)
    return pred(r, c).astype(BF16)


def _rmsnorm_kernel(x_ref, g_ref, o_ref):
    x = x_ref[...]
    r = lax.rsqrt(jnp.mean(x * x, axis=-1, keepdims=True) + EPS)
    o_ref[...] = (x * r * g_ref[...]).astype(o_ref.dtype)


def _rmsnorm(x, g, *, layer=None):
    T, width = x.shape
    tm = min(T, 512)
    if layer is None:
        g = g.reshape(1, width)
        g_spec = pl.BlockSpec((1, width), lambda i: (0, 0))
    else:
        g = g.reshape(g.shape[0], 1, width)
        g_spec = pl.BlockSpec((None, 1, width), lambda i: (layer, 0, 0))
    return pl.pallas_call(
        _rmsnorm_kernel,
        name="rmsnorm",
        grid=(T // tm,),
        in_specs=[pl.BlockSpec((tm, width), lambda i: (i, 0)), g_spec],
        out_specs=pl.BlockSpec((tm, width), lambda i: (i, 0)),
        out_shape=jax.ShapeDtypeStruct((T, width), BF16),
        compiler_params=_params(("arbitrary",), 6 * tm * width * 4),
    )(x, g)


def _scale_rows(acc, rs):
    return jnp.concatenate([acc[:, k * LANES:(k + 1) * LANES] * rs for k in range(acc.shape[1] // LANES)], axis=1)


def _mm_kernel(n_w, has_rs, n_extra, n_out, epilogue, x_ref, *refs):
    refs = list(refs)
    rs_ref = refs.pop(0) if has_rs else None
    w_refs = refs[:n_w]
    extra = refs[n_w:n_w + n_extra]
    outs = refs[n_w + n_extra:n_w + n_extra + n_out]
    w_bf = refs[n_w + n_extra + n_out:2 * n_w + n_extra + n_out]
    scr = refs[2 * n_w + n_extra + n_out:]

    @pl.when(pl.program_id(1) == 0)
    def _():
        for w, s in zip(w_refs, w_bf):
            s[...] = w[...].astype(BF16)

    x = x_ref[...]
    accs = [jnp.dot(x, s[...], preferred_element_type=F32) for s in w_bf]
    if has_rs:
        rs = rs_ref[...]
        accs = [_scale_rows(a, rs) for a in accs]
    epilogue(accs, extra, outs, scr)


def _mm(act, weights, epilogue, extras, outs, *, n_cols, tn, tm=1024, scratch=()):
    x, row_scale = act
    M, K = x.shape
    tm = min(tm, M)
    assert M % tm == 0 and n_cols % tn == 0
    in_specs = [pl.BlockSpec((tm, K), lambda j, i: (i, 0))]
    args = [x]
    nbytes = 2 * tm * K * 2 + len(weights) * (2 * K * tn * 4 + K * tn * 2 + 2 * tm * tn * 4)
    if row_scale is not None:
        in_specs.append(pl.BlockSpec((tm, LANES), lambda j, i: (i, 0)))
        args.append(row_scale)
        nbytes += 2 * tm * LANES * 4
    for arr, layer, off in weights:
        if layer is None:
            in_specs.append(pl.BlockSpec((K, tn), lambda j, i, off=off: (0, j + off)))
        else:
            in_specs.append(pl.BlockSpec((None, K, tn), lambda j, i, off=off, layer=layer: (layer, 0, j + off)))
        args.append(arr)
    for arr, bs, im in extras:
        in_specs.append(pl.BlockSpec(bs, im))
        args.append(arr)
        nbytes += 2 * 4 * functools.reduce(lambda a, b: a * (b or 1), bs, 1)
    out_specs, out_shapes = [], []
    for sds, bs, im in outs:
        out_specs.append(pl.BlockSpec(bs, im))
        out_shapes.append(sds)
        nbytes += 2 * sds.dtype.itemsize * functools.reduce(lambda a, b: a * (b or 1), bs, 1)
    for shape, dtype in scratch:
        nbytes += jnp.dtype(dtype).itemsize * functools.reduce(lambda a, b: a * b, shape, 1)
    kern = functools.partial(_mm_kernel, len(weights), row_scale is not None, len(extras), len(outs), epilogue)
    res = pl.pallas_call(
        kern,
        name="mm_" + getattr(epilogue, "func", epilogue).__name__[len("_epi_"):],
        grid=(n_cols // tn, M // tm),
        in_specs=in_specs,
        out_specs=out_specs,
        out_shape=out_shapes,
        scratch_shapes=[pltpu.VMEM((K, tn), BF16) for _ in weights] + [pltpu.VMEM(s, d) for s, d in scratch],
        compiler_params=_params(("arbitrary", "arbitrary"), nbytes),
    )(*args)
    return res


def _epi_plain(accs, extra, outs, scr):
    outs[0][...] = accs[0].astype(outs[0].dtype)


def _epi_swiglu(accs, extra, outs, scr):
    g, u = accs
    outs[0][...] = (g * (1.0 / (1.0 + jnp.exp(-g))) * u).astype(outs[0].dtype)


def _epi_residual(alpha, accs, extra, outs, scr):
    outs[0][...] = extra[0][...] + alpha * accs[0]


def _epi_residual_norm(alpha, width, accs, extra, outs, scr):
    res_ref, g_ref = extra
    h_o, hg_o, rs_o = outs
    ss = scr[0]
    j = pl.program_id(0)
    i = pl.program_id(1)
    h = res_ref[...] + alpha * accs[0]
    h_o[...] = h
    hg_o[...] = (h * g_ref[...]).astype(hg_o.dtype)
    sq = h * h
    part = sq[:, :LANES]
    for k in range(1, sq.shape[1] // LANES):
        part = part + sq[:, k * LANES:(k + 1) * LANES]

    @pl.when(j == 0)
    def _():
        ss[i] = part

    @pl.when(j > 0)
    def _():
        ss[i] = ss[i] + part

    @pl.when(j == pl.num_programs(0) - 1)
    def _():
        tot = jnp.sum(ss[i], -1, keepdims=True)
        rs_o[...] = jnp.broadcast_to(lax.rsqrt(tot / width + EPS), rs_o.shape)


def _mm_plain(act, w, layer, col_off, n_cols, tn, dtype):
    M = act[0].shape[0]
    tm = min(1024, M)
    out = (jax.ShapeDtypeStruct((M, n_cols), dtype), (tm, tn), lambda j, i: (i, j))
    return _mm(act, [(w, layer, col_off)], _epi_plain, [], [out], n_cols=n_cols, tn=tn, tm=tm)[0]


def _mm_residual(x, w, layer, res, alpha, next_norm=None):
    M, N = res.shape
    K = x.shape[1]
    tn = 512 if N % 512 == 0 else 256
    tm = min(1024 if K <= 4096 else 512, M)
    blk = ((tm, tn), lambda j, i: (i, j))
    h_out = (jax.ShapeDtypeStruct((M, N), F32),) + blk
    if next_norm is None:
        return _mm((x, None), [(w, layer, 0)], functools.partial(_epi_residual, alpha), [(res,) + blk],
                   [h_out], n_cols=N, tn=tn, tm=tm)[0], None
    gains, g_layer = next_norm
    g_spec = (gains.reshape(gains.shape[0], 1, N), (None, 1, tn), lambda j, i: (g_layer, 0, j))
    last_j = N // tn - 1
    rs_out = (jax.ShapeDtypeStruct((M, LANES), F32), (tm, LANES), lambda j, i: (jnp.where(j == last_j, i, 0), 0))
    h, hg, rs = _mm((x, None), [(w, layer, 0)], functools.partial(_epi_residual_norm, alpha, N),
                    [(res,) + blk, g_spec], [h_out, (jax.ShapeDtypeStruct((M, N), BF16),) + blk, rs_out],
                    n_cols=N, tn=tn, tm=tm, scratch=[((M // tm, tm, LANES), F32)])
    return h, (hg, rs)


def _ffn(h, act, wg, wu, wd, layer, next_norm):
    M, D = h.shape
    F = wg.shape[-1]
    tm = min(1024, M)
    hid = _mm(act, [(wg, layer, 0), (wu, layer, 0)], _epi_swiglu, [],
              [(jax.ShapeDtypeStruct((M, F), BF16), (tm, 256), lambda j, i: (i, j))],
              n_cols=F, tn=256, tm=tm)[0]
    return _mm_residual(hid, wd, layer, h, 0.5, next_norm)


def _epi_mla_q(n_heads, out_scale, accs, extra, outs, scr):
    g = extra[0][...] * out_scale
    c, s = extra[1][...], extra[2][...]
    o = outs[0]
    sel = _sel_matrix((MLA_HEAD_PAD, MLA_HEAD_PAD), lambda r, col: r < MLA_QK)
    acc = accs[0]
    for hh in range(n_heads):
        b0 = hh * MLA_HEAD_PAD
        a = acc[:, b0:b0 + MLA_HEAD_PAD]
        r = lax.rsqrt(_mxu_row_sum(a * a, sel) / MLA_QK + EPS)
        an = a * r * g
        o[:, b0:b0 + LANES] = an[:, :LANES].astype(o.dtype)
        o[:, b0 + LANES:b0 + 2 * LANES] = _rope_mla(an[:, LANES:], c, s).astype(o.dtype)


def _epi_mla_kv(n_heads, accs, extra, outs, scr):
    g = extra[0][...]
    c, s = extra[1][...], extra[2][...]
    kr = extra[3][...]
    kr2 = kr * kr
    k_o, v_o = outs
    sel = _sel_matrix((MLA_HEAD_PAD, LANES), lambda r, col: r < MLA_QK)
    acc = accs[0]
    for hh in range(n_heads):
        b0 = hh * MLA_HEAD_PAD
        a = acc[:, b0:b0 + MLA_HEAD_PAD]
        kn = a[:, :LANES]
        ss = _mxu_row_sum(jnp.concatenate([kn * kn, kr2], axis=1), sel)
        r = lax.rsqrt(ss / MLA_QK + EPS)
        k_o[:, b0:b0 + LANES] = (kn * r * g[:, :LANES]).astype(k_o.dtype)
        k_o[:, b0 + LANES:b0 + 2 * LANES] = _rope_mla(kr * r * g[:, LANES:], c, s).astype(k_o.dtype)
        v_o[:, hh * MLA_V:(hh + 1) * MLA_V] = a[:, LANES:].astype(v_o.dtype)


def _pad_gain_mla(g):
    return jnp.concatenate([g, g[MLA_NOPE:]]).reshape(1, MLA_HEAD_PAD)


def _mla_attn_kernel(S, tq, nh, q_ref, k_ref, v_ref, o_ref):
    row = lax.broadcasted_iota(jnp.int32, (tq, tq), 0)
    col = lax.broadcasted_iota(jnp.int32, (tq, tq), 1)
    tri = jnp.where(col <= row, 0.0, NEG)
    nt = (((1,), (1,)), ((), ()))

    def scores(i, hh):
        r0 = i * tq
        ks = slice(hh * MLA_HEAD_PAD, (hh + 1) * MLA_HEAD_PAD)
        q = q_ref[r0:r0 + tq, ks]
        s_d = lax.dot_general(q, k_ref[r0:r0 + tq, ks], nt, preferred_element_type=F32) + tri
        s_o = lax.dot_general(q, k_ref[0:r0, ks], nt, preferred_element_type=F32) if i > 0 else None
        return s_d, s_o

    def finish(i, hh, s_d, s_o):
        r0 = i * tq
        vs = slice(hh * MLA_V, (hh + 1) * MLA_V)
        m = jnp.max(s_d, -1, keepdims=True)
        if i > 0:
            m = jnp.maximum(m, jnp.max(s_o, -1, keepdims=True))
        e_d = jnp.exp2(s_d - m)
        l = jnp.sum(e_d, -1, keepdims=True)
        o = jnp.dot(e_d.astype(BF16), v_ref[r0:r0 + tq, vs], preferred_element_type=F32)
        if i > 0:
            e_o = jnp.exp2(s_o - m)
            l = l + jnp.sum(e_o, -1, keepdims=True)
            o = o + jnp.dot(e_o.astype(BF16), v_ref[0:r0, vs], preferred_element_type=F32)
        o_ref[r0:r0 + tq, vs] = (o / l).astype(o_ref.dtype)

    tiles = [(i, hh) for i in reversed(range(S // tq)) for hh in range(nh)]
    pending = [scores(*t) for t in tiles[:MLA_SCORES_AHEAD]]
    for n, t in enumerate(tiles):
        if n + MLA_SCORES_AHEAD < len(tiles):
            pending.append(scores(*tiles[n + MLA_SCORES_AHEAD]))
        finish(*t, *pending.pop(0))


def _mla_attention(q, k, v, B, S, H):
    T = B * S
    tq = min(ATTN_TQ, S)
    nh = 2 if H % 2 == 0 else 1
    kern = functools.partial(_mla_attn_kernel, S, tq, nh)
    nbytes = 2 * nh * S * (2 * MLA_HEAD_PAD + 2 * MLA_V) * 2 + 12 * tq * S * 4
    return pl.pallas_call(
        kern,
        name="mla_attn",
        grid=(B, H // nh),
        in_specs=[pl.BlockSpec((S, nh * MLA_HEAD_PAD), lambda b, h: (b, h)),
                  pl.BlockSpec((S, nh * MLA_HEAD_PAD), lambda b, h: (b, h)),
                  pl.BlockSpec((S, nh * MLA_V), lambda b, h: (b, h))],
        out_specs=pl.BlockSpec((S, nh * MLA_V), lambda b, h: (b, h)),
        out_shape=jax.ShapeDtypeStruct((T, H * MLA_V), BF16),
        compiler_params=_params(("arbitrary", "arbitrary"), nbytes),
    )(q, k, v)


def _mla_layer(h, act, a, B, S, tabs, mla_w_in, mla_g_cq, mla_g_ckv, mla_w_uq, mla_w_ukv, mla_g_q, mla_g_k, mla_w_o,
               next_norm):
    T, D = h.shape
    q_lora = mla_g_cq.shape[-1]
    kv_lora = mla_g_ckv.shape[-1]
    H = mla_w_ukv.shape[-1] // (MLA_NOPE + MLA_V)
    c_tab, s_tab = tabs
    tm = min(1024, T)
    lat = q_lora + kv_lora
    assert q_lora % kv_lora == 0
    cq = _mm_plain(act, mla_w_in, a, 0, q_lora, min(512, q_lora), F32)
    ckv = _mm_plain(act, mla_w_in, a, q_lora // kv_lora, kv_lora, kv_lora, F32)
    w_kr = lax.slice(mla_w_in, (a, 0, lat), (a + 1, D, lat + MLA_ROPE))[0]
    kr = _mm_plain(act, jnp.concatenate([w_kr, w_kr], axis=1), None, 0, LANES, LANES, F32)
    cq = _rmsnorm(cq, mla_g_cq[a])
    ckv = _rmsnorm(ckv, mla_g_ckv[a])
    w_uq = mla_w_uq[a].reshape(q_lora, H, MLA_QK)
    w_uq = jnp.concatenate([w_uq, w_uq[:, :, MLA_NOPE:]], axis=2).reshape(q_lora, H * MLA_HEAD_PAD)
    hp = 2 if H % 2 == 0 else 1
    tn = hp * MLA_HEAD_PAD
    tab_specs = [(t, (tm, LANES), lambda j, i: (i, 0)) for t in (c_tab, s_tab)]
    gain = lambda g: (_pad_gain_mla(g), (1, MLA_HEAD_PAD), lambda j, i: (0, 0))
    q = _mm((cq, None), [(w_uq, None, 0)], functools.partial(_epi_mla_q, hp, MLA_QK ** -0.5 * LOG2E),
            [gain(mla_g_q[a])] + tab_specs,
            [(jax.ShapeDtypeStruct((T, H * MLA_HEAD_PAD), BF16), (tm, tn), lambda j, i: (i, j))],
            n_cols=H * MLA_HEAD_PAD, tn=tn, tm=tm)[0]
    kr_spec = (kr, (tm, LANES), lambda j, i: (i, 0))
    k, v = _mm((ckv, None), [(mla_w_ukv, a, 0)], functools.partial(_epi_mla_kv, hp),
               [gain(mla_g_k[a])] + tab_specs + [kr_spec],
               [(jax.ShapeDtypeStruct((T, H * MLA_HEAD_PAD), BF16), (tm, tn), lambda j, i: (i, j)),
                (jax.ShapeDtypeStruct((T, H * MLA_V), BF16), (tm, hp * MLA_V), lambda j, i: (i, j))],
               n_cols=H * MLA_HEAD_PAD, tn=tn, tm=tm)
    o = _mla_attention(q, k, v, B, S, H)
    return _mm_residual(o, mla_w_o, a, h, 1.0, next_norm)


def _nsa_head_norm(a, g2):
    sel = _sel_matrix((MXU_DIM, MXU_DIM), lambda r, col: r // NSA_DH == col // NSA_DH)
    r = lax.rsqrt(_mxu_row_sum(a * a, sel) / NSA_DH + EPS)
    return a * r * g2


def _epi_nsa_k(n_pairs, accs, extra, outs, scr):
    g = extra[0][...]
    g2 = jnp.concatenate([g, g], axis=1)
    c, s = extra[1][...], extra[2][...]
    o = outs[0]
    acc = accs[0]
    for pp in range(n_pairs):
        kn = _nsa_head_norm(acc[:, pp * MXU_DIM:(pp + 1) * MXU_DIM], g2)
        for hh in range(2):
            col = (2 * pp + hh) * NSA_DH
            o[:, col:col + NSA_DH] = _rope_nsa(kn[:, hh * NSA_DH:(hh + 1) * NSA_DH], c, s).astype(o.dtype)


def _compress_kernel(n_cmp, k_ref, v_ref, pk_ref, pv_ref, kw1_ref, kb1_ref, kw2_ref,
                     vw1_ref, vb1_ref, vw2_ref, gk_ref, ko_ref, vo_ref):
    half = CMP_STRIDE * NSA_DH
    n_rows = ko_ref.shape[0]
    row = lax.broadcasted_iota(jnp.int32, (n_rows, 1), 0)

    def phi(t_ref, pos_ref, w1_ref, b1_ref, w2_ref):
        slabs = [t_ref[pl.ds(l, n_rows, stride=CMP_STRIDE), :] for l in range(CMP_STRIDE)]
        lo = jnp.concatenate([slabs[l] + pos_ref[:, l * NSA_DH:(l + 1) * NSA_DH]
                              for l in range(CMP_STRIDE)], axis=1).astype(BF16)
        hi = jnp.concatenate([slabs[l] + pos_ref[:, half + l * NSA_DH:half + (l + 1) * NSA_DH]
                              for l in range(CMP_STRIDE)], axis=1).astype(BF16)
        p_lo = jnp.dot(lo, w1_ref[0:half, :].astype(BF16), preferred_element_type=F32)
        p_hi = jnp.dot(hi, w1_ref[half:2 * half, :].astype(BF16), preferred_element_type=F32)
        pre = p_lo + pltpu.roll(p_hi, n_rows - 1, 0) + b1_ref[...]
        hid = pre * (1.0 / (1.0 + jnp.exp(-pre)))
        out = jnp.dot(hid.astype(BF16), w2_ref[...].astype(BF16), preferred_element_type=F32)
        return jnp.where(row < n_cmp, out, 0.0)

    kc = phi(k_ref, pk_ref, kw1_ref, kb1_ref, kw2_ref)
    r = lax.rsqrt(jnp.mean(kc * kc, -1, keepdims=True) + EPS)
    ko_ref[...] = (kc * r * gk_ref[...]).astype(ko_ref.dtype)
    vo_ref[...] = phi(v_ref, pv_ref, vw1_ref, vb1_ref, vw2_ref).astype(vo_ref.dtype)


def _compress(raw, B, S, G, cmp_pos_k, cmp_pos_v, k_w1, k_b1, k_w2, v_w1, v_b1, v_w2, g_k_cmp):
    n_cmp = (S - CMP_BLOCK) // CMP_STRIDE + 1
    n_rows = S // CMP_STRIDE
    hid = k_w1.shape[-1]
    full = lambda shape: pl.BlockSpec(shape, lambda b, g: (0,) * len(shape))
    w_specs = [full((CMP_BLOCK * NSA_DH, hid)), full((1, hid)), full((hid, NSA_DH))]
    out_spec = pl.BlockSpec((None, None, n_rows, NSA_DH), lambda b, g: (b, g, 0, 0))
    out_sds = jax.ShapeDtypeStruct((B, G, n_rows, NSA_DH), BF16)
    nbytes = 4 * (2 * CMP_BLOCK * NSA_DH * hid * 4) + 4 * S * NSA_DH * 4 + 8 * n_rows * CMP_BLOCK * NSA_DH * 4
    return pl.pallas_call(
        functools.partial(_compress_kernel, n_cmp),
        name="nsa_compress",
        grid=(B, G),
        in_specs=[pl.BlockSpec((S, NSA_DH), lambda b, g: (b, g)),
                  pl.BlockSpec((S, NSA_DH), lambda b, g: (b, G + g)),
                  full((1, CMP_BLOCK * NSA_DH)), full((1, CMP_BLOCK * NSA_DH))]
                 + w_specs + w_specs + [full((1, NSA_DH))],
        out_specs=[out_spec, out_spec],
        out_shape=[out_sds, out_sds],
        compiler_params=_params(("arbitrary", "arbitrary"), nbytes),
    )(raw, raw, cmp_pos_k.reshape(1, -1), cmp_pos_v.reshape(1, -1),
      k_w1, k_b1.reshape(1, -1), k_w2, v_w1, v_b1.reshape(1, -1), v_w2, g_k_cmp.reshape(1, -1))


def _nsa_shared_kv(h, B, S, tabs, kv_norm, kv_w, cmp_pos_k, cmp_pos_v, cmp_k_w1, cmp_k_b1, cmp_k_w2,
                   cmp_v_w1, cmp_v_b1, cmp_v_w2, g_k_cmp, g_k_slc, g_k_win):
    T = h.shape[0]
    G = kv_w.shape[-1] // (2 * N_BRANCH * NSA_DH)
    assert G % 2 == 0
    part = G * NSA_DH
    tm = min(1024, T)
    c_tab, s_tab = tabs
    y = (_rmsnorm(h, kv_norm), None)
    raw = _mm_plain(y, kv_w, None, 0, 2 * part, part, F32)
    k_cmp, v_cmp = _compress(raw, B, S, G, cmp_pos_k, cmp_pos_v, cmp_k_w1, cmp_k_b1, cmp_k_w2,
                             cmp_v_w1, cmp_v_b1, cmp_v_w2, g_k_cmp)

    def k_branch(part_idx, gain):
        extras = [(gain.reshape(1, NSA_DH), (1, NSA_DH), lambda j, i: (0, 0)),
                  (c_tab, (tm, LANES), lambda j, i: (i, 0)), (s_tab, (tm, LANES), lambda j, i: (i, 0))]
        return _mm(y, [(kv_w, None, part_idx)], functools.partial(_epi_nsa_k, G // 2), extras,
                   [(jax.ShapeDtypeStruct((T, part), BF16), (tm, part), lambda j, i: (i, j))],
                   n_cols=part, tn=part, tm=tm)[0]

    k_slc = k_branch(2, g_k_slc)
    v_slc = _mm_plain(y, kv_w, None, 3, part, part, BF16)
    k_win = k_branch(4, g_k_win)
    v_win = _mm_plain(y, kv_w, None, 5, part, part, BF16)
    return k_cmp, v_cmp, k_slc, v_slc, k_win, v_win


def _epi_nsa_q(n_pairs, out_scale, accs, extra, outs, scr):
    g = extra[0][...] * out_scale
    g2 = jnp.concatenate([g, g], axis=1)
    c, s = extra[1][...], extra[2][...]
    q_o, qr_o = outs
    acc = accs[0]
    for pp in range(n_pairs):
        qn = _nsa_head_norm(acc[:, pp * MXU_DIM:(pp + 1) * MXU_DIM], g2)
        for hh in range(2):
            x = qn[:, hh * NSA_DH:(hh + 1) * NSA_DH]
            q_o[2 * pp + hh] = x.astype(q_o.dtype)
            qr_o[2 * pp + hh] = _rope_nsa(x, c, s).astype(qr_o.dtype)


def _nsa_attn_kernel(S, tq, hpg, n_cmp,
                     q_ref, qr_ref, gl_ref, bg_ref, kc_ref, vc_ref, ks_ref, vs_ref, kw_ref, vw_ref,
                     o_ref, ocmp_scr, obuf_scr):
    g_idx = pl.program_id(1)
    n_cr = kc_ref.shape[0]
    n_slc = S // SLC_BLOCK
    n_sel = min(SLC_TOPK, n_slc)
    k_cmp = kc_ref[...]
    v_cmp = vc_ref[...]
    lane = lax.broadcasted_iota(jnp.int32, (tq, LANES), 1)
    nt = (((1,), (1,)), ((), ()))

    jj = lax.broadcasted_iota(jnp.int32, (n_slc, n_cr), 0)
    nn = lax.broadcasted_iota(jnp.int32, (n_slc, n_cr), 1)
    cs = nn * CMP_STRIDE
    ss = jj * SLC_BLOCK
    ov = jnp.maximum(jnp.minimum(cs + CMP_BLOCK, ss + SLC_BLOCK) - jnp.maximum(cs, ss), 0)
    agg_t = jnp.where(nn < n_cmp, ov.astype(F32) / CMP_STRIDE, 0.0)

    for i in range(S // tq):
        r0 = i * tq
        nk = r0 + tq
        rows = slice(r0, r0 + tq)

        spos_c = r0 + lax.broadcasted_iota(jnp.int32, (tq, n_cr), 0)
        ncol = lax.broadcasted_iota(jnp.int32, (tq, n_cr), 1)
        cmask = (ncol * CMP_STRIDE + (CMP_BLOCK - 1) <= spos_c) & (ncol < n_cmp)
        cbias = jnp.where(cmask, 0.0, NEG)
        cmask_f = cmask.astype(F32)
        q_all = q_ref[:, rows, :].reshape(hpg * tq, NSA_DH)
        sc = lax.dot_general(q_all, k_cmp, nt, preferred_element_type=F32).reshape(hpg, tq, n_cr) + cbias
        m = jnp.max(sc, -1, keepdims=True)
        e = jnp.exp2(sc - m) * cmask_f
        l = jnp.sum(e, -1, keepdims=True)
        p = e / jnp.where(l > 0.0, l, 1.0)
        ocmp_scr[...] = jnp.dot(p.reshape(hpg * tq, n_cr).astype(BF16), v_cmp,
                                preferred_element_type=F32).reshape(hpg, tq, NSA_DH)
        psum = jnp.sum(p, axis=0)

        imp = lax.dot_general(agg_t, psum, nt, precision=lax.Precision.HIGHEST,
                              preferred_element_type=F32)
        jb = lax.broadcasted_iota(jnp.int32, (n_slc, tq), 0)
        sp = r0 + lax.broadcasted_iota(jnp.int32, (n_slc, tq), 1)
        cur = sp // SLC_BLOCK
        valid = jb * SLC_BLOCK <= sp
        forced = (jb == 0) | (jb == cur) | (jb == cur - 1)
        imp = jnp.where(forced, FORCE, jnp.where(valid, imp, -1.0))
        rank = jnp.zeros((n_slc, tq), jnp.int32)
        for jp in range(n_slc):
            other = imp[jp:jp + 1, :]
            ahead = (other > imp) | ((other == imp) & (jp < jb))
            rank = rank + ahead.astype(jnp.int32)
        sel_t = (rank < n_sel).astype(BF16)

        eb = lax.broadcasted_iota(jnp.int32, (n_slc, nk), 0)
        ek = lax.broadcasted_iota(jnp.int32, (n_slc, nk), 1)
        expand = (ek // SLC_BLOCK == eb).astype(BF16)
        sel_keys = lax.dot_general(sel_t, expand, (((0,), (0,)), ((), ())), preferred_element_type=F32)
        qrow = r0 + lax.broadcasted_iota(jnp.int32, (tq, nk), 0)
        kcol = lax.broadcasted_iota(jnp.int32, (tq, nk), 1)
        sbias = jnp.where((sel_keys > 0.5) & (kcol <= qrow), 0.0, NEG)

        w0 = (max(0, r0 - WINDOW) // tq) * tq
        nw = nk - w0
        wrow = r0 + lax.broadcasted_iota(jnp.int32, (tq, nw), 0)
        wcol = w0 + lax.broadcasted_iota(jnp.int32, (tq, nw), 1)
        wdiff = wrow - wcol
        wbias = jnp.where((wdiff >= 0) & (wdiff < WINDOW), 0.0, NEG)

        z = gl_ref[rows, :] + bg_ref[...]
        gates = 1.0 / (1.0 + jnp.exp(-z))

        k_s = ks_ref[0:nk, :]
        v_s = vs_ref[0:nk, :]
        k_w = kw_ref[w0:nk, :]
        v_w = vw_ref[w0:nk, :]

        def main_body(pp, carry):
            hs = (2 * pp, 2 * pp + 1)
            qs = [qr_ref[hh, rows, :] for hh in hs]
            s_s = [lax.dot_general(q, k_s, nt, preferred_element_type=F32) + sbias for q in qs]
            s_w = [lax.dot_general(q, k_w, nt, preferred_element_type=F32) + wbias for q in qs]
            m_s = [jnp.max(s, -1, keepdims=True) for s in s_s]
            m_w = [jnp.max(s, -1, keepdims=True) for s in s_w]
            e_s = [jnp.exp2(s - m) for s, m in zip(s_s, m_s)]
            e_w = [jnp.exp2(s - m) for s, m in zip(s_w, m_w)]
            l_s = [jnp.sum(e, -1, keepdims=True) for e in e_s]
            l_w = [jnp.sum(e, -1, keepdims=True) for e in e_w]
            o_s = [jnp.dot(e.astype(BF16), v_s, preferred_element_type=F32) / l for e, l in zip(e_s, l_s)]
            o_w = [jnp.dot(e.astype(BF16), v_w, preferred_element_type=F32) / l for e, l in zip(e_w, l_w)]
            for t, hh in enumerate(hs):
                c0 = (g_idx * hpg + hh) * N_BRANCH
                gate = lambda br: jnp.sum(jnp.where(lane == c0 + br, gates, 0.0), -1, keepdims=True)
                o = gate(0) * ocmp_scr[hh] + gate(1) * o_s[t] + gate(2) * o_w[t]
                obuf_scr[hh] = o.astype(obuf_scr.dtype)
            return carry

        lax.fori_loop(0, hpg // 2, main_body, 0)
        for hh in range(hpg):
            o_ref[rows, hh * NSA_DH:(hh + 1) * NSA_DH] = obuf_scr[hh]


def _nsa_attention(q, qr, gate_logits, b_gate, shared, B, S, G, hpg):
    k_cmp, v_cmp, k_slc, v_slc, k_win, v_win = shared
    T = B * S
    H = G * hpg
    tq = min(ATTN_TQ, S)
    n_cmp = (S - CMP_BLOCK) // CMP_STRIDE + 1
    n_cr = k_cmp.shape[2]
    kern = functools.partial(_nsa_attn_kernel, S, tq, hpg, n_cmp)
    head_spec = pl.BlockSpec((hpg, S, NSA_DH), lambda b, g: (g, b, 0))
    cmp_spec = pl.BlockSpec((None, None, n_cr, NSA_DH), lambda b, g: (b, g, 0, 0))
    kv_spec = pl.BlockSpec((S, NSA_DH), lambda b, g: (b, g))
    nbytes = (2 * 2 * hpg * S * NSA_DH * 2 + 2 * S * LANES * 4 + 8 * S * NSA_DH * 2
              + 2 * S * hpg * NSA_DH * 2 + 10 * tq * S * 4)
    return pl.pallas_call(
        kern,
        name="nsa_attn",
        grid=(B, G),
        in_specs=[head_spec, head_spec,
                  pl.BlockSpec((S, LANES), lambda b, g: (b, 0)),
                  pl.BlockSpec((1, LANES), lambda b, g: (0, 0)),
                  cmp_spec, cmp_spec, kv_spec, kv_spec, kv_spec, kv_spec],
        out_specs=pl.BlockSpec((S, hpg * NSA_DH), lambda b, g: (b, g)),
        out_shape=jax.ShapeDtypeStruct((T, H * NSA_DH), BF16),
        scratch_shapes=[pltpu.VMEM((hpg, tq, NSA_DH), F32), pltpu.VMEM((hpg, tq, NSA_DH), BF16)],
        compiler_params=_params(("arbitrary", "arbitrary"), nbytes),
    )(q, qr, gate_logits, b_gate, k_cmp, v_cmp, k_slc, v_slc, k_win, v_win)


def _nsa_layer(h, u, b, B, S, G, tabs, shared, nsa_w_in, nsa_b_gate, nsa_g_q, nsa_w_o, next_norm):
    T, D = h.shape
    n_gate = nsa_b_gate.shape[-1]
    H = n_gate // N_BRANCH
    hpg = H // G
    q_width = H * NSA_DH
    assert n_gate <= LANES and hpg % 2 == 0
    c_tab, s_tab = tabs
    tm = min(1024, T)
    hp = min(4, H)
    tn = hp * NSA_DH
    extras = [(nsa_g_q[b].reshape(1, NSA_DH), (1, NSA_DH), lambda j, i: (0, 0)),
              (c_tab, (tm, LANES), lambda j, i: (i, 0)), (s_tab, (tm, LANES), lambda j, i: (i, 0))]
    head_out = (jax.ShapeDtypeStruct((H, T, NSA_DH), BF16), (hp, tm, NSA_DH), lambda j, i: (j, i, 0))
    q, qr = _mm(u, [(nsa_w_in, b, 0)], functools.partial(_epi_nsa_q, hp // 2, NSA_DH ** -0.5 * LOG2E),
                extras, [head_out, head_out], n_cols=q_width, tn=tn, tm=tm)
    w_gate = lax.slice(nsa_w_in, (b, 0, q_width), (b + 1, D, q_width + n_gate))[0]
    w_gate = jnp.pad(w_gate, ((0, 0), (0, LANES - n_gate)))
    gate_logits = _mm_plain(u, w_gate, None, 0, LANES, LANES, F32)
    b_gate = jnp.pad(nsa_b_gate[b], (0, LANES - n_gate)).reshape(1, LANES)
    o = _nsa_attention(q, qr, gate_logits, b_gate, shared, B, S, G, hpg)
    return _mm_residual(o, nsa_w_o, b, h, 1.0, next_norm)


def _lane_table(values):
    return jnp.asarray(values, F32).reshape(1, LANES)


def kernel(x, positions, ffn1_norm, ffn1_w_gate, ffn1_w_up, ffn1_w_down, mix_norm, ffn2_norm, ffn2_w_gate, ffn2_w_up, ffn2_w_down, mla_w_in, mla_g_cq, mla_g_ckv, mla_w_uq, mla_w_ukv, mla_g_q, mla_g_k, mla_w_o, kv_norm, kv_w, cmp_pos_k, cmp_pos_v, cmp_k_w1, cmp_k_b1, cmp_k_w2, cmp_v_w1, cmp_v_b1, cmp_v_w2, g_k_cmp, g_k_slc, g_k_win, nsa_w_in, nsa_b_gate, nsa_g_q, nsa_w_o):
    B, S, D = x.shape
    T = B * S
    depth = ffn1_norm.shape[0]
    n_a = mla_w_in.shape[0]
    G = kv_w.shape[-1] // (2 * N_BRANCH * NSA_DH)

    pos = positions.reshape(T, 1).astype(F32)
    half_a = MLA_ROPE // 2
    inv_a = jnp.power(ROPE_THETA, -jnp.arange(0, MLA_ROPE, 2, dtype=F32) / MLA_ROPE)
    zeros_a = jnp.zeros((LANES - MLA_ROPE,), F32)
    ones_h = jnp.ones((half_a,), F32)
    tabs_a = _rope_tables(pos, _lane_table(jnp.concatenate([inv_a, inv_a, zeros_a])),
                          _lane_table(jnp.concatenate([ones_h, ones_h, zeros_a])),
                          _lane_table(jnp.concatenate([-ones_h, ones_h, zeros_a])))
    inv_b = jnp.power(ROPE_THETA, -jnp.arange(0, NSA_DH, 2, dtype=F32) / NSA_DH)
    ones_b = jnp.ones((NSA_DH // 2,), F32)
    tabs_b = _rope_tables(pos, _lane_table(jnp.concatenate([inv_b, inv_b])),
                          _lane_table(jnp.concatenate([ones_b, ones_b])),
                          _lane_table(jnp.concatenate([-ones_b, ones_b])))

    h = x.reshape(T, D)
    shared = None
    act = (_rmsnorm(h, ffn1_norm, layer=0), None)
    for layer in range(depth):
        h, act = _ffn(h, act, ffn1_w_gate, ffn1_w_up, ffn1_w_down, layer, (mix_norm, layer))
        if layer < n_a:
            h, act = _mla_layer(h, act, layer, B, S, tabs_a, mla_w_in, mla_g_cq, mla_g_ckv, mla_w_uq, mla_w_ukv,
                                mla_g_q, mla_g_k, mla_w_o, (ffn2_norm, layer))
        else:
            h, act = _nsa_layer(h, act, layer - n_a, B, S, G, tabs_b, shared, nsa_w_in, nsa_b_gate, nsa_g_q,
                                nsa_w_o, (ffn2_norm, layer))
        h, act = _ffn(h, act, ffn2_w_gate, ffn2_w_up, ffn2_w_down, layer,
                      (ffn1_norm, layer + 1) if layer + 1 < depth else None)
        if layer == n_a - 1:
            shared = _nsa_shared_kv(h, B, S, tabs_b, kv_norm, kv_w, cmp_pos_k, cmp_pos_v, cmp_k_w1, cmp_k_b1,
                                    cmp_k_w2, cmp_v_w1, cmp_v_b1, cmp_v_w2, g_k_cmp, g_k_slc, g_k_win)
    return h.reshape(B, S, D)
```

```python
import functools
import math

import jax
import jax.numpy as jnp
from jax import lax
from jax.experimental import pallas as pl
from jax.experimental.pallas import tpu as pltpu

F32 = jnp.float32
BF16 = jnp.bfloat16

ROPE_THETA = 10000.0
EPS = 1e-6
NEG = -1e30
FORCE = 1e6
MLA_NOPE = 128
MLA_ROPE = 64
MLA_V = 128
MLA_QK = MLA_NOPE + MLA_ROPE
MLA_HEAD_PAD = 256
NSA_DH = 128
N_BRANCH = 3
CMP_BLOCK = 32
CMP_STRIDE = 16
SLC_BLOCK = 64
SLC_TOPK = 16
WINDOW = 512
LOG2E = math.log2(math.e)

LANES = 128
MXU_DIM = 256
V7X_VMEM_BYTES = 64 * 1024 * 1024
VMEM_CAP = V7X_VMEM_BYTES - 8 * 1024 * 1024

ATTN_TQ = 256
MLA_SCORES_AHEAD = 2


def _vmem_limit(nbytes):
    return int(min(VMEM_CAP, max(32 * 1024 * 1024, nbytes + 8 * 1024 * 1024)))


def _params(sem, nbytes):
    return pltpu.CompilerParams(dimension_semantics=sem, vmem_limit_bytes=_vmem_limit(nbytes))


def _rope_table_kernel(pos_ref, inv_ref, mc_ref, ms_ref, c_ref, s_ref):
    ang = pos_ref[...] * inv_ref[...]
    c_ref[...] = jnp.cos(ang) * mc_ref[...]
    s_ref[...] = jnp.sin(ang) * ms_ref[...]


def _rope_tables(pos, inv, mask_c, mask_s):
    T = pos.shape[0]
    tm = min(T, 1024)
    row = pl.BlockSpec((1, LANES), lambda i: (0, 0))
    out = pl.BlockSpec((tm, LANES), lambda i: (i, 0))
    return pl.pallas_call(
        _rope_table_kernel,
        name="rope_tables",
        grid=(T // tm,),
        in_specs=[pl.BlockSpec((tm, 1), lambda i: (i, 0)), row, row, row],
        out_specs=[out, out],
        out_shape=[jax.ShapeDtypeStruct((T, LANES), F32)] * 2,
        compiler_params=_params(("arbitrary",), 0),
    )(pos, inv, mask_c, mask_s)


def _rope_nsa(x, c, s):
    return x * c + pltpu.roll(x, 64, 1) * s


def _rope_mla(x, c, s):
    return x * c + pltpu.roll(x, 32, 1) * s


def _mxu_row_sum(sq, sel):
    return jnp.dot(sq.astype(BF16), sel, preferred_element_type=F32)


def _sel_matrix(shape, pred):
    r = lax.broadcasted_iota(jnp.int32, shape, 0)
    c = lax.broadcasted_iota(jnp.int32, shape, 1)
    return pred(r, c).astype(BF16)


def _rmsnorm_kernel(x_ref, g_ref, o_ref):
    x = x_ref[...]
    r = lax.rsqrt(jnp.mean(x * x, axis=-1, keepdims=True) + EPS)
    o_ref[...] = (x * r * g_ref[...]).astype(o_ref.dtype)


def _rmsnorm(x, g, *, layer=None):
    T, width = x.shape
    tm = min(T, 512)
    if layer is None:
        g = g.reshape(1, width)
        g_spec = pl.BlockSpec((1, width), lambda i: (0, 0))
    else:
        g = g.reshape(g.shape[0], 1, width)
        g_spec = pl.BlockSpec((None, 1, width), lambda i: (layer, 0, 0))
    return pl.pallas_call(
        _rmsnorm_kernel,
        name="rmsnorm",
        grid=(T // tm,),
        in_specs=[pl.BlockSpec((tm, width), lambda i: (i, 0)), g_spec],
        out_specs=pl.BlockSpec((tm, width), lambda i: (i, 0)),
        out_shape=jax.ShapeDtypeStruct((T, width), BF16),
        compiler_params=_params(("arbitrary",), 6 * tm * width * 4),
    )(x, g)


def _scale_rows(acc, rs):
    return jnp.concatenate([acc[:, k * LANES:(k + 1) * LANES] * rs for k in range(acc.shape[1] // LANES)], axis=1)


def _mm_kernel(n_w, has_rs, w_transposed, n_extra, n_out, epilogue, x_ref, *refs):
    refs = list(refs)
    rs_ref = refs.pop(0) if has_rs else None
    w_refs = refs[:n_w]
    extra = refs[n_w:n_w + n_extra]
    outs = refs[n_w + n_extra:n_w + n_extra + n_out]
    w_bf = refs[n_w + n_extra + n_out:2 * n_w + n_extra + n_out]
    scr = refs[2 * n_w + n_extra + n_out:]

    @pl.when(pl.program_id(1) == 0)
    def _():
        for w, s in zip(w_refs, w_bf):
            s[...] = w[...].astype(BF16)

    x = x_ref[...]
    dims = (((1,), (1 if w_transposed else 0,)), ((), ()))
    accs = [lax.dot_general(x, s[...], dims, preferred_element_type=F32) for s in w_bf]
    if has_rs:
        rs = rs_ref[...]
        accs = [_scale_rows(a, rs) for a in accs]
    epilogue(accs, extra, outs, scr)


def _mm(act, weights, epilogue, extras, outs, *, n_cols, tn, tm=1024, scratch=(), w_transposed=False):
    x, row_scale = act
    M, K = x.shape
    tm = min(tm, M)
    assert M % tm == 0 and n_cols % tn == 0
    in_specs = [pl.BlockSpec((tm, K), lambda j, i: (i, 0))]
    args = [x]
    nbytes = 2 * tm * K * 2 + len(weights) * (2 * K * tn * 4 + K * tn * 2 + 2 * tm * tn * 4)
    if row_scale is not None:
        in_specs.append(pl.BlockSpec((tm, LANES), lambda j, i: (i, 0)))
        args.append(row_scale)
        nbytes += 2 * tm * LANES * 4
    for arr, layer, off in weights:
        if w_transposed:
            assert layer is not None
            in_specs.append(pl.BlockSpec((None, tn, K), lambda j, i, off=off, layer=layer: (layer, j + off, 0)))
        elif layer is None:
            in_specs.append(pl.BlockSpec((K, tn), lambda j, i, off=off: (0, j + off)))
        else:
            in_specs.append(pl.BlockSpec((None, K, tn), lambda j, i, off=off, layer=layer: (layer, 0, j + off)))
        args.append(arr)
    for arr, bs, im in extras:
        in_specs.append(pl.BlockSpec(bs, im))
        args.append(arr)
        nbytes += 2 * 4 * functools.reduce(lambda a, b: a * (b or 1), bs, 1)
    out_specs, out_shapes = [], []
    for sds, bs, im in outs:
        out_specs.append(pl.BlockSpec(bs, im))
        out_shapes.append(sds)
        nbytes += 2 * sds.dtype.itemsize * functools.reduce(lambda a, b: a * (b or 1), bs, 1)
    for shape, dtype in scratch:
        nbytes += jnp.dtype(dtype).itemsize * functools.reduce(lambda a, b: a * b, shape, 1)
    kern = functools.partial(_mm_kernel, len(weights), row_scale is not None, w_transposed, len(extras), len(outs),
                             epilogue)
    res = pl.pallas_call(
        kern,
        name="mm_" + getattr(epilogue, "func", epilogue).__name__[len("_epi_"):],
        grid=(n_cols // tn, M // tm),
        in_specs=in_specs,
        out_specs=out_specs,
        out_shape=out_shapes,
        scratch_shapes=[pltpu.VMEM((tn, K) if w_transposed else (K, tn), BF16) for _ in weights]
                       + [pltpu.VMEM(s, d) for s, d in scratch],
        compiler_params=_params(("arbitrary", "arbitrary"), nbytes),
    )(*args)
    return res


def _epi_plain(accs, extra, outs, scr):
    outs[0][...] = accs[0].astype(outs[0].dtype)


def _epi_swiglu(accs, extra, outs, scr):
    g, u = accs
    outs[0][...] = (g * (1.0 / (1.0 + jnp.exp(-g))) * u).astype(outs[0].dtype)


def _epi_residual(alpha, accs, extra, outs, scr):
    outs[0][...] = extra[0][...] + alpha * accs[0]


def _epi_residual_norm(alpha, width, accs, extra, outs, scr):
    res_ref, g_ref = extra
    h_o, hg_o, rs_o = outs
    ss = scr[0]
    j = pl.program_id(0)
    i = pl.program_id(1)
    h = res_ref[...] + alpha * accs[0]
    h_o[...] = h
    hg_o[...] = (h * g_ref[...]).astype(hg_o.dtype)
    sq = h * h
    part = sq[:, :LANES]
    for k in range(1, sq.shape[1] // LANES):
        part = part + sq[:, k * LANES:(k + 1) * LANES]

    @pl.when(j == 0)
    def _():
        ss[i] = part

    @pl.when(j > 0)
    def _():
        ss[i] = ss[i] + part

    @pl.when(j == pl.num_programs(0) - 1)
    def _():
        tot = jnp.sum(ss[i], -1, keepdims=True)
        rs_o[...] = jnp.broadcast_to(lax.rsqrt(tot / width + EPS), rs_o.shape)


def _mm_plain(act, w, layer, col_off, n_cols, tn, dtype, w_transposed=False):
    M = act[0].shape[0]
    tm = min(1024, M)
    out = (jax.ShapeDtypeStruct((M, n_cols), dtype), (tm, tn), lambda j, i: (i, j))
    return _mm(act, [(w, layer, col_off)], _epi_plain, [], [out], n_cols=n_cols, tn=tn, tm=tm,
               w_transposed=w_transposed)[0]


def _mm_residual(x, w, layer, res, alpha, next_norm=None):
    M, N = res.shape
    K = x.shape[1]
    tn = 512 if N % 512 == 0 else 256
    tm = min(1024 if K <= 4096 else 512, M)
    blk = ((tm, tn), lambda j, i: (i, j))
    h_out = (jax.ShapeDtypeStruct((M, N), F32),) + blk
    if next_norm is None:
        return _mm((x, None), [(w, layer, 0)], functools.partial(_epi_residual, alpha), [(res,) + blk],
                   [h_out], n_cols=N, tn=tn, tm=tm)[0], None
    gains, g_layer = next_norm
    g_spec = (gains.reshape(gains.shape[0], 1, N), (None, 1, tn), lambda j, i: (g_layer, 0, j))
    last_j = N // tn - 1
    rs_out = (jax.ShapeDtypeStruct((M, LANES), F32), (tm, LANES), lambda j, i: (jnp.where(j == last_j, i, 0), 0))
    h, hg, rs = _mm((x, None), [(w, layer, 0)], functools.partial(_epi_residual_norm, alpha, N),
                    [(res,) + blk, g_spec], [h_out, (jax.ShapeDtypeStruct((M, N), BF16),) + blk, rs_out],
                    n_cols=N, tn=tn, tm=tm, scratch=[((M // tm, tm, LANES), F32)])
    return h, (hg, rs)


def _ffn(h, act, wg, wu, wd, layer, next_norm):
    M, D = h.shape
    F = wg.shape[-1]
    tm = min(1024, M)
    hid = _mm(act, [(wg, layer, 0), (wu, layer, 0)], _epi_swiglu, [],
              [(jax.ShapeDtypeStruct((M, F), BF16), (tm, 256), lambda j, i: (i, j))],
              n_cols=F, tn=256, tm=tm)[0]
    return _mm_residual(hid, wd, layer, h, 0.5, next_norm)


def _epi_mla_q(n_heads, out_scale, accs, extra, outs, scr):
    g = extra[0][...] * out_scale
    c, s = extra[1][...], extra[2][...]
    o = outs[0]
    sel = _sel_matrix((MLA_HEAD_PAD, MLA_HEAD_PAD), lambda r, col: r < MLA_QK)
    acc = accs[0]
    for hh in range(n_heads):
        b0 = hh * MLA_HEAD_PAD
        a = acc[:, b0:b0 + MLA_HEAD_PAD]
        r = lax.rsqrt(_mxu_row_sum(a * a, sel) / MLA_QK + EPS)
        an = a * r * g
        o[:, b0:b0 + LANES] = an[:, :LANES].astype(o.dtype)
        o[:, b0 + LANES:b0 + 2 * LANES] = _rope_mla(an[:, LANES:], c, s).astype(o.dtype)


def _epi_mla_kv(n_heads, accs, extra, outs, scr):
    g = extra[0][...]
    c, s = extra[1][...], extra[2][...]
    kr = extra[3][...]
    kr2 = kr * kr
    k_o, v_o = outs
    sel = _sel_matrix((MLA_HEAD_PAD, LANES), lambda r, col: r < MLA_QK)
    acc = accs[0]
    for hh in range(n_heads):
        b0 = hh * MLA_HEAD_PAD
        a = acc[:, b0:b0 + MLA_HEAD_PAD]
        kn = a[:, :LANES]
        ss = _mxu_row_sum(jnp.concatenate([kn * kn, kr2], axis=1), sel)
        r = lax.rsqrt(ss / MLA_QK + EPS)
        k_o[:, b0:b0 + LANES] = (kn * r * g[:, :LANES]).astype(k_o.dtype)
        k_o[:, b0 + LANES:b0 + 2 * LANES] = _rope_mla(kr * r * g[:, LANES:], c, s).astype(k_o.dtype)
        v_o[:, hh * MLA_V:(hh + 1) * MLA_V] = a[:, LANES:].astype(v_o.dtype)


def _pad_gain_mla(g):
    return jnp.concatenate([g, g[MLA_NOPE:]]).reshape(1, MLA_HEAD_PAD)


def _mla_attn_kernel(S, tq, nh, q_ref, k_ref, v_ref, o_ref):
    row = lax.broadcasted_iota(jnp.int32, (tq, tq), 0)
    col = lax.broadcasted_iota(jnp.int32, (tq, tq), 1)
    tri = jnp.where(col <= row, 0.0, NEG)
    nt = (((1,), (1,)), ((), ()))

    def scores(i, hh):
        r0 = i * tq
        ks = slice(hh * MLA_HEAD_PAD, (hh + 1) * MLA_HEAD_PAD)
        q = q_ref[r0:r0 + tq, ks]
        s_d = lax.dot_general(q, k_ref[r0:r0 + tq, ks], nt, preferred_element_type=F32) + tri
        s_o = lax.dot_general(q, k_ref[0:r0, ks], nt, preferred_element_type=F32) if i > 0 else None
        return s_d, s_o

    def finish(i, hh, s_d, s_o):
        r0 = i * tq
        vs = slice(hh * MLA_V, (hh + 1) * MLA_V)
        m = jnp.max(s_d, -1, keepdims=True)
        if i > 0:
            m = jnp.maximum(m, jnp.max(s_o, -1, keepdims=True))
        e_d = jnp.exp2(s_d - m)
        l = jnp.sum(e_d, -1, keepdims=True)
        o = jnp.dot(e_d.astype(BF16), v_ref[r0:r0 + tq, vs], preferred_element_type=F32)
        if i > 0:
            e_o = jnp.exp2(s_o - m)
            l = l + jnp.sum(e_o, -1, keepdims=True)
            o = o + jnp.dot(e_o.astype(BF16), v_ref[0:r0, vs], preferred_element_type=F32)
        o_ref[r0:r0 + tq, vs] = (o / l).astype(o_ref.dtype)

    tiles = [(i, hh) for i in reversed(range(S // tq)) for hh in range(nh)]
    pending = [scores(*t) for t in tiles[:MLA_SCORES_AHEAD]]
    for n, t in enumerate(tiles):
        if n + MLA_SCORES_AHEAD < len(tiles):
            pending.append(scores(*tiles[n + MLA_SCORES_AHEAD]))
        finish(*t, *pending.pop(0))


def _mla_attention(q, k, v, B, S, H):
    T = B * S
    tq = min(ATTN_TQ, S)
    nh = 2 if H % 2 == 0 else 1
    kern = functools.partial(_mla_attn_kernel, S, tq, nh)
    nbytes = 2 * nh * S * (2 * MLA_HEAD_PAD + 2 * MLA_V) * 2 + 12 * tq * S * 4
    return pl.pallas_call(
        kern,
        name="mla_attn",
        grid=(B, H // nh),
        in_specs=[pl.BlockSpec((S, nh * MLA_HEAD_PAD), lambda b, h: (b, h)),
                  pl.BlockSpec((S, nh * MLA_HEAD_PAD), lambda b, h: (b, h)),
                  pl.BlockSpec((S, nh * MLA_V), lambda b, h: (b, h))],
        out_specs=pl.BlockSpec((S, nh * MLA_V), lambda b, h: (b, h)),
        out_shape=jax.ShapeDtypeStruct((T, H * MLA_V), BF16),
        compiler_params=_params(("arbitrary", "arbitrary"), nbytes),
    )(q, k, v)


def _mla_layer(h, act, a, B, S, tabs, mla_w_in, mla_g_cq, mla_g_ckv, mla_w_uq, mla_w_ukv, mla_g_q, mla_g_k, mla_w_o,
               next_norm):
    T, D = h.shape
    q_lora = mla_g_cq.shape[-1]
    kv_lora = mla_g_ckv.shape[-1]
    H = mla_w_ukv.shape[-1] // (MLA_NOPE + MLA_V)
    c_tab, s_tab = tabs
    tm = min(2048, T)
    lat = q_lora + kv_lora
    assert q_lora % kv_lora == 0
    w_in_t = jnp.swapaxes(mla_w_in, 1, 2)
    cq = _mm_plain(act, w_in_t, a, 0, q_lora, min(512, q_lora), F32, w_transposed=True)
    ckv = _mm_plain(act, w_in_t, a, q_lora // kv_lora, kv_lora, kv_lora, F32, w_transposed=True)
    w_kr = lax.slice(mla_w_in, (a, 0, lat), (a + 1, D, lat + MLA_ROPE))[0]
    kr = _mm_plain(act, jnp.concatenate([w_kr, w_kr], axis=1), None, 0, LANES, LANES, F32)
    cq = _rmsnorm(cq, mla_g_cq[a])
    ckv = _rmsnorm(ckv, mla_g_ckv[a])
    w_uq = mla_w_uq[a].reshape(q_lora, H, MLA_QK)
    w_uq = jnp.concatenate([w_uq, w_uq[:, :, MLA_NOPE:]], axis=2).reshape(q_lora, H * MLA_HEAD_PAD)
    hp = 2 if H % 2 == 0 else 1
    tn = hp * MLA_HEAD_PAD
    tab_specs = [(t, (tm, LANES), lambda j, i: (i, 0)) for t in (c_tab, s_tab)]
    gain = lambda g: (_pad_gain_mla(g), (1, MLA_HEAD_PAD), lambda j, i: (0, 0))
    q = _mm((cq, None), [(w_uq, None, 0)], functools.partial(_epi_mla_q, hp, MLA_QK ** -0.5 * LOG2E),
            [gain(mla_g_q[a])] + tab_specs,
            [(jax.ShapeDtypeStruct((T, H * MLA_HEAD_PAD), BF16), (tm, tn), lambda j, i: (i, j))],
            n_cols=H * MLA_HEAD_PAD, tn=tn, tm=tm)[0]
    kr_spec = (kr, (tm, LANES), lambda j, i: (i, 0))
    k, v = _mm((ckv, None), [(mla_w_ukv, a, 0)], functools.partial(_epi_mla_kv, hp),
               [gain(mla_g_k[a])] + tab_specs + [kr_spec],
               [(jax.ShapeDtypeStruct((T, H * MLA_HEAD_PAD), BF16), (tm, tn), lambda j, i: (i, j)),
                (jax.ShapeDtypeStruct((T, H * MLA_V), BF16), (tm, hp * MLA_V), lambda j, i: (i, j))],
               n_cols=H * MLA_HEAD_PAD, tn=tn, tm=tm)
    o = _mla_attention(q, k, v, B, S, H)
    return _mm_residual(o, mla_w_o, a, h, 1.0, next_norm)


def _nsa_head_norm(a, g2):
    sel = _sel_matrix((MXU_DIM, MXU_DIM), lambda r, col: r // NSA_DH == col // NSA_DH)
    r = lax.rsqrt(_mxu_row_sum(a * a, sel) / NSA_DH + EPS)
    return a * r * g2


def _epi_nsa_k(n_pairs, accs, extra, outs, scr):
    g = extra[0][...]
    g2 = jnp.concatenate([g, g], axis=1)
    c, s = extra[1][...], extra[2][...]
    o = outs[0]
    acc = accs[0]
    for pp in range(n_pairs):
        kn = _nsa_head_norm(acc[:, pp * MXU_DIM:(pp + 1) * MXU_DIM], g2)
        for hh in range(2):
            col = (2 * pp + hh) * NSA_DH
            o[:, col:col + NSA_DH] = _rope_nsa(kn[:, hh * NSA_DH:(hh + 1) * NSA_DH], c, s).astype(o.dtype)


def _compress_kernel(n_cmp, k_ref, v_ref, pk_ref, pv_ref, kw1_ref, kb1_ref, kw2_ref,
                     vw1_ref, vb1_ref, vw2_ref, gk_ref, ko_ref, vo_ref):
    half = CMP_STRIDE * NSA_DH
    n_rows = ko_ref.shape[0]
    row = lax.broadcasted_iota(jnp.int32, (n_rows, 1), 0)

    def phi(t_ref, pos_ref, w1_ref, b1_ref, w2_ref):
        slabs = [t_ref[pl.ds(l, n_rows, stride=CMP_STRIDE), :] for l in range(CMP_STRIDE)]
        lo = jnp.concatenate([slabs[l] + pos_ref[:, l * NSA_DH:(l + 1) * NSA_DH]
                              for l in range(CMP_STRIDE)], axis=1).astype(BF16)
        hi = jnp.concatenate([slabs[l] + pos_ref[:, half + l * NSA_DH:half + (l + 1) * NSA_DH]
                              for l in range(CMP_STRIDE)], axis=1).astype(BF16)
        p_lo = jnp.dot(lo, w1_ref[0:half, :].astype(BF16), preferred_element_type=F32)
        p_hi = jnp.dot(hi, w1_ref[half:2 * half, :].astype(BF16), preferred_element_type=F32)
        pre = p_lo + pltpu.roll(p_hi, n_rows - 1, 0) + b1_ref[...]
        hid = pre * (1.0 / (1.0 + jnp.exp(-pre)))
        out = jnp.dot(hid.astype(BF16), w2_ref[...].astype(BF16), preferred_element_type=F32)
        return jnp.where(row < n_cmp, out, 0.0)

    kc = phi(k_ref, pk_ref, kw1_ref, kb1_ref, kw2_ref)
    r = lax.rsqrt(jnp.mean(kc * kc, -1, keepdims=True) + EPS)
    ko_ref[...] = (kc * r * gk_ref[...]).astype(ko_ref.dtype)
    vo_ref[...] = phi(v_ref, pv_ref, vw1_ref, vb1_ref, vw2_ref).astype(vo_ref.dtype)


def _compress(raw, B, S, G, cmp_pos_k, cmp_pos_v, k_w1, k_b1, k_w2, v_w1, v_b1, v_w2, g_k_cmp):
    n_cmp = (S - CMP_BLOCK) // CMP_STRIDE + 1
    n_rows = S // CMP_STRIDE
    hid = k_w1.shape[-1]
    full = lambda shape: pl.BlockSpec(shape, lambda b, g: (0,) * len(shape))
    w_specs = [full((CMP_BLOCK * NSA_DH, hid)), full((1, hid)), full((hid, NSA_DH))]
    out_spec = pl.BlockSpec((None, None, n_rows, NSA_DH), lambda b, g: (b, g, 0, 0))
    out_sds = jax.ShapeDtypeStruct((B, G, n_rows, NSA_DH), BF16)
    nbytes = 4 * (2 * CMP_BLOCK * NSA_DH * hid * 4) + 4 * S * NSA_DH * 4 + 8 * n_rows * CMP_BLOCK * NSA_DH * 4
    return pl.pallas_call(
        functools.partial(_compress_kernel, n_cmp),
        name="nsa_compress",
        grid=(B, G),
        in_specs=[pl.BlockSpec((S, NSA_DH), lambda b, g: (b, g)),
                  pl.BlockSpec((S, NSA_DH), lambda b, g: (b, G + g)),
                  full((1, CMP_BLOCK * NSA_DH)), full((1, CMP_BLOCK * NSA_DH))]
                 + w_specs + w_specs + [full((1, NSA_DH))],
        out_specs=[out_spec, out_spec],
        out_shape=[out_sds, out_sds],
        compiler_params=_params(("arbitrary", "arbitrary"), nbytes),
    )(raw, raw, cmp_pos_k.reshape(1, -1), cmp_pos_v.reshape(1, -1),
      k_w1, k_b1.reshape(1, -1), k_w2, v_w1, v_b1.reshape(1, -1), v_w2, g_k_cmp.reshape(1, -1))


def _nsa_shared_kv(h, B, S, tabs, kv_norm, kv_w, cmp_pos_k, cmp_pos_v, cmp_k_w1, cmp_k_b1, cmp_k_w2,
                   cmp_v_w1, cmp_v_b1, cmp_v_w2, g_k_cmp, g_k_slc, g_k_win):
    T = h.shape[0]
    G = kv_w.shape[-1] // (2 * N_BRANCH * NSA_DH)
    assert G % 2 == 0
    part = G * NSA_DH
    tm = min(1024, T)
    c_tab, s_tab = tabs
    y = (_rmsnorm(h, kv_norm), None)
    raw = _mm_plain(y, kv_w, None, 0, 2 * part, part, F32)
    k_cmp, v_cmp = _compress(raw, B, S, G, cmp_pos_k, cmp_pos_v, cmp_k_w1, cmp_k_b1, cmp_k_w2,
                             cmp_v_w1, cmp_v_b1, cmp_v_w2, g_k_cmp)

    def k_branch(part_idx, gain):
        extras = [(gain.reshape(1, NSA_DH), (1, NSA_DH), lambda j, i: (0, 0)),
                  (c_tab, (tm, LANES), lambda j, i: (i, 0)), (s_tab, (tm, LANES), lambda j, i: (i, 0))]
        return _mm(y, [(kv_w, None, part_idx)], functools.partial(_epi_nsa_k, G // 2), extras,
                   [(jax.ShapeDtypeStruct((T, part), BF16), (tm, part), lambda j, i: (i, j))],
                   n_cols=part, tn=part, tm=tm)[0]

    k_slc = k_branch(2, g_k_slc)
    v_slc = _mm_plain(y, kv_w, None, 3, part, part, BF16)
    k_win = k_branch(4, g_k_win)
    v_win = _mm_plain(y, kv_w, None, 5, part, part, BF16)
    return k_cmp, v_cmp, k_slc, v_slc, k_win, v_win


def _epi_nsa_q(n_pairs, out_scale, accs, extra, outs, scr):
    g = extra[0][...] * out_scale
    g2 = jnp.concatenate([g, g], axis=1)
    c, s = extra[1][...], extra[2][...]
    q_o, qr_o = outs
    acc = accs[0]
    for pp in range(n_pairs):
        qn = _nsa_head_norm(acc[:, pp * MXU_DIM:(pp + 1) * MXU_DIM], g2)
        for hh in range(2):
            x = qn[:, hh * NSA_DH:(hh + 1) * NSA_DH]
            q_o[2 * pp + hh] = x.astype(q_o.dtype)
            qr_o[2 * pp + hh] = _rope_nsa(x, c, s).astype(qr_o.dtype)


def _nsa_attn_kernel(S, tq, hpg, n_cmp,
                     q_ref, qr_ref, gl_ref, bg_ref, kc_ref, vc_ref, ks_ref, vs_ref, kw_ref, vw_ref,
                     o_ref, ocmp_scr, obuf_scr):
    g_idx = pl.program_id(1)
    n_cr = kc_ref.shape[0]
    n_slc = S // SLC_BLOCK
    n_sel = min(SLC_TOPK, n_slc)
    k_cmp = kc_ref[...]
    v_cmp = vc_ref[...]
    lane = lax.broadcasted_iota(jnp.int32, (tq, LANES), 1)
    nt = (((1,), (1,)), ((), ()))

    jj = lax.broadcasted_iota(jnp.int32, (n_slc, n_cr), 0)
    nn = lax.broadcasted_iota(jnp.int32, (n_slc, n_cr), 1)
    cs = nn * CMP_STRIDE
    ss = jj * SLC_BLOCK
    ov = jnp.maximum(jnp.minimum(cs + CMP_BLOCK, ss + SLC_BLOCK) - jnp.maximum(cs, ss), 0)
    agg_t = jnp.where(nn < n_cmp, ov.astype(F32) / CMP_STRIDE, 0.0)

    for i in range(S // tq):
        r0 = i * tq
        nk = r0 + tq
        rows = slice(r0, r0 + tq)

        spos_c = r0 + lax.broadcasted_iota(jnp.int32, (tq, n_cr), 0)
        ncol = lax.broadcasted_iota(jnp.int32, (tq, n_cr), 1)
        cmask = (ncol * CMP_STRIDE + (CMP_BLOCK - 1) <= spos_c) & (ncol < n_cmp)
        cbias = jnp.where(cmask, 0.0, NEG)
        cmask_f = cmask.astype(F32)
        q_all = q_ref[:, rows, :].reshape(hpg * tq, NSA_DH)
        sc = lax.dot_general(q_all, k_cmp, nt, preferred_element_type=F32).reshape(hpg, tq, n_cr) + cbias
        m = jnp.max(sc, -1, keepdims=True)
        e = jnp.exp2(sc - m) * cmask_f
        l = jnp.sum(e, -1, keepdims=True)
        p = e / jnp.where(l > 0.0, l, 1.0)
        ocmp_scr[...] = jnp.dot(p.reshape(hpg * tq, n_cr).astype(BF16), v_cmp,
                                preferred_element_type=F32).reshape(hpg, tq, NSA_DH)
        psum = jnp.sum(p, axis=0)

        imp = lax.dot_general(agg_t, psum, nt, precision=lax.Precision.HIGHEST,
                              preferred_element_type=F32)
        jb = lax.broadcasted_iota(jnp.int32, (n_slc, tq), 0)
        sp = r0 + lax.broadcasted_iota(jnp.int32, (n_slc, tq), 1)
        cur = sp // SLC_BLOCK
        valid = jb * SLC_BLOCK <= sp
        forced = (jb == 0) | (jb == cur) | (jb == cur - 1)
        imp = jnp.where(forced, FORCE, jnp.where(valid, imp, -1.0))
        rank = jnp.zeros((n_slc, tq), jnp.int32)
        for jp in range(n_slc):
            other = imp[jp:jp + 1, :]
            ahead = (other > imp) | ((other == imp) & (jp < jb))
            rank = rank + ahead.astype(jnp.int32)
        sel_t = (rank < n_sel).astype(BF16)

        eb = lax.broadcasted_iota(jnp.int32, (n_slc, nk), 0)
        ek = lax.broadcasted_iota(jnp.int32, (n_slc, nk), 1)
        expand = (ek // SLC_BLOCK == eb).astype(BF16)
        sel_keys = lax.dot_general(sel_t, expand, (((0,), (0,)), ((), ())), preferred_element_type=F32)
        qrow = r0 + lax.broadcasted_iota(jnp.int32, (tq, nk), 0)
        kcol = lax.broadcasted_iota(jnp.int32, (tq, nk), 1)
        sbias = jnp.where((sel_keys > 0.5) & (kcol <= qrow), 0.0, NEG)

        w0 = (max(0, r0 - WINDOW) // tq) * tq
        nw = nk - w0
        wrow = r0 + lax.broadcasted_iota(jnp.int32, (tq, nw), 0)
        wcol = w0 + lax.broadcasted_iota(jnp.int32, (tq, nw), 1)
        wdiff = wrow - wcol
        wbias = jnp.where((wdiff >= 0) & (wdiff < WINDOW), 0.0, NEG)

        z = gl_ref[rows, :] + bg_ref[...]
        gates = 1.0 / (1.0 + jnp.exp(-z))

        k_s = ks_ref[0:nk, :]
        v_s = vs_ref[0:nk, :]
        k_w = kw_ref[w0:nk, :]
        v_w = vw_ref[w0:nk, :]

        def main_body(pp, carry):
            hs = (2 * pp, 2 * pp + 1)
            qs = [qr_ref[hh, rows, :] for hh in hs]
            s_s = [lax.dot_general(q, k_s, nt, preferred_element_type=F32) + sbias for q in qs]
            s_w = [lax.dot_general(q, k_w, nt, preferred_element_type=F32) + wbias for q in qs]
            m_s = [jnp.max(s, -1, keepdims=True) for s in s_s]
            m_w = [jnp.max(s, -1, keepdims=True) for s in s_w]
            e_s = [jnp.exp2(s - m) for s, m in zip(s_s, m_s)]
            e_w = [jnp.exp2(s - m) for s, m in zip(s_w, m_w)]
            l_s = [jnp.sum(e, -1, keepdims=True) for e in e_s]
            l_w = [jnp.sum(e, -1, keepdims=True) for e in e_w]
            o_s = [jnp.dot(e.astype(BF16), v_s, preferred_element_type=F32) / l for e, l in zip(e_s, l_s)]
            o_w = [jnp.dot(e.astype(BF16), v_w, preferred_element_type=F32) / l for e, l in zip(e_w, l_w)]
            for t, hh in enumerate(hs):
                c0 = (g_idx * hpg + hh) * N_BRANCH
                gate = lambda br: jnp.sum(jnp.where(lane == c0 + br, gates, 0.0), -1, keepdims=True)
                o = gate(0) * ocmp_scr[hh] + gate(1) * o_s[t] + gate(2) * o_w[t]
                obuf_scr[hh] = o.astype(obuf_scr.dtype)
            return carry

        lax.fori_loop(0, hpg // 2, main_body, 0)
        for hh in range(hpg):
            o_ref[rows, hh * NSA_DH:(hh + 1) * NSA_DH] = obuf_scr[hh]


def _nsa_attention(q, qr, gate_logits, b_gate, shared, B, S, G, hpg):
    k_cmp, v_cmp, k_slc, v_slc, k_win, v_win = shared
    T = B * S
    H = G * hpg
    tq = min(ATTN_TQ, S)
    n_cmp = (S - CMP_BLOCK) // CMP_STRIDE + 1
    n_cr = k_cmp.shape[2]
    kern = functools.partial(_nsa_attn_kernel, S, tq, hpg, n_cmp)
    head_spec = pl.BlockSpec((hpg, S, NSA_DH), lambda b, g: (g, b, 0))
    cmp_spec = pl.BlockSpec((None, None, n_cr, NSA_DH), lambda b, g: (b, g, 0, 0))
    kv_spec = pl.BlockSpec((S, NSA_DH), lambda b, g: (b, g))
    nbytes = (2 * 2 * hpg * S * NSA_DH * 2 + 2 * S * LANES * 4 + 8 * S * NSA_DH * 2
              + 2 * S * hpg * NSA_DH * 2 + 10 * tq * S * 4)
    return pl.pallas_call(
        kern,
        name="nsa_attn",
        grid=(B, G),
        in_specs=[head_spec, head_spec,
                  pl.BlockSpec((S, LANES), lambda b, g: (b, 0)),
                  pl.BlockSpec((1, LANES), lambda b, g: (0, 0)),
                  cmp_spec, cmp_spec, kv_spec, kv_spec, kv_spec, kv_spec],
        out_specs=pl.BlockSpec((S, hpg * NSA_DH), lambda b, g: (b, g)),
        out_shape=jax.ShapeDtypeStruct((T, H * NSA_DH), BF16),
        scratch_shapes=[pltpu.VMEM((hpg, tq, NSA_DH), F32), pltpu.VMEM((hpg, tq, NSA_DH), BF16)],
        compiler_params=_params(("arbitrary", "arbitrary"), nbytes),
    )(q, qr, gate_logits, b_gate, k_cmp, v_cmp, k_slc, v_slc, k_win, v_win)


def _nsa_layer(h, u, b, B, S, G, tabs, shared, nsa_w_in, nsa_b_gate, nsa_g_q, nsa_w_o, next_norm):
    T, D = h.shape
    n_gate = nsa_b_gate.shape[-1]
    H = n_gate // N_BRANCH
    hpg = H // G
    q_width = H * NSA_DH
    assert n_gate <= LANES and hpg % 2 == 0
    c_tab, s_tab = tabs
    tm = min(1024, T)
    hp = min(4, H)
    tn = hp * NSA_DH
    extras = [(nsa_g_q[b].reshape(1, NSA_DH), (1, NSA_DH), lambda j, i: (0, 0)),
              (c_tab, (tm, LANES), lambda j, i: (i, 0)), (s_tab, (tm, LANES), lambda j, i: (i, 0))]
    head_out = (jax.ShapeDtypeStruct((H, T, NSA_DH), BF16), (hp, tm, NSA_DH), lambda j, i: (j, i, 0))
    q, qr = _mm(u, [(jnp.swapaxes(nsa_w_in, 1, 2), b, 0)],
                functools.partial(_epi_nsa_q, hp // 2, NSA_DH ** -0.5 * LOG2E),
                extras, [head_out, head_out], n_cols=q_width, tn=tn, tm=tm, w_transposed=True)
    w_gate = lax.slice(nsa_w_in, (b, 0, q_width), (b + 1, D, q_width + n_gate))[0]
    w_gate = jnp.pad(w_gate, ((0, 0), (0, LANES - n_gate)))
    gate_logits = _mm_plain(u, w_gate, None, 0, LANES, LANES, F32)
    b_gate = jnp.pad(nsa_b_gate[b], (0, LANES - n_gate)).reshape(1, LANES)
    o = _nsa_attention(q, qr, gate_logits, b_gate, shared, B, S, G, hpg)
    return _mm_residual(o, nsa_w_o, b, h, 1.0, next_norm)


def _lane_table(values):
    return jnp.asarray(values, F32).reshape(1, LANES)


def kernel(x, positions, ffn1_norm, ffn1_w_gate, ffn1_w_up, ffn1_w_down, mix_norm, ffn2_norm, ffn2_w_gate, ffn2_w_up, ffn2_w_down, mla_w_in, mla_g_cq, mla_g_ckv, mla_w_uq, mla_w_ukv, mla_g_q, mla_g_k, mla_w_o, kv_norm, kv_w, cmp_pos_k, cmp_pos_v, cmp_k_w1, cmp_k_b1, cmp_k_w2, cmp_v_w1, cmp_v_b1, cmp_v_w2, g_k_cmp, g_k_slc, g_k_win, nsa_w_in, nsa_b_gate, nsa_g_q, nsa_w_o):
    B, S, D = x.shape
    T = B * S
    depth = ffn1_norm.shape[0]
    n_a = mla_w_in.shape[0]
    G = kv_w.shape[-1] // (2 * N_BRANCH * NSA_DH)

    pos = positions.reshape(T, 1).astype(F32)
    half_a = MLA_ROPE // 2
    inv_a = jnp.power(ROPE_THETA, -jnp.arange(0, MLA_ROPE, 2, dtype=F32) / MLA_ROPE)
    zeros_a = jnp.zeros((LANES - MLA_ROPE,), F32)
    ones_h = jnp.ones((half_a,), F32)
    tabs_a = _rope_tables(pos, _lane_table(jnp.concatenate([inv_a, inv_a, zeros_a])),
                          _lane_table(jnp.concatenate([ones_h, ones_h, zeros_a])),
                          _lane_table(jnp.concatenate([-ones_h, ones_h, zeros_a])))
    inv_b = jnp.power(ROPE_THETA, -jnp.arange(0, NSA_DH, 2, dtype=F32) / NSA_DH)
    ones_b = jnp.ones((NSA_DH // 2,), F32)
    tabs_b = _rope_tables(pos, _lane_table(jnp.concatenate([inv_b, inv_b])),
                          _lane_table(jnp.concatenate([ones_b, ones_b])),
                          _lane_table(jnp.concatenate([-ones_b, ones_b])))

    h = x.reshape(T, D)
    shared = None
    act = (_rmsnorm(h, ffn1_norm, layer=0), None)
    for layer in range(depth):
        h, act = _ffn(h, act, ffn1_w_gate, ffn1_w_up, ffn1_w_down, layer, (mix_norm, layer))
        if layer < n_a:
            h, act = _mla_layer(h, act, layer, B, S, tabs_a, mla_w_in, mla_g_cq, mla_g_ckv, mla_w_uq, mla_w_ukv,
                                mla_g_q, mla_g_k, mla_w_o, (ffn2_norm, layer))
        else:
            h, act = _nsa_layer(h, act, layer - n_a, B, S, G, tabs_b, shared, nsa_w_in, nsa_b_gate, nsa_g_q,
                                nsa_w_o, (ffn2_norm, layer))
        h, act = _ffn(h, act, ffn2_w_gate, ffn2_w_up, ffn2_w_down, layer,
                      (ffn1_norm, layer + 1) if layer + 1 < depth else None)
        if layer == n_a - 1:
            shared = _nsa_shared_kv(h, B, S, tabs_b, kv_norm, kv_w, cmp_pos_k, cmp_pos_v, cmp_k_w1, cmp_k_b1,
                                    cmp_k_w2, cmp_v_w1, cmp_v_b1, cmp_v_w2, g_k_cmp, g_k_slc, g_k_win)
    return h.reshape(B, S, D)
```

```python
import functools
import math

import jax
import jax.numpy as jnp
from jax import lax
from jax.experimental import pallas as pl
from jax.experimental.pallas import tpu as pltpu

F32 = jnp.float32
BF16 = jnp.bfloat16

ROPE_THETA = 10000.0
EPS = 1e-6
NEG = -1e30
FORCE = 1e6
MLA_NOPE = 128
MLA_ROPE = 64
MLA_V = 128
MLA_QK = MLA_NOPE + MLA_ROPE
MLA_HEAD_PAD = 256
NSA_DH = 128
N_BRANCH = 3
CMP_BLOCK = 32
CMP_STRIDE = 16
SLC_BLOCK = 64
SLC_TOPK = 16
WINDOW = 512
LOG2E = math.log2(math.e)

LANES = 128
MXU_DIM = 256
V7X_VMEM_BYTES = 64 * 1024 * 1024
VMEM_CAP = V7X_VMEM_BYTES - 8 * 1024 * 1024

ATTN_TQ = 256
MLA_SCORES_AHEAD = 2


def _vmem_limit(nbytes):
    return int(min(VMEM_CAP, max(32 * 1024 * 1024, nbytes + 8 * 1024 * 1024)))


def _params(sem, nbytes):
    return pltpu.CompilerParams(dimension_semantics=sem, vmem_limit_bytes=_vmem_limit(nbytes))


def _rope_table_kernel(pos_ref, inv_ref, mc_ref, ms_ref, c_ref, s_ref):
    ang = pos_ref[...] * inv_ref[...]
    c_ref[...] = jnp.cos(ang) * mc_ref[...]
    s_ref[...] = jnp.sin(ang) * ms_ref[...]


def _rope_tables(pos, inv, mask_c, mask_s):
    T = pos.shape[0]
    tm = min(T, 1024)
    row = pl.BlockSpec((1, LANES), lambda i: (0, 0))
    out = pl.BlockSpec((tm, LANES), lambda i: (i, 0))
    return pl.pallas_call(
        _rope_table_kernel,
        name="rope_tables",
        grid=(T // tm,),
        in_specs=[pl.BlockSpec((tm, 1), lambda i: (i, 0)), row, row, row],
        out_specs=[out, out],
        out_shape=[jax.ShapeDtypeStruct((T, LANES), F32)] * 2,
        compiler_params=_params(("arbitrary",), 0),
    )(pos, inv, mask_c, mask_s)


def _rope_nsa(x, c, s):
    return x * c + pltpu.roll(x, 64, 1) * s


def _rope_mla(x, c, s):
    return x * c + pltpu.roll(x, 32, 1) * s


def _mxu_row_sum(sq, sel):
    return jnp.dot(sq.astype(BF16), sel, preferred_element_type=F32)


def _sel_matrix(shape, pred):
    r = lax.broadcasted_iota(jnp.int32, shape, 0)
    c = lax.broadcasted_iota(jnp.int32, shape, 1)
    return pred(r, c).astype(BF16)


def _rmsnorm_kernel(x_ref, g_ref, o_ref):
    x = x_ref[...]
    r = lax.rsqrt(jnp.mean(x * x, axis=-1, keepdims=True) + EPS)
    o_ref[...] = (x * r * g_ref[...]).astype(o_ref.dtype)


def _rmsnorm(x, g, *, layer=None):
    T, width = x.shape
    tm = min(T, 512)
    if layer is None:
        g = g.reshape(1, width)
        g_spec = pl.BlockSpec((1, width), lambda i: (0, 0))
    else:
        g = g.reshape(g.shape[0], 1, width)
        g_spec = pl.BlockSpec((None, 1, width), lambda i: (layer, 0, 0))
    return pl.pallas_call(
        _rmsnorm_kernel,
        name="rmsnorm",
        grid=(T // tm,),
        in_specs=[pl.BlockSpec((tm, width), lambda i: (i, 0)), g_spec],
        out_specs=pl.BlockSpec((tm, width), lambda i: (i, 0)),
        out_shape=jax.ShapeDtypeStruct((T, width), BF16),
        compiler_params=_params(("arbitrary",), 6 * tm * width * 4),
    )(x, g)


def _scale_rows(acc, rs):
    return jnp.concatenate([acc[:, k * LANES:(k + 1) * LANES] * rs for k in range(acc.shape[1] // LANES)], axis=1)


def _mm_kernel(n_w, has_rs, w_transposed, n_extra, n_out, epilogue, x_ref, *refs):
    refs = list(refs)
    rs_ref = refs.pop(0) if has_rs else None
    w_refs = refs[:n_w]
    extra = refs[n_w:n_w + n_extra]
    outs = refs[n_w + n_extra:n_w + n_extra + n_out]
    w_bf = refs[n_w + n_extra + n_out:2 * n_w + n_extra + n_out]
    scr = refs[2 * n_w + n_extra + n_out:]

    @pl.when(pl.program_id(1) == 0)
    def _():
        for w, s in zip(w_refs, w_bf):
            s[...] = w[...].astype(BF16)

    if scr:
        @pl.when((pl.program_id(0) == 0) & (pl.program_id(1) == 0))
        def _():
            for s in scr:
                s[...] = jnp.zeros(s.shape, s.dtype)

    x = x_ref[...]
    dims = (((1,), (1 if w_transposed else 0,)), ((), ()))
    accs = [lax.dot_general(x, s[...], dims, preferred_element_type=F32) for s in w_bf]
    if has_rs:
        rs = rs_ref[...]
        accs = [_scale_rows(a, rs) for a in accs]
    epilogue(accs, extra, outs, scr)


def _mm(act, weights, epilogue, extras, outs, *, n_cols, tn, tm=1024, scratch=(), w_transposed=False):
    x, row_scale = act
    M, K = x.shape
    tm = min(tm, M)
    assert M % tm == 0 and n_cols % tn == 0
    in_specs = [pl.BlockSpec((tm, K), lambda j, i: (i, 0))]
    args = [x]
    nbytes = 2 * tm * K * 2 + len(weights) * (2 * K * tn * 4 + K * tn * 2 + 2 * tm * tn * 4)
    if row_scale is not None:
        in_specs.append(pl.BlockSpec((tm, LANES), lambda j, i: (i, 0)))
        args.append(row_scale)
        nbytes += 2 * tm * LANES * 4
    for arr, layer, off in weights:
        if w_transposed:
            assert layer is not None
            in_specs.append(pl.BlockSpec((None, tn, K), lambda j, i, off=off, layer=layer: (layer, j + off, 0)))
        elif layer is None:
            in_specs.append(pl.BlockSpec((K, tn), lambda j, i, off=off: (0, j + off)))
        else:
            in_specs.append(pl.BlockSpec((None, K, tn), lambda j, i, off=off, layer=layer: (layer, 0, j + off)))
        args.append(arr)
    for arr, bs, im in extras:
        in_specs.append(pl.BlockSpec(bs, im))
        args.append(arr)
        nbytes += 2 * 4 * functools.reduce(lambda a, b: a * (b or 1), bs, 1)
    out_specs, out_shapes = [], []
    for sds, bs, im in outs:
        out_specs.append(pl.BlockSpec(bs, im))
        out_shapes.append(sds)
        nbytes += 2 * sds.dtype.itemsize * functools.reduce(lambda a, b: a * (b or 1), bs, 1)
    for shape, dtype in scratch:
        nbytes += jnp.dtype(dtype).itemsize * functools.reduce(lambda a, b: a * b, shape, 1)
    kern = functools.partial(_mm_kernel, len(weights), row_scale is not None, w_transposed, len(extras), len(outs),
                             epilogue)
    res = pl.pallas_call(
        kern,
        name="mm_" + getattr(epilogue, "func", epilogue).__name__[len("_epi_"):],
        grid=(n_cols // tn, M // tm),
        in_specs=in_specs,
        out_specs=out_specs,
        out_shape=out_shapes,
        scratch_shapes=[pltpu.VMEM((tn, K) if w_transposed else (K, tn), BF16) for _ in weights]
                       + [pltpu.VMEM(s, d) for s, d in scratch],
        compiler_params=_params(("arbitrary", "arbitrary"), nbytes),
    )(*args)
    return res


def _epi_plain(accs, extra, outs, scr):
    outs[0][...] = accs[0].astype(outs[0].dtype)


def _epi_swiglu(accs, extra, outs, scr):
    g, u = accs
    outs[0][...] = (g * (1.0 / (1.0 + jnp.exp(-g))) * u).astype(outs[0].dtype)


def _epi_residual(alpha, accs, extra, outs, scr):
    outs[0][...] = extra[0][...] + alpha * accs[0]


def _epi_residual_norm(alpha, width, accs, extra, outs, scr):
    res_ref = extra[0]
    h_o, rs_o = outs[0], outs[-1]
    ss = scr[0]
    i = pl.program_id(1)
    h = res_ref[...] + alpha * accs[0]
    h_o[...] = h
    for g_ref, hg_o in zip(extra[1:], outs[1:-1]):
        hg_o[...] = (h * g_ref[...]).astype(hg_o.dtype)
    sq = h * h
    part = sq[:, :LANES]
    for k in range(1, sq.shape[1] // LANES):
        part = part + sq[:, k * LANES:(k + 1) * LANES]
    tot = ss[i] + part
    ss[i] = tot
    rs_o[...] = jnp.broadcast_to(lax.rsqrt(jnp.sum(tot, -1, keepdims=True) / width + EPS), rs_o.shape)


def _mm_plain(act, w, layer, col_off, n_cols, tn, dtype, w_transposed=False):
    M = act[0].shape[0]
    tm = min(1024, M)
    out = (jax.ShapeDtypeStruct((M, n_cols), dtype), (tm, tn), lambda j, i: (i, j))
    return _mm(act, [(w, layer, col_off)], _epi_plain, [], [out], n_cols=n_cols, tn=tn, tm=tm,
               w_transposed=w_transposed)[0]


def _mm_residual(x, w, layer, res, alpha, next_norms=()):
    M, N = res.shape
    K = x.shape[1]
    tn = 512 if N % 512 == 0 else 256
    tm = min(1024 if K <= 4096 else 512, M)
    blk = ((tm, tn), lambda j, i: (i, j))
    h_out = (jax.ShapeDtypeStruct((M, N), F32),) + blk
    if not next_norms:
        return _mm((x, None), [(w, layer, 0)], functools.partial(_epi_residual, alpha), [(res,) + blk],
                   [h_out], n_cols=N, tn=tn, tm=tm)[0], []
    g_specs = [(gains.reshape(-1, 1, N), (None, 1, tn), lambda j, i, gl=gl: (gl, 0, j)) for gains, gl in next_norms]
    hg_outs = [(jax.ShapeDtypeStruct((M, N), BF16),) + blk for _ in next_norms]
    last_j = N // tn - 1
    rs_out = (jax.ShapeDtypeStruct((M, LANES), F32), (tm, LANES), lambda j, i: (jnp.where(j == last_j, i, 0), 0))
    res_all = _mm((x, None), [(w, layer, 0)], functools.partial(_epi_residual_norm, alpha, N),
                  [(res,) + blk] + g_specs, [h_out] + hg_outs + [rs_out],
                  n_cols=N, tn=tn, tm=tm, scratch=[((M // tm, tm, LANES), F32)])
    return res_all[0], [(hg, res_all[-1]) for hg in res_all[1:-1]]


def _ffn(h, act, wg, wu, wd, layer, next_norms):
    M, D = h.shape
    F = wg.shape[-1]
    tn = 512 if F % 512 == 0 else 256
    tm = min(512, M)
    hid = _mm(act, [(wg, layer, 0), (wu, layer, 0)], _epi_swiglu, [],
              [(jax.ShapeDtypeStruct((M, F), BF16), (tm, tn), lambda j, i: (i, j))],
              n_cols=F, tn=tn, tm=tm)[0]
    return _mm_residual(hid, wd, layer, h, 0.5, next_norms)


def _epi_mla_q(n_heads, out_scale, accs, extra, outs, scr):
    g = extra[0][...] * out_scale
    c, s = extra[1][...], extra[2][...]
    o = outs[0]
    sel = _sel_matrix((MLA_HEAD_PAD, MLA_HEAD_PAD), lambda r, col: r < MLA_QK)
    acc = accs[0]
    for hh in range(n_heads):
        b0 = hh * MLA_HEAD_PAD
        a = acc[:, b0:b0 + MLA_HEAD_PAD]
        r = lax.rsqrt(_mxu_row_sum(a * a, sel) / MLA_QK + EPS)
        an = a * r * g
        o[:, b0:b0 + LANES] = an[:, :LANES].astype(o.dtype)
        o[:, b0 + LANES:b0 + 2 * LANES] = _rope_mla(an[:, LANES:], c, s).astype(o.dtype)


def _epi_mla_kv(n_heads, accs, extra, outs, scr):
    g = extra[0][...]
    c, s = extra[1][...], extra[2][...]
    kr = extra[3][...]
    kr2 = kr * kr
    k_o, v_o = outs
    sel = _sel_matrix((MLA_HEAD_PAD, LANES), lambda r, col: r < MLA_QK)
    acc = accs[0]
    for hh in range(n_heads):
        b0 = hh * MLA_HEAD_PAD
        a = acc[:, b0:b0 + MLA_HEAD_PAD]
        kn = a[:, :LANES]
        ss = _mxu_row_sum(jnp.concatenate([kn * kn, kr2], axis=1), sel)
        r = lax.rsqrt(ss / MLA_QK + EPS)
        k_o[:, b0:b0 + LANES] = (kn * r * g[:, :LANES]).astype(k_o.dtype)
        k_o[:, b0 + LANES:b0 + 2 * LANES] = _rope_mla(kr * r * g[:, LANES:], c, s).astype(k_o.dtype)
        v_o[:, hh * MLA_V:(hh + 1) * MLA_V] = a[:, LANES:].astype(v_o.dtype)


def _pad_gain_mla(g):
    return jnp.concatenate([g, g[MLA_NOPE:]]).reshape(1, MLA_HEAD_PAD)


def _mla_attn_kernel(S, tq, nh, q_ref, k_ref, v_ref, o_ref):
    row = lax.broadcasted_iota(jnp.int32, (tq, tq), 0)
    col = lax.broadcasted_iota(jnp.int32, (tq, tq), 1)
    tri = jnp.where(col <= row, 0.0, NEG)
    nt = (((1,), (1,)), ((), ()))

    def scores(i, hh):
        r0 = i * tq
        ks = slice(hh * MLA_HEAD_PAD, (hh + 1) * MLA_HEAD_PAD)
        q = q_ref[r0:r0 + tq, ks]
        s_d = lax.dot_general(q, k_ref[r0:r0 + tq, ks], nt, preferred_element_type=F32) + tri
        s_o = lax.dot_general(q, k_ref[0:r0, ks], nt, preferred_element_type=F32) if i > 0 else None
        return s_d, s_o

    def finish(i, hh, s_d, s_o):
        r0 = i * tq
        vs = slice(hh * MLA_V, (hh + 1) * MLA_V)
        m = jnp.max(s_d, -1, keepdims=True)
        if i > 0:
            m = jnp.maximum(m, jnp.max(s_o, -1, keepdims=True))
        e_d = jnp.exp2(s_d - m)
        l = jnp.sum(e_d, -1, keepdims=True)
        o = jnp.dot(e_d.astype(BF16), v_ref[r0:r0 + tq, vs], preferred_element_type=F32)
        if i > 0:
            e_o = jnp.exp2(s_o - m)
            l = l + jnp.sum(e_o, -1, keepdims=True)
            o = o + jnp.dot(e_o.astype(BF16), v_ref[0:r0, vs], preferred_element_type=F32)
        o_ref[r0:r0 + tq, vs] = (o / l).astype(o_ref.dtype)

    tiles = [(i, hh) for i in reversed(range(S // tq)) for hh in range(nh)]
    pending = [scores(*t) for t in tiles[:MLA_SCORES_AHEAD]]
    for n, t in enumerate(tiles):
        if n + MLA_SCORES_AHEAD < len(tiles):
            pending.append(scores(*tiles[n + MLA_SCORES_AHEAD]))
        finish(*t, *pending.pop(0))


def _mla_attention(q, k, v, B, S, H):
    T = B * S
    tq = min(ATTN_TQ, S)
    nh = 2 if H % 2 == 0 else 1
    kern = functools.partial(_mla_attn_kernel, S, tq, nh)
    nbytes = 2 * nh * S * (2 * MLA_HEAD_PAD + 2 * MLA_V) * 2 + 12 * tq * S * 4
    return pl.pallas_call(
        kern,
        name="mla_attn",
        grid=(B, H // nh),
        in_specs=[pl.BlockSpec((S, nh * MLA_HEAD_PAD), lambda b, h: (b, h)),
                  pl.BlockSpec((S, nh * MLA_HEAD_PAD), lambda b, h: (b, h)),
                  pl.BlockSpec((S, nh * MLA_V), lambda b, h: (b, h))],
        out_specs=pl.BlockSpec((S, nh * MLA_V), lambda b, h: (b, h)),
        out_shape=jax.ShapeDtypeStruct((T, H * MLA_V), BF16),
        compiler_params=_params(("arbitrary", "arbitrary"), nbytes),
    )(q, k, v)


def _mla_layer(h, act, a, B, S, tabs, mla_w_in, mla_g_cq, mla_g_ckv, mla_w_uq, mla_w_ukv, mla_g_q, mla_g_k, mla_w_o,
               next_norms):
    T, D = h.shape
    q_lora = mla_g_cq.shape[-1]
    kv_lora = mla_g_ckv.shape[-1]
    H = mla_w_ukv.shape[-1] // (MLA_NOPE + MLA_V)
    c_tab, s_tab = tabs
    tm = min(2048, T)
    lat = q_lora + kv_lora
    assert q_lora % kv_lora == 0
    w_in_t = jnp.swapaxes(mla_w_in, 1, 2)
    cq = _mm_plain(act, w_in_t, a, 0, q_lora, min(512, q_lora), F32, w_transposed=True)
    ckv = _mm_plain(act, w_in_t, a, q_lora // kv_lora, kv_lora, kv_lora, F32, w_transposed=True)
    w_kr = lax.slice(mla_w_in, (a, 0, lat), (a + 1, D, lat + MLA_ROPE))[0]
    kr = _mm_plain(act, jnp.concatenate([w_kr, w_kr], axis=1), None, 0, LANES, LANES, F32)
    cq = _rmsnorm(cq, mla_g_cq[a])
    ckv = _rmsnorm(ckv, mla_g_ckv[a])
    w_uq = mla_w_uq[a].reshape(q_lora, H, MLA_QK)
    w_uq = jnp.concatenate([w_uq, w_uq[:, :, MLA_NOPE:]], axis=2).reshape(q_lora, H * MLA_HEAD_PAD)
    hp = 2 if H % 2 == 0 else 1
    tn = hp * MLA_HEAD_PAD
    tab_specs = [(t, (tm, LANES), lambda j, i: (i, 0)) for t in (c_tab, s_tab)]
    gain = lambda g: (_pad_gain_mla(g), (1, MLA_HEAD_PAD), lambda j, i: (0, 0))
    q = _mm((cq, None), [(w_uq, None, 0)], functools.partial(_epi_mla_q, hp, MLA_QK ** -0.5 * LOG2E),
            [gain(mla_g_q[a])] + tab_specs,
            [(jax.ShapeDtypeStruct((T, H * MLA_HEAD_PAD), BF16), (tm, tn), lambda j, i: (i, j))],
            n_cols=H * MLA_HEAD_PAD, tn=tn, tm=tm)[0]
    kr_spec = (kr, (tm, LANES), lambda j, i: (i, 0))
    k, v = _mm((ckv, None), [(mla_w_ukv, a, 0)], functools.partial(_epi_mla_kv, hp),
               [gain(mla_g_k[a])] + tab_specs + [kr_spec],
               [(jax.ShapeDtypeStruct((T, H * MLA_HEAD_PAD), BF16), (tm, tn), lambda j, i: (i, j)),
                (jax.ShapeDtypeStruct((T, H * MLA_V), BF16), (tm, hp * MLA_V), lambda j, i: (i, j))],
               n_cols=H * MLA_HEAD_PAD, tn=tn, tm=tm)
    o = _mla_attention(q, k, v, B, S, H)
    return _mm_residual(o, mla_w_o, a, h, 1.0, next_norms)


def _nsa_head_norm(a, g2):
    sel = _sel_matrix((MXU_DIM, MXU_DIM), lambda r, col: r // NSA_DH == col // NSA_DH)
    r = lax.rsqrt(_mxu_row_sum(a * a, sel) / NSA_DH + EPS)
    return a * r * g2


def _epi_nsa_k(n_pairs, accs, extra, outs, scr):
    g = extra[0][...]
    g2 = jnp.concatenate([g, g], axis=1)
    c, s = extra[1][...], extra[2][...]
    o = outs[0]
    acc = accs[0]
    for pp in range(n_pairs):
        kn = _nsa_head_norm(acc[:, pp * MXU_DIM:(pp + 1) * MXU_DIM], g2)
        for hh in range(2):
            col = (2 * pp + hh) * NSA_DH
            o[:, col:col + NSA_DH] = _rope_nsa(kn[:, hh * NSA_DH:(hh + 1) * NSA_DH], c, s).astype(o.dtype)


def _compress_kernel(n_cmp, k_ref, v_ref, pk_ref, pv_ref, kw1_ref, kb1_ref, kw2_ref,
                     vw1_ref, vb1_ref, vw2_ref, gk_ref, ko_ref, vo_ref):
    half = CMP_STRIDE * NSA_DH
    n_rows = ko_ref.shape[0]
    row = lax.broadcasted_iota(jnp.int32, (n_rows, 1), 0)

    def phi(t_ref, pos_ref, w1_ref, b1_ref, w2_ref):
        slabs = [t_ref[pl.ds(l, n_rows, stride=CMP_STRIDE), :] for l in range(CMP_STRIDE)]
        lo = jnp.concatenate([slabs[l] + pos_ref[:, l * NSA_DH:(l + 1) * NSA_DH]
                              for l in range(CMP_STRIDE)], axis=1).astype(BF16)
        hi = jnp.concatenate([slabs[l] + pos_ref[:, half + l * NSA_DH:half + (l + 1) * NSA_DH]
                              for l in range(CMP_STRIDE)], axis=1).astype(BF16)
        p_lo = jnp.dot(lo, w1_ref[0:half, :].astype(BF16), preferred_element_type=F32)
        p_hi = jnp.dot(hi, w1_ref[half:2 * half, :].astype(BF16), preferred_element_type=F32)
        pre = p_lo + pltpu.roll(p_hi, n_rows - 1, 0) + b1_ref[...]
        hid = pre * (1.0 / (1.0 + jnp.exp(-pre)))
        out = jnp.dot(hid.astype(BF16), w2_ref[...].astype(BF16), preferred_element_type=F32)
        return jnp.where(row < n_cmp, out, 0.0)

    kc = phi(k_ref, pk_ref, kw1_ref, kb1_ref, kw2_ref)
    r = lax.rsqrt(jnp.mean(kc * kc, -1, keepdims=True) + EPS)
    ko_ref[...] = (kc * r * gk_ref[...]).astype(ko_ref.dtype)
    vo_ref[...] = phi(v_ref, pv_ref, vw1_ref, vb1_ref, vw2_ref).astype(vo_ref.dtype)


def _compress(raw, B, S, G, cmp_pos_k, cmp_pos_v, k_w1, k_b1, k_w2, v_w1, v_b1, v_w2, g_k_cmp):
    n_cmp = (S - CMP_BLOCK) // CMP_STRIDE + 1
    n_rows = S // CMP_STRIDE
    hid = k_w1.shape[-1]
    full = lambda shape: pl.BlockSpec(shape, lambda b, g: (0,) * len(shape))
    w_specs = [full((CMP_BLOCK * NSA_DH, hid)), full((1, hid)), full((hid, NSA_DH))]
    out_spec = pl.BlockSpec((None, None, n_rows, NSA_DH), lambda b, g: (b, g, 0, 0))
    out_sds = jax.ShapeDtypeStruct((B, G, n_rows, NSA_DH), BF16)
    nbytes = 4 * (2 * CMP_BLOCK * NSA_DH * hid * 4) + 4 * S * NSA_DH * 4 + 8 * n_rows * CMP_BLOCK * NSA_DH * 4
    return pl.pallas_call(
        functools.partial(_compress_kernel, n_cmp),
        name="nsa_compress",
        grid=(B, G),
        in_specs=[pl.BlockSpec((S, NSA_DH), lambda b, g: (b, g)),
                  pl.BlockSpec((S, NSA_DH), lambda b, g: (b, G + g)),
                  full((1, CMP_BLOCK * NSA_DH)), full((1, CMP_BLOCK * NSA_DH))]
                 + w_specs + w_specs + [full((1, NSA_DH))],
        out_specs=[out_spec, out_spec],
        out_shape=[out_sds, out_sds],
        compiler_params=_params(("arbitrary", "arbitrary"), nbytes),
    )(raw, raw, cmp_pos_k.reshape(1, -1), cmp_pos_v.reshape(1, -1),
      k_w1, k_b1.reshape(1, -1), k_w2, v_w1, v_b1.reshape(1, -1), v_w2, g_k_cmp.reshape(1, -1))


def _nsa_shared_kv(y, B, S, tabs, kv_w, cmp_pos_k, cmp_pos_v, cmp_k_w1, cmp_k_b1, cmp_k_w2,
                   cmp_v_w1, cmp_v_b1, cmp_v_w2, g_k_cmp, g_k_slc, g_k_win):
    T = y[0].shape[0]
    G = kv_w.shape[-1] // (2 * N_BRANCH * NSA_DH)
    assert G % 2 == 0
    part = G * NSA_DH
    tm = min(1024, T)
    c_tab, s_tab = tabs
    raw = _mm_plain(y, kv_w, None, 0, 2 * part, part, F32)
    k_cmp, v_cmp = _compress(raw, B, S, G, cmp_pos_k, cmp_pos_v, cmp_k_w1, cmp_k_b1, cmp_k_w2,
                             cmp_v_w1, cmp_v_b1, cmp_v_w2, g_k_cmp)

    def k_branch(part_idx, gain):
        extras = [(gain.reshape(1, NSA_DH), (1, NSA_DH), lambda j, i: (0, 0)),
                  (c_tab, (tm, LANES), lambda j, i: (i, 0)), (s_tab, (tm, LANES), lambda j, i: (i, 0))]
        return _mm(y, [(kv_w, None, part_idx)], functools.partial(_epi_nsa_k, G // 2), extras,
                   [(jax.ShapeDtypeStruct((T, part), BF16), (tm, part), lambda j, i: (i, j))],
                   n_cols=part, tn=part, tm=tm)[0]

    k_slc = k_branch(2, g_k_slc)
    v_slc = _mm_plain(y, kv_w, None, 3, part, part, BF16)
    k_win = k_branch(4, g_k_win)
    v_win = _mm_plain(y, kv_w, None, 5, part, part, BF16)
    return k_cmp, v_cmp, k_slc, v_slc, k_win, v_win


def _epi_nsa_q(n_pairs, out_scale, accs, extra, outs, scr):
    g = extra[0][...] * out_scale
    g2 = jnp.concatenate([g, g], axis=1)
    c, s = extra[1][...], extra[2][...]
    q_o, qr_o = outs
    acc = accs[0]
    for pp in range(n_pairs):
        qn = _nsa_head_norm(acc[:, pp * MXU_DIM:(pp + 1) * MXU_DIM], g2)
        for hh in range(2):
            x = qn[:, hh * NSA_DH:(hh + 1) * NSA_DH]
            q_o[2 * pp + hh] = x.astype(q_o.dtype)
            qr_o[2 * pp + hh] = _rope_nsa(x, c, s).astype(qr_o.dtype)


def _nsa_attn_kernel(S, tq, hpg, n_cmp,
                     q_ref, qr_ref, gl_ref, bg_ref, kc_ref, vc_ref, ks_ref, vs_ref, kw_ref, vw_ref,
                     o_ref, ocmp_scr, obuf_scr):
    g_idx = pl.program_id(1)
    n_cr = kc_ref.shape[0]
    n_slc = S // SLC_BLOCK
    n_sel = min(SLC_TOPK, n_slc)
    k_cmp = kc_ref[...]
    v_cmp = vc_ref[...]
    lane = lax.broadcasted_iota(jnp.int32, (tq, LANES), 1)
    nt = (((1,), (1,)), ((), ()))

    jj = lax.broadcasted_iota(jnp.int32, (n_slc, n_cr), 0)
    nn = lax.broadcasted_iota(jnp.int32, (n_slc, n_cr), 1)
    cs = nn * CMP_STRIDE
    ss = jj * SLC_BLOCK
    ov = jnp.maximum(jnp.minimum(cs + CMP_BLOCK, ss + SLC_BLOCK) - jnp.maximum(cs, ss), 0)
    agg_t = jnp.where(nn < n_cmp, ov.astype(F32) / CMP_STRIDE, 0.0)

    for i in range(S // tq):
        r0 = i * tq
        nk = r0 + tq
        rows = slice(r0, r0 + tq)

        spos_c = r0 + lax.broadcasted_iota(jnp.int32, (tq, n_cr), 0)
        ncol = lax.broadcasted_iota(jnp.int32, (tq, n_cr), 1)
        cmask = (ncol * CMP_STRIDE + (CMP_BLOCK - 1) <= spos_c) & (ncol < n_cmp)
        cbias = jnp.where(cmask, 0.0, NEG)
        cmask_f = cmask.astype(F32)
        q_all = q_ref[:, rows, :].reshape(hpg * tq, NSA_DH)
        sc = lax.dot_general(q_all, k_cmp, nt, preferred_element_type=F32).reshape(hpg, tq, n_cr) + cbias
        m = jnp.max(sc, -1, keepdims=True)
        e = jnp.exp2(sc - m) * cmask_f
        l = jnp.sum(e, -1, keepdims=True)
        p = e / jnp.where(l > 0.0, l, 1.0)
        ocmp_scr[...] = jnp.dot(p.reshape(hpg * tq, n_cr).astype(BF16), v_cmp,
                                preferred_element_type=F32).reshape(hpg, tq, NSA_DH)
        psum = jnp.sum(p, axis=0)

        imp = lax.dot_general(agg_t, psum, nt, precision=lax.Precision.HIGHEST,
                              preferred_element_type=F32)
        jb = lax.broadcasted_iota(jnp.int32, (n_slc, tq), 0)
        sp = r0 + lax.broadcasted_iota(jnp.int32, (n_slc, tq), 1)
        cur = sp // SLC_BLOCK
        valid = jb * SLC_BLOCK <= sp
        forced = (jb == 0) | (jb == cur) | (jb == cur - 1)
        imp = jnp.where(forced, FORCE, jnp.where(valid, imp, -1.0))
        rank = jnp.zeros((n_slc, tq), jnp.int32)
        for jp in range(n_slc):
            other = imp[jp:jp + 1, :]
            ahead = (other > imp) | ((other == imp) & (jp < jb))
            rank = rank + ahead.astype(jnp.int32)
        sel_t = (rank < n_sel).astype(BF16)

        eb = lax.broadcasted_iota(jnp.int32, (n_slc, nk), 0)
        ek = lax.broadcasted_iota(jnp.int32, (n_slc, nk), 1)
        expand = (ek // SLC_BLOCK == eb).astype(BF16)
        sel_keys = lax.dot_general(sel_t, expand, (((0,), (0,)), ((), ())), preferred_element_type=F32)
        qrow = r0 + lax.broadcasted_iota(jnp.int32, (tq, nk), 0)
        kcol = lax.broadcasted_iota(jnp.int32, (tq, nk), 1)
        sbias = jnp.where((sel_keys > 0.5) & (kcol <= qrow), 0.0, NEG)

        w0 = (max(0, r0 - WINDOW) // tq) * tq
        nw = nk - w0
        wrow = r0 + lax.broadcasted_iota(jnp.int32, (tq, nw), 0)
        wcol = w0 + lax.broadcasted_iota(jnp.int32, (tq, nw), 1)
        wdiff = wrow - wcol
        wbias = jnp.where((wdiff >= 0) & (wdiff < WINDOW), 0.0, NEG)

        z = gl_ref[rows, :] + bg_ref[...]
        gates = 1.0 / (1.0 + jnp.exp(-z))

        k_s = ks_ref[0:nk, :]
        v_s = vs_ref[0:nk, :]
        k_w = kw_ref[w0:nk, :]
        v_w = vw_ref[w0:nk, :]

        def main_body(pp, carry):
            hs = (2 * pp, 2 * pp + 1)
            qs = [qr_ref[hh, rows, :] for hh in hs]
            s_s = [lax.dot_general(q, k_s, nt, preferred_element_type=F32) + sbias for q in qs]
            s_w = [lax.dot_general(q, k_w, nt, preferred_element_type=F32) + wbias for q in qs]
            m_s = [jnp.max(s, -1, keepdims=True) for s in s_s]
            m_w = [jnp.max(s, -1, keepdims=True) for s in s_w]
            e_s = [jnp.exp2(s - m) for s, m in zip(s_s, m_s)]
            e_w = [jnp.exp2(s - m) for s, m in zip(s_w, m_w)]
            l_s = [jnp.sum(e, -1, keepdims=True) for e in e_s]
            l_w = [jnp.sum(e, -1, keepdims=True) for e in e_w]
            o_s = [jnp.dot(e.astype(BF16), v_s, preferred_element_type=F32) / l for e, l in zip(e_s, l_s)]
            o_w = [jnp.dot(e.astype(BF16), v_w, preferred_element_type=F32) / l for e, l in zip(e_w, l_w)]
            for t, hh in enumerate(hs):
                c0 = (g_idx * hpg + hh) * N_BRANCH
                gate = lambda br: jnp.sum(jnp.where(lane == c0 + br, gates, 0.0), -1, keepdims=True)
                o = gate(0) * ocmp_scr[hh] + gate(1) * o_s[t] + gate(2) * o_w[t]
                obuf_scr[hh] = o.astype(obuf_scr.dtype)
            return carry

        lax.fori_loop(0, hpg // 2, main_body, 0)
        for hh in range(hpg):
            o_ref[rows, hh * NSA_DH:(hh + 1) * NSA_DH] = obuf_scr[hh]


def _nsa_attention(q, qr, gate_logits, b_gate, shared, B, S, G, hpg):
    k_cmp, v_cmp, k_slc, v_slc, k_win, v_win = shared
    T = B * S
    H = G * hpg
    tq = min(ATTN_TQ, S)
    n_cmp = (S - CMP_BLOCK) // CMP_STRIDE + 1
    n_cr = k_cmp.shape[2]
    kern = functools.partial(_nsa_attn_kernel, S, tq, hpg, n_cmp)
    head_spec = pl.BlockSpec((hpg, S, NSA_DH), lambda b, g: (g, b, 0))
    cmp_spec = pl.BlockSpec((None, None, n_cr, NSA_DH), lambda b, g: (b, g, 0, 0))
    kv_spec = pl.BlockSpec((S, NSA_DH), lambda b, g: (b, g))
    nbytes = (2 * 2 * hpg * S * NSA_DH * 2 + 2 * S * LANES * 4 + 8 * S * NSA_DH * 2
              + 2 * S * hpg * NSA_DH * 2 + 10 * tq * S * 4)
    return pl.pallas_call(
        kern,
        name="nsa_attn",
        grid=(B, G),
        in_specs=[head_spec, head_spec,
                  pl.BlockSpec((S, LANES), lambda b, g: (b, 0)),
                  pl.BlockSpec((1, LANES), lambda b, g: (0, 0)),
                  cmp_spec, cmp_spec, kv_spec, kv_spec, kv_spec, kv_spec],
        out_specs=pl.BlockSpec((S, hpg * NSA_DH), lambda b, g: (b, g)),
        out_shape=jax.ShapeDtypeStruct((T, H * NSA_DH), BF16),
        scratch_shapes=[pltpu.VMEM((hpg, tq, NSA_DH), F32), pltpu.VMEM((hpg, tq, NSA_DH), BF16)],
        compiler_params=_params(("arbitrary", "arbitrary"), nbytes),
    )(q, qr, gate_logits, b_gate, k_cmp, v_cmp, k_slc, v_slc, k_win, v_win)


def _nsa_layer(h, u, b, B, S, G, tabs, shared, nsa_w_in, nsa_b_gate, nsa_g_q, nsa_w_o, next_norms):
    T, D = h.shape
    n_gate = nsa_b_gate.shape[-1]
    H = n_gate // N_BRANCH
    hpg = H // G
    q_width = H * NSA_DH
    assert n_gate <= LANES and hpg % 2 == 0
    c_tab, s_tab = tabs
    tm = min(1024, T)
    hp = min(4, H)
    tn = hp * NSA_DH
    extras = [(nsa_g_q[b].reshape(1, NSA_DH), (1, NSA_DH), lambda j, i: (0, 0)),
              (c_tab, (tm, LANES), lambda j, i: (i, 0)), (s_tab, (tm, LANES), lambda j, i: (i, 0))]
    head_out = (jax.ShapeDtypeStruct((H, T, NSA_DH), BF16), (hp, tm, NSA_DH), lambda j, i: (j, i, 0))
    q, qr = _mm(u, [(jnp.swapaxes(nsa_w_in, 1, 2), b, 0)],
                functools.partial(_epi_nsa_q, hp // 2, NSA_DH ** -0.5 * LOG2E),
                extras, [head_out, head_out], n_cols=q_width, tn=tn, tm=tm, w_transposed=True)
    w_gate = lax.slice(nsa_w_in, (b, 0, q_width), (b + 1, D, q_width + n_gate))[0]
    w_gate = jnp.pad(w_gate, ((0, 0), (0, LANES - n_gate)))
    gate_logits = _mm_plain(u, w_gate, None, 0, LANES, LANES, F32)
    b_gate = jnp.pad(nsa_b_gate[b], (0, LANES - n_gate)).reshape(1, LANES)
    o = _nsa_attention(q, qr, gate_logits, b_gate, shared, B, S, G, hpg)
    return _mm_residual(o, nsa_w_o, b, h, 1.0, next_norms)


def _lane_table(values):
    return jnp.asarray(values, F32).reshape(1, LANES)


def kernel(x, positions, ffn1_norm, ffn1_w_gate, ffn1_w_up, ffn1_w_down, mix_norm, ffn2_norm, ffn2_w_gate, ffn2_w_up, ffn2_w_down, mla_w_in, mla_g_cq, mla_g_ckv, mla_w_uq, mla_w_ukv, mla_g_q, mla_g_k, mla_w_o, kv_norm, kv_w, cmp_pos_k, cmp_pos_v, cmp_k_w1, cmp_k_b1, cmp_k_w2, cmp_v_w1, cmp_v_b1, cmp_v_w2, g_k_cmp, g_k_slc, g_k_win, nsa_w_in, nsa_b_gate, nsa_g_q, nsa_w_o):
    B, S, D = x.shape
    T = B * S
    depth = ffn1_norm.shape[0]
    n_a = mla_w_in.shape[0]
    G = kv_w.shape[-1] // (2 * N_BRANCH * NSA_DH)

    pos = positions.reshape(T, 1).astype(F32)
    half_a = MLA_ROPE // 2
    inv_a = jnp.power(ROPE_THETA, -jnp.arange(0, MLA_ROPE, 2, dtype=F32) / MLA_ROPE)
    zeros_a = jnp.zeros((LANES - MLA_ROPE,), F32)
    ones_h = jnp.ones((half_a,), F32)
    tabs_a = _rope_tables(pos, _lane_table(jnp.concatenate([inv_a, inv_a, zeros_a])),
                          _lane_table(jnp.concatenate([ones_h, ones_h, zeros_a])),
                          _lane_table(jnp.concatenate([-ones_h, ones_h, zeros_a])))
    inv_b = jnp.power(ROPE_THETA, -jnp.arange(0, NSA_DH, 2, dtype=F32) / NSA_DH)
    ones_b = jnp.ones((NSA_DH // 2,), F32)
    tabs_b = _rope_tables(pos, _lane_table(jnp.concatenate([inv_b, inv_b])),
                          _lane_table(jnp.concatenate([ones_b, ones_b])),
                          _lane_table(jnp.concatenate([-ones_b, ones_b])))

    h = x.reshape(T, D)
    shared = None
    act = (_rmsnorm(h, ffn1_norm, layer=0), None)
    for layer in range(depth):
        h, (act,) = _ffn(h, act, ffn1_w_gate, ffn1_w_up, ffn1_w_down, layer, [(mix_norm, layer)])
        if layer < n_a:
            h, (act,) = _mla_layer(h, act, layer, B, S, tabs_a, mla_w_in, mla_g_cq, mla_g_ckv, mla_w_uq, mla_w_ukv,
                                   mla_g_q, mla_g_k, mla_w_o, [(ffn2_norm, layer)])
        else:
            h, (act,) = _nsa_layer(h, act, layer - n_a, B, S, G, tabs_b, shared, nsa_w_in, nsa_b_gate, nsa_g_q,
                                   nsa_w_o, [(ffn2_norm, layer)])
        norms = [(ffn1_norm, layer + 1)] if layer + 1 < depth else []
        if layer == n_a - 1:
            norms = norms + [(kv_norm, 0)]
        h, acts = _ffn(h, act, ffn2_w_gate, ffn2_w_up, ffn2_w_down, layer, norms)
        act = acts[0] if acts else None
        if layer == n_a - 1:
            shared = _nsa_shared_kv(acts[-1], B, S, tabs_b, kv_w, cmp_pos_k, cmp_pos_v, cmp_k_w1, cmp_k_b1,
                                    cmp_k_w2, cmp_v_w1, cmp_v_b1, cmp_v_w2, g_k_cmp, g_k_slc, g_k_win)
    return h.reshape(B, S, D)
```

```python
import functools
import math

import jax
import jax.numpy as jnp
from jax import lax
from jax.experimental import pallas as pl
from jax.experimental.pallas import tpu as pltpu

F32 = jnp.float32
BF16 = jnp.bfloat16

ROPE_THETA = 10000.0
EPS = 1e-6
NEG = -1e30
FORCE = 1e6
MLA_NOPE = 128
MLA_ROPE = 64
MLA_V = 128
MLA_QK = MLA_NOPE + MLA_ROPE
MLA_HEAD_PAD = 256
NSA_DH = 128
N_BRANCH = 3
CMP_BLOCK = 32
CMP_STRIDE = 16
SLC_BLOCK = 64
SLC_TOPK = 16
WINDOW = 512
LOG2E = math.log2(math.e)

LANES = 128
MXU_DIM = 256
V7X_VMEM_BYTES = 64 * 1024 * 1024
VMEM_CAP = V7X_VMEM_BYTES - 8 * 1024 * 1024

ATTN_TQ = 256
MLA_SCORES_AHEAD = 2


def _vmem_limit(nbytes):
    return int(min(VMEM_CAP, max(32 * 1024 * 1024, nbytes + 8 * 1024 * 1024)))


def _params(sem, nbytes):
    return pltpu.CompilerParams(dimension_semantics=sem, vmem_limit_bytes=_vmem_limit(nbytes))


def _rope_table_kernel(pos_ref, inv_ref, mc_ref, ms_ref, c_ref, s_ref):
    ang = pos_ref[...] * inv_ref[...]
    c_ref[...] = jnp.cos(ang) * mc_ref[...]
    s_ref[...] = jnp.sin(ang) * ms_ref[...]


def _rope_tables(pos, inv, mask_c, mask_s):
    T = pos.shape[0]
    tm = min(T, 1024)
    row = pl.BlockSpec((1, LANES), lambda i: (0, 0))
    out = pl.BlockSpec((tm, LANES), lambda i: (i, 0))
    return pl.pallas_call(
        _rope_table_kernel,
        name="rope_tables",
        grid=(T // tm,),
        in_specs=[pl.BlockSpec((tm, 1), lambda i: (i, 0)), row, row, row],
        out_specs=[out, out],
        out_shape=[jax.ShapeDtypeStruct((T, LANES), F32)] * 2,
        compiler_params=_params(("arbitrary",), 0),
    )(pos, inv, mask_c, mask_s)


def _rope_nsa(x, c, s):
    return x * c + pltpu.roll(x, 64, 1) * s


def _rope_mla(x, c, s):
    return x * c + pltpu.roll(x, 32, 1) * s


def _mxu_row_sum(sq, sel):
    return jnp.dot(sq.astype(BF16), sel, preferred_element_type=F32)


def _sel_matrix(shape, pred):
    r = lax.broadcasted_iota(jnp.int32, shape, 0)
    c = lax.broadcasted_iota(jnp.int32, shape, 1)
    return pred(r, c).astype(BF16)


def _rmsnorm_kernel(x_ref, g_ref, o_ref):
    x = x_ref[...]
    r = lax.rsqrt(jnp.mean(x * x, axis=-1, keepdims=True) + EPS)
    o_ref[...] = (x * r * g_ref[...]).astype(o_ref.dtype)


def _rmsnorm(x, g, *, layer=None):
    T, width = x.shape
    tm = min(T, 512)
    if layer is None:
        g = g.reshape(1, width)
        g_spec = pl.BlockSpec((1, width), lambda i: (0, 0))
    else:
        g = g.reshape(g.shape[0], 1, width)
        g_spec = pl.BlockSpec((None, 1, width), lambda i: (layer, 0, 0))
    return pl.pallas_call(
        _rmsnorm_kernel,
        name="rmsnorm",
        grid=(T // tm,),
        in_specs=[pl.BlockSpec((tm, width), lambda i: (i, 0)), g_spec],
        out_specs=pl.BlockSpec((tm, width), lambda i: (i, 0)),
        out_shape=jax.ShapeDtypeStruct((T, width), BF16),
        compiler_params=_params(("arbitrary",), 6 * tm * width * 4),
    )(x, g)


def _scale_rows(acc, rs):
    return jnp.concatenate([acc[:, k * LANES:(k + 1) * LANES] * rs for k in range(acc.shape[1] // LANES)], axis=1)


def _mm_kernel(n_w, has_rs, w_transposed, n_extra, n_out, epilogue, x_ref, *refs):
    refs = list(refs)
    rs_ref = refs.pop(0) if has_rs else None
    w_refs = refs[:n_w]
    extra = refs[n_w:n_w + n_extra]
    outs = refs[n_w + n_extra:n_w + n_extra + n_out]
    w_bf = refs[n_w + n_extra + n_out:2 * n_w + n_extra + n_out]
    scr = refs[2 * n_w + n_extra + n_out:]

    @pl.when(pl.program_id(1) == 0)
    def _():
        for w, s in zip(w_refs, w_bf):
            s[...] = w[...].astype(BF16)

    if scr:
        @pl.when((pl.program_id(0) == 0) & (pl.program_id(1) == 0))
        def _():
            for s in scr:
                s[...] = jnp.zeros(s.shape, s.dtype)

    x = x_ref[...]
    dims = (((1,), (1 if w_transposed else 0,)), ((), ()))
    accs = [lax.dot_general(x, s[...], dims, preferred_element_type=F32) for s in w_bf]
    if has_rs:
        rs = rs_ref[...]
        accs = [_scale_rows(a, rs) for a in accs]
    epilogue(accs, extra, outs, scr)


def _mm(act, weights, epilogue, extras, outs, *, n_cols, tn, tm=1024, scratch=(), w_transposed=False):
    x, row_scale = act
    M, K = x.shape
    tm = min(tm, M)
    assert M % tm == 0 and n_cols % tn == 0
    in_specs = [pl.BlockSpec((tm, K), lambda j, i: (i, 0))]
    args = [x]
    nbytes = 2 * tm * K * 2 + len(weights) * (2 * K * tn * 4 + K * tn * 2 + 2 * tm * tn * 4)
    if row_scale is not None:
        in_specs.append(pl.BlockSpec((tm, LANES), lambda j, i: (i, 0)))
        args.append(row_scale)
        nbytes += 2 * tm * LANES * 4
    for arr, layer, off in weights:
        if w_transposed:
            assert layer is not None
            in_specs.append(pl.BlockSpec((None, tn, K), lambda j, i, off=off, layer=layer: (layer, j + off, 0)))
        elif layer is None:
            in_specs.append(pl.BlockSpec((K, tn), lambda j, i, off=off: (0, j + off)))
        else:
            in_specs.append(pl.BlockSpec((None, K, tn), lambda j, i, off=off, layer=layer: (layer, 0, j + off)))
        args.append(arr)
    for arr, bs, im in extras:
        in_specs.append(pl.BlockSpec(bs, im))
        args.append(arr)
        nbytes += 2 * 4 * functools.reduce(lambda a, b: a * (b or 1), bs, 1)
    out_specs, out_shapes = [], []
    for sds, bs, im in outs:
        out_specs.append(pl.BlockSpec(bs, im))
        out_shapes.append(sds)
        nbytes += 2 * sds.dtype.itemsize * functools.reduce(lambda a, b: a * (b or 1), bs, 1)
    for shape, dtype in scratch:
        nbytes += jnp.dtype(dtype).itemsize * functools.reduce(lambda a, b: a * b, shape, 1)
    kern = functools.partial(_mm_kernel, len(weights), row_scale is not None, w_transposed, len(extras), len(outs),
                             epilogue)
    res = pl.pallas_call(
        kern,
        name="mm_" + getattr(epilogue, "func", epilogue).__name__[len("_epi_"):],
        grid=(n_cols // tn, M // tm),
        in_specs=in_specs,
        out_specs=out_specs,
        out_shape=out_shapes,
        scratch_shapes=[pltpu.VMEM((tn, K) if w_transposed else (K, tn), BF16) for _ in weights]
                       + [pltpu.VMEM(s, d) for s, d in scratch],
        compiler_params=_params(("arbitrary", "arbitrary"), nbytes),
    )(*args)
    return res


def _epi_plain(accs, extra, outs, scr):
    outs[0][...] = accs[0].astype(outs[0].dtype)


def _epi_swiglu(accs, extra, outs, scr):
    g, u = accs
    outs[0][...] = (g * (1.0 / (1.0 + jnp.exp(-g))) * u).astype(outs[0].dtype)


def _epi_residual(alpha, accs, extra, outs, scr):
    outs[0][...] = extra[0][...] + alpha * accs[0]


def _epi_residual_norm(alpha, width, accs, extra, outs, scr):
    res_ref = extra[0]
    h_o, rs_o = outs[0], outs[-1]
    ss = scr[0]
    i = pl.program_id(1)
    h = res_ref[...] + alpha * accs[0]
    h_o[...] = h
    for g_ref, hg_o in zip(extra[1:], outs[1:-1]):
        hg_o[...] = (h * g_ref[...]).astype(hg_o.dtype)
    sq = h * h
    part = sq[:, :LANES]
    for k in range(1, sq.shape[1] // LANES):
        part = part + sq[:, k * LANES:(k + 1) * LANES]
    tot = ss[i] + part
    ss[i] = tot

    @pl.when(pl.program_id(0) == pl.num_programs(0) - 1)
    def _():
        rs_o[...] = jnp.broadcast_to(lax.rsqrt(jnp.sum(tot, -1, keepdims=True) / width + EPS), rs_o.shape)


def _mm_plain(act, w, layer, col_off, n_cols, tn, dtype, w_transposed=False):
    M = act[0].shape[0]
    tm = min(1024, M)
    out = (jax.ShapeDtypeStruct((M, n_cols), dtype), (tm, tn), lambda j, i: (i, j))
    return _mm(act, [(w, layer, col_off)], _epi_plain, [], [out], n_cols=n_cols, tn=tn, tm=tm,
               w_transposed=w_transposed)[0]


def _mm_residual(x, w, layer, res, alpha, next_norms=()):
    M, N = res.shape
    K = x.shape[1]
    tn = 512 if N % 512 == 0 else 256
    tm = min(1024 if K <= 4096 else 512, M)
    blk = ((tm, tn), lambda j, i: (i, j))
    h_out = (jax.ShapeDtypeStruct((M, N), F32),) + blk
    if not next_norms:
        return _mm((x, None), [(w, layer, 0)], functools.partial(_epi_residual, alpha), [(res,) + blk],
                   [h_out], n_cols=N, tn=tn, tm=tm)[0], []
    g_specs = [(gains.reshape(-1, 1, N), (None, 1, tn), lambda j, i, gl=gl: (gl, 0, j)) for gains, gl in next_norms]
    hg_outs = [(jax.ShapeDtypeStruct((M, N), BF16),) + blk for _ in next_norms]
    last_j = N // tn - 1
    rs_out = (jax.ShapeDtypeStruct((M, LANES), F32), (tm, LANES), lambda j, i: (jnp.where(j == last_j, i, 0), 0))
    res_all = _mm((x, None), [(w, layer, 0)], functools.partial(_epi_residual_norm, alpha, N),
                  [(res,) + blk] + g_specs, [h_out] + hg_outs + [rs_out],
                  n_cols=N, tn=tn, tm=tm, scratch=[((M // tm, tm, LANES), F32)])
    return res_all[0], [(hg, res_all[-1]) for hg in res_all[1:-1]]


def _ffn(h, act, wg, wu, wd, layer, next_norms):
    M, D = h.shape
    F = wg.shape[-1]
    tn = 512 if F % 512 == 0 else 256
    tm = min(512, M)
    hid = _mm(act, [(wg, layer, 0), (wu, layer, 0)], _epi_swiglu, [],
              [(jax.ShapeDtypeStruct((M, F), BF16), (tm, tn), lambda j, i: (i, j))],
              n_cols=F, tn=tn, tm=tm)[0]
    return _mm_residual(hid, wd, layer, h, 0.5, next_norms)


def _epi_mla_q(n_heads, out_scale, accs, extra, outs, scr):
    g = extra[0][...] * out_scale
    c, s = extra[1][...], extra[2][...]
    o = outs[0]
    sel = _sel_matrix((MLA_HEAD_PAD, MLA_HEAD_PAD), lambda r, col: r < MLA_QK)
    acc = accs[0]
    for hh in range(n_heads):
        b0 = hh * MLA_HEAD_PAD
        a = acc[:, b0:b0 + MLA_HEAD_PAD]
        r = lax.rsqrt(_mxu_row_sum(a * a, sel) / MLA_QK + EPS)
        an = a * r * g
        o[:, b0:b0 + LANES] = an[:, :LANES].astype(o.dtype)
        o[:, b0 + LANES:b0 + 2 * LANES] = _rope_mla(an[:, LANES:], c, s).astype(o.dtype)


def _epi_mla_kv(n_heads, accs, extra, outs, scr):
    g = extra[0][...]
    c, s = extra[1][...], extra[2][...]
    kr = extra[3][...]
    kr2 = kr * kr
    k_o, v_o = outs
    sel = _sel_matrix((MLA_HEAD_PAD, LANES), lambda r, col: r < MLA_QK)
    acc = accs[0]
    for hh in range(n_heads):
        b0 = hh * MLA_HEAD_PAD
        a = acc[:, b0:b0 + MLA_HEAD_PAD]
        kn = a[:, :LANES]
        ss = _mxu_row_sum(jnp.concatenate([kn * kn, kr2], axis=1), sel)
        r = lax.rsqrt(ss / MLA_QK + EPS)
        k_o[:, b0:b0 + LANES] = (kn * r * g[:, :LANES]).astype(k_o.dtype)
        k_o[:, b0 + LANES:b0 + 2 * LANES] = _rope_mla(kr * r * g[:, LANES:], c, s).astype(k_o.dtype)
        v_o[:, hh * MLA_V:(hh + 1) * MLA_V] = a[:, LANES:].astype(v_o.dtype)


def _pad_gain_mla(g):
    return jnp.concatenate([g, g[MLA_NOPE:]]).reshape(1, MLA_HEAD_PAD)


def _mla_attn_kernel(S, tq, nh, q_ref, k_ref, v_ref, o_ref):
    row = lax.broadcasted_iota(jnp.int32, (tq, tq), 0)
    col = lax.broadcasted_iota(jnp.int32, (tq, tq), 1)
    tri = jnp.where(col <= row, 0.0, NEG)
    nt = (((1,), (1,)), ((), ()))

    def scores(i, hh):
        r0 = i * tq
        ks = slice(hh * MLA_HEAD_PAD, (hh + 1) * MLA_HEAD_PAD)
        q = q_ref[r0:r0 + tq, ks]
        s_d = lax.dot_general(q, k_ref[r0:r0 + tq, ks], nt, preferred_element_type=F32) + tri
        s_o = lax.dot_general(q, k_ref[0:r0, ks], nt, preferred_element_type=F32) if i > 0 else None
        return s_d, s_o

    def finish(i, hh, s_d, s_o):
        r0 = i * tq
        vs = slice(hh * MLA_V, (hh + 1) * MLA_V)
        m = jnp.max(s_d, -1, keepdims=True)
        if i > 0:
            m = jnp.maximum(m, jnp.max(s_o, -1, keepdims=True))
        e_d = jnp.exp2(s_d - m)
        l = jnp.sum(e_d, -1, keepdims=True)
        o = jnp.dot(e_d.astype(BF16), v_ref[r0:r0 + tq, vs], preferred_element_type=F32)
        if i > 0:
            e_o = jnp.exp2(s_o - m)
            l = l + jnp.sum(e_o, -1, keepdims=True)
            o = o + jnp.dot(e_o.astype(BF16), v_ref[0:r0, vs], preferred_element_type=F32)
        o_ref[r0:r0 + tq, vs] = (o / l).astype(o_ref.dtype)

    tiles = [(i, hh) for i in reversed(range(S // tq)) for hh in range(nh)]
    pending = [scores(*t) for t in tiles[:MLA_SCORES_AHEAD]]
    for n, t in enumerate(tiles):
        if n + MLA_SCORES_AHEAD < len(tiles):
            pending.append(scores(*tiles[n + MLA_SCORES_AHEAD]))
        finish(*t, *pending.pop(0))


def _mla_attention(q, k, v, B, S, H):
    T = B * S
    tq = min(ATTN_TQ, S)
    nh = 2 if H % 2 == 0 else 1
    kern = functools.partial(_mla_attn_kernel, S, tq, nh)
    nbytes = 2 * nh * S * (2 * MLA_HEAD_PAD + 2 * MLA_V) * 2 + 12 * tq * S * 4
    return pl.pallas_call(
        kern,
        name="mla_attn",
        grid=(B, H // nh),
        in_specs=[pl.BlockSpec((S, nh * MLA_HEAD_PAD), lambda b, h: (b, h)),
                  pl.BlockSpec((S, nh * MLA_HEAD_PAD), lambda b, h: (b, h)),
                  pl.BlockSpec((S, nh * MLA_V), lambda b, h: (b, h))],
        out_specs=pl.BlockSpec((S, nh * MLA_V), lambda b, h: (b, h)),
        out_shape=jax.ShapeDtypeStruct((T, H * MLA_V), BF16),
        compiler_params=_params(("arbitrary", "arbitrary"), nbytes),
    )(q, k, v)


def _mla_layer(h, act, a, B, S, tabs, mla_w_in, mla_g_cq, mla_g_ckv, mla_w_uq, mla_w_ukv, mla_g_q, mla_g_k, mla_w_o,
               next_norms):
    T, D = h.shape
    q_lora = mla_g_cq.shape[-1]
    kv_lora = mla_g_ckv.shape[-1]
    H = mla_w_ukv.shape[-1] // (MLA_NOPE + MLA_V)
    c_tab, s_tab = tabs
    tm = min(2048, T)
    lat = q_lora + kv_lora
    assert q_lora % kv_lora == 0
    w_in_t = jnp.swapaxes(mla_w_in, 1, 2)
    cq = _mm_plain(act, w_in_t, a, 0, q_lora, min(512, q_lora), F32, w_transposed=True)
    ckv = _mm_plain(act, w_in_t, a, q_lora // kv_lora, kv_lora, kv_lora, F32, w_transposed=True)
    w_kr = lax.slice(mla_w_in, (a, 0, lat), (a + 1, D, lat + MLA_ROPE))[0]
    kr = _mm_plain(act, jnp.concatenate([w_kr, w_kr], axis=1), None, 0, LANES, LANES, F32)
    cq = _rmsnorm(cq, mla_g_cq[a])
    ckv = _rmsnorm(ckv, mla_g_ckv[a])
    w_uq = mla_w_uq[a].reshape(q_lora, H, MLA_QK)
    w_uq = jnp.concatenate([w_uq, w_uq[:, :, MLA_NOPE:]], axis=2).reshape(q_lora, H * MLA_HEAD_PAD)
    hp = 2 if H % 2 == 0 else 1
    tn = hp * MLA_HEAD_PAD
    tab_specs = [(t, (tm, LANES), lambda j, i: (i, 0)) for t in (c_tab, s_tab)]
    gain = lambda g: (_pad_gain_mla(g), (1, MLA_HEAD_PAD), lambda j, i: (0, 0))
    q = _mm((cq, None), [(w_uq, None, 0)], functools.partial(_epi_mla_q, hp, MLA_QK ** -0.5 * LOG2E),
            [gain(mla_g_q[a])] + tab_specs,
            [(jax.ShapeDtypeStruct((T, H * MLA_HEAD_PAD), BF16), (tm, tn), lambda j, i: (i, j))],
            n_cols=H * MLA_HEAD_PAD, tn=tn, tm=tm)[0]
    kr_spec = (kr, (tm, LANES), lambda j, i: (i, 0))
    k, v = _mm((ckv, None), [(mla_w_ukv, a, 0)], functools.partial(_epi_mla_kv, hp),
               [gain(mla_g_k[a])] + tab_specs + [kr_spec],
               [(jax.ShapeDtypeStruct((T, H * MLA_HEAD_PAD), BF16), (tm, tn), lambda j, i: (i, j)),
                (jax.ShapeDtypeStruct((T, H * MLA_V), BF16), (tm, hp * MLA_V), lambda j, i: (i, j))],
               n_cols=H * MLA_HEAD_PAD, tn=tn, tm=tm)
    o = _mla_attention(q, k, v, B, S, H)
    return _mm_residual(o, mla_w_o, a, h, 1.0, next_norms)


def _nsa_head_norm(a, g2):
    sel = _sel_matrix((MXU_DIM, MXU_DIM), lambda r, col: r // NSA_DH == col // NSA_DH)
    r = lax.rsqrt(_mxu_row_sum(a * a, sel) / NSA_DH + EPS)
    return a * r * g2


def _epi_nsa_k(n_pairs, accs, extra, outs, scr):
    g = extra[0][...]
    g2 = jnp.concatenate([g, g], axis=1)
    c, s = extra[1][...], extra[2][...]
    o = outs[0]
    acc = accs[0]
    for pp in range(n_pairs):
        kn = _nsa_head_norm(acc[:, pp * MXU_DIM:(pp + 1) * MXU_DIM], g2)
        for hh in range(2):
            col = (2 * pp + hh) * NSA_DH
            o[:, col:col + NSA_DH] = _rope_nsa(kn[:, hh * NSA_DH:(hh + 1) * NSA_DH], c, s).astype(o.dtype)


def _compress_kernel(n_cmp, k_ref, v_ref, pk_ref, pv_ref, kw1_ref, kb1_ref, kw2_ref,
                     vw1_ref, vb1_ref, vw2_ref, gk_ref, ko_ref, vo_ref):
    half = CMP_STRIDE * NSA_DH
    n_rows = ko_ref.shape[0]
    row = lax.broadcasted_iota(jnp.int32, (n_rows, 1), 0)

    def phi(t_ref, pos_ref, w1_ref, b1_ref, w2_ref):
        slabs = [t_ref[pl.ds(l, n_rows, stride=CMP_STRIDE), :] for l in range(CMP_STRIDE)]
        lo = jnp.concatenate([slabs[l] + pos_ref[:, l * NSA_DH:(l + 1) * NSA_DH]
                              for l in range(CMP_STRIDE)], axis=1).astype(BF16)
        hi = jnp.concatenate([slabs[l] + pos_ref[:, half + l * NSA_DH:half + (l + 1) * NSA_DH]
                              for l in range(CMP_STRIDE)], axis=1).astype(BF16)
        p_lo = jnp.dot(lo, w1_ref[0:half, :].astype(BF16), preferred_element_type=F32)
        p_hi = jnp.dot(hi, w1_ref[half:2 * half, :].astype(BF16), preferred_element_type=F32)
        pre = p_lo + pltpu.roll(p_hi, n_rows - 1, 0) + b1_ref[...]
        hid = pre * (1.0 / (1.0 + jnp.exp(-pre)))
        out = jnp.dot(hid.astype(BF16), w2_ref[...].astype(BF16), preferred_element_type=F32)
        return jnp.where(row < n_cmp, out, 0.0)

    kc = phi(k_ref, pk_ref, kw1_ref, kb1_ref, kw2_ref)
    r = lax.rsqrt(jnp.mean(kc * kc, -1, keepdims=True) + EPS)
    ko_ref[...] = (kc * r * gk_ref[...]).astype(ko_ref.dtype)
    vo_ref[...] = phi(v_ref, pv_ref, vw1_ref, vb1_ref, vw2_ref).astype(vo_ref.dtype)


def _compress(raw, B, S, G, cmp_pos_k, cmp_pos_v, k_w1, k_b1, k_w2, v_w1, v_b1, v_w2, g_k_cmp):
    n_cmp = (S - CMP_BLOCK) // CMP_STRIDE + 1
    n_rows = S // CMP_STRIDE
    hid = k_w1.shape[-1]
    full = lambda shape: pl.BlockSpec(shape, lambda b, g: (0,) * len(shape))
    w_specs = [full((CMP_BLOCK * NSA_DH, hid)), full((1, hid)), full((hid, NSA_DH))]
    out_spec = pl.BlockSpec((None, None, n_rows, NSA_DH), lambda b, g: (b, g, 0, 0))
    out_sds = jax.ShapeDtypeStruct((B, G, n_rows, NSA_DH), BF16)
    nbytes = 4 * (2 * CMP_BLOCK * NSA_DH * hid * 4) + 4 * S * NSA_DH * 4 + 8 * n_rows * CMP_BLOCK * NSA_DH * 4
    return pl.pallas_call(
        functools.partial(_compress_kernel, n_cmp),
        name="nsa_compress",
        grid=(B, G),
        in_specs=[pl.BlockSpec((S, NSA_DH), lambda b, g: (b, g)),
                  pl.BlockSpec((S, NSA_DH), lambda b, g: (b, G + g)),
                  full((1, CMP_BLOCK * NSA_DH)), full((1, CMP_BLOCK * NSA_DH))]
                 + w_specs + w_specs + [full((1, NSA_DH))],
        out_specs=[out_spec, out_spec],
        out_shape=[out_sds, out_sds],
        compiler_params=_params(("arbitrary", "arbitrary"), nbytes),
    )(raw, raw, cmp_pos_k.reshape(1, -1), cmp_pos_v.reshape(1, -1),
      k_w1, k_b1.reshape(1, -1), k_w2, v_w1, v_b1.reshape(1, -1), v_w2, g_k_cmp.reshape(1, -1))


def _nsa_shared_kv(y, B, S, tabs, kv_w, cmp_pos_k, cmp_pos_v, cmp_k_w1, cmp_k_b1, cmp_k_w2,
                   cmp_v_w1, cmp_v_b1, cmp_v_w2, g_k_cmp, g_k_slc, g_k_win):
    T = y[0].shape[0]
    G = kv_w.shape[-1] // (2 * N_BRANCH * NSA_DH)
    assert G % 2 == 0
    part = G * NSA_DH
    tm = min(1024, T)
    c_tab, s_tab = tabs
    raw = _mm_plain(y, kv_w, None, 0, 2 * part, part, F32)
    k_cmp, v_cmp = _compress(raw, B, S, G, cmp_pos_k, cmp_pos_v, cmp_k_w1, cmp_k_b1, cmp_k_w2,
                             cmp_v_w1, cmp_v_b1, cmp_v_w2, g_k_cmp)

    def k_branch(part_idx, gain):
        extras = [(gain.reshape(1, NSA_DH), (1, NSA_DH), lambda j, i: (0, 0)),
                  (c_tab, (tm, LANES), lambda j, i: (i, 0)), (s_tab, (tm, LANES), lambda j, i: (i, 0))]
        return _mm(y, [(kv_w, None, part_idx)], functools.partial(_epi_nsa_k, G // 2), extras,
                   [(jax.ShapeDtypeStruct((T, part), BF16), (tm, part), lambda j, i: (i, j))],
                   n_cols=part, tn=part, tm=tm)[0]

    k_slc = k_branch(2, g_k_slc)
    v_slc = _mm_plain(y, kv_w, None, 3, part, part, BF16)
    k_win = k_branch(4, g_k_win)
    v_win = _mm_plain(y, kv_w, None, 5, part, part, BF16)
    return k_cmp, v_cmp, k_slc, v_slc, k_win, v_win


def _epi_nsa_q(n_pairs, out_scale, accs, extra, outs, scr):
    g = extra[0][...] * out_scale
    g2 = jnp.concatenate([g, g], axis=1)
    c, s = extra[1][...], extra[2][...]
    q_o, qr_o = outs
    acc = accs[0]
    for pp in range(n_pairs):
        qn = _nsa_head_norm(acc[:, pp * MXU_DIM:(pp + 1) * MXU_DIM], g2)
        for hh in range(2):
            x = qn[:, hh * NSA_DH:(hh + 1) * NSA_DH]
            q_o[2 * pp + hh] = x.astype(q_o.dtype)
            qr_o[2 * pp + hh] = _rope_nsa(x, c, s).astype(qr_o.dtype)


def _nsa_attn_kernel(S, tq, hpg, n_cmp,
                     q_ref, qr_ref, gl_ref, bg_ref, kc_ref, vc_ref, ks_ref, vs_ref, kw_ref, vw_ref,
                     o_ref, ocmp_scr, obuf_scr):
    g_idx = pl.program_id(1)
    n_cr = kc_ref.shape[0]
    n_slc = S // SLC_BLOCK
    n_sel = min(SLC_TOPK, n_slc)
    k_cmp = kc_ref[...]
    v_cmp = vc_ref[...]
    lane = lax.broadcasted_iota(jnp.int32, (tq, LANES), 1)
    nt = (((1,), (1,)), ((), ()))

    jj = lax.broadcasted_iota(jnp.int32, (n_slc, n_cr), 0)
    nn = lax.broadcasted_iota(jnp.int32, (n_slc, n_cr), 1)
    cs = nn * CMP_STRIDE
    ss = jj * SLC_BLOCK
    ov = jnp.maximum(jnp.minimum(cs + CMP_BLOCK, ss + SLC_BLOCK) - jnp.maximum(cs, ss), 0)
    agg_t = jnp.where(nn < n_cmp, ov.astype(F32) / CMP_STRIDE, 0.0)

    for i in range(S // tq):
        r0 = i * tq
        nk = r0 + tq
        rows = slice(r0, r0 + tq)

        spos_c = r0 + lax.broadcasted_iota(jnp.int32, (tq, n_cr), 0)
        ncol = lax.broadcasted_iota(jnp.int32, (tq, n_cr), 1)
        cmask = (ncol * CMP_STRIDE + (CMP_BLOCK - 1) <= spos_c) & (ncol < n_cmp)
        cbias = jnp.where(cmask, 0.0, NEG)
        cmask_f = cmask.astype(F32)
        q_all = q_ref[:, rows, :].reshape(hpg * tq, NSA_DH)
        sc = lax.dot_general(q_all, k_cmp, nt, preferred_element_type=F32).reshape(hpg, tq, n_cr) + cbias
        m = jnp.max(sc, -1, keepdims=True)
        e = jnp.exp2(sc - m) * cmask_f
        l = jnp.sum(e, -1, keepdims=True)
        p = e / jnp.where(l > 0.0, l, 1.0)
        ocmp_scr[...] = jnp.dot(p.reshape(hpg * tq, n_cr).astype(BF16), v_cmp,
                                preferred_element_type=F32).reshape(hpg, tq, NSA_DH)
        psum = jnp.sum(p, axis=0)

        imp = lax.dot_general(agg_t, psum, nt, precision=lax.Precision.HIGHEST,
                              preferred_element_type=F32)
        jb = lax.broadcasted_iota(jnp.int32, (n_slc, tq), 0)
        sp = r0 + lax.broadcasted_iota(jnp.int32, (n_slc, tq), 1)
        cur = sp // SLC_BLOCK
        valid = jb * SLC_BLOCK <= sp
        forced = (jb == 0) | (jb == cur) | (jb == cur - 1)
        imp = jnp.where(forced, FORCE, jnp.where(valid, imp, -1.0))
        rank = jnp.zeros((n_slc, tq), jnp.int32)
        for jp in range(n_slc):
            other = imp[jp:jp + 1, :]
            ahead = (other > imp) | ((other == imp) & (jp < jb))
            rank = rank + ahead.astype(jnp.int32)
        sel_t = (rank < n_sel).astype(BF16)

        eb = lax.broadcasted_iota(jnp.int32, (n_slc, nk), 0)
        ek = lax.broadcasted_iota(jnp.int32, (n_slc, nk), 1)
        expand = (ek // SLC_BLOCK == eb).astype(BF16)
        sel_keys = lax.dot_general(sel_t, expand, (((0,), (0,)), ((), ())), preferred_element_type=F32)
        qrow = r0 + lax.broadcasted_iota(jnp.int32, (tq, nk), 0)
        kcol = lax.broadcasted_iota(jnp.int32, (tq, nk), 1)
        sbias = jnp.where((sel_keys > 0.5) & (kcol <= qrow), 0.0, NEG)

        w0 = (max(0, r0 - WINDOW) // tq) * tq
        nw = nk - w0
        wrow = r0 + lax.broadcasted_iota(jnp.int32, (tq, nw), 0)
        wcol = w0 + lax.broadcasted_iota(jnp.int32, (tq, nw), 1)
        wdiff = wrow - wcol
        wbias = jnp.where((wdiff >= 0) & (wdiff < WINDOW), 0.0, NEG)

        z = gl_ref[rows, :] + bg_ref[...]
        gates = 1.0 / (1.0 + jnp.exp(-z))

        k_s = ks_ref[0:nk, :]
        v_s = vs_ref[0:nk, :]
        k_w = kw_ref[w0:nk, :]
        v_w = vw_ref[w0:nk, :]

        def main_body(pp, carry):
            hs = (2 * pp, 2 * pp + 1)
            qs = [qr_ref[hh, rows, :] for hh in hs]
            s_s = [lax.dot_general(q, k_s, nt, preferred_element_type=F32) + sbias for q in qs]
            s_w = [lax.dot_general(q, k_w, nt, preferred_element_type=F32) + wbias for q in qs]
            m_s = [jnp.max(s, -1, keepdims=True) for s in s_s]
            m_w = [jnp.max(s, -1, keepdims=True) for s in s_w]
            e_s = [jnp.exp2(s - m) for s, m in zip(s_s, m_s)]
            e_w = [jnp.exp2(s - m) for s, m in zip(s_w, m_w)]
            l_s = [jnp.sum(e, -1, keepdims=True) for e in e_s]
            l_w = [jnp.sum(e, -1, keepdims=True) for e in e_w]
            o_s = [jnp.dot(e.astype(BF16), v_s, preferred_element_type=F32) / l for e, l in zip(e_s, l_s)]
            o_w = [jnp.dot(e.astype(BF16), v_w, preferred_element_type=F32) / l for e, l in zip(e_w, l_w)]
            for t, hh in enumerate(hs):
                c0 = (g_idx * hpg + hh) * N_BRANCH
                gate = lambda br: jnp.sum(jnp.where(lane == c0 + br, gates, 0.0), -1, keepdims=True)
                o = gate(0) * ocmp_scr[hh] + gate(1) * o_s[t] + gate(2) * o_w[t]
                obuf_scr[hh] = o.astype(obuf_scr.dtype)
            return carry

        lax.fori_loop(0, hpg // 2, main_body, 0)
        for hh in range(hpg):
            o_ref[rows, hh * NSA_DH:(hh + 1) * NSA_DH] = obuf_scr[hh]


def _nsa_attention(q, qr, gate_logits, b_gate, shared, B, S, G, hpg):
    k_cmp, v_cmp, k_slc, v_slc, k_win, v_win = shared
    T = B * S
    H = G * hpg
    tq = min(ATTN_TQ, S)
    n_cmp = (S - CMP_BLOCK) // CMP_STRIDE + 1
    n_cr = k_cmp.shape[2]
    kern = functools.partial(_nsa_attn_kernel, S, tq, hpg, n_cmp)
    head_spec = pl.BlockSpec((hpg, S, NSA_DH), lambda b, g: (g, b, 0))
    cmp_spec = pl.BlockSpec((None, None, n_cr, NSA_DH), lambda b, g: (b, g, 0, 0))
    kv_spec = pl.BlockSpec((S, NSA_DH), lambda b, g: (b, g))
    nbytes = (2 * 2 * hpg * S * NSA_DH * 2 + 2 * S * LANES * 4 + 8 * S * NSA_DH * 2
              + 2 * S * hpg * NSA_DH * 2 + 10 * tq * S * 4)
    return pl.pallas_call(
        kern,
        name="nsa_attn",
        grid=(B, G),
        in_specs=[head_spec, head_spec,
                  pl.BlockSpec((S, LANES), lambda b, g: (b, 0)),
                  pl.BlockSpec((1, LANES), lambda b, g: (0, 0)),
                  cmp_spec, cmp_spec, kv_spec, kv_spec, kv_spec, kv_spec],
        out_specs=pl.BlockSpec((S, hpg * NSA_DH), lambda b, g: (b, g)),
        out_shape=jax.ShapeDtypeStruct((T, H * NSA_DH), BF16),
        scratch_shapes=[pltpu.VMEM((hpg, tq, NSA_DH), F32), pltpu.VMEM((hpg, tq, NSA_DH), BF16)],
        compiler_params=_params(("arbitrary", "arbitrary"), nbytes),
    )(q, qr, gate_logits, b_gate, k_cmp, v_cmp, k_slc, v_slc, k_win, v_win)


def _nsa_layer(h, u, b, B, S, G, tabs, shared, nsa_w_in, nsa_b_gate, nsa_g_q, nsa_w_o, next_norms):
    T, D = h.shape
    n_gate = nsa_b_gate.shape[-1]
    H = n_gate // N_BRANCH
    hpg = H // G
    q_width = H * NSA_DH
    assert n_gate <= LANES and hpg % 2 == 0
    c_tab, s_tab = tabs
    tm = min(1024, T)
    hp = min(4, H)
    tn = hp * NSA_DH
    extras = [(nsa_g_q[b].reshape(1, NSA_DH), (1, NSA_DH), lambda j, i: (0, 0)),
              (c_tab, (tm, LANES), lambda j, i: (i, 0)), (s_tab, (tm, LANES), lambda j, i: (i, 0))]
    head_out = (jax.ShapeDtypeStruct((H, T, NSA_DH), BF16), (hp, tm, NSA_DH), lambda j, i: (j, i, 0))
    q, qr = _mm(u, [(jnp.swapaxes(nsa_w_in, 1, 2), b, 0)],
                functools.partial(_epi_nsa_q, hp // 2, NSA_DH ** -0.5 * LOG2E),
                extras, [head_out, head_out], n_cols=q_width, tn=tn, tm=tm, w_transposed=True)
    w_gate = lax.slice(nsa_w_in, (b, 0, q_width), (b + 1, D, q_width + n_gate))[0]
    w_gate = jnp.pad(w_gate, ((0, 0), (0, LANES - n_gate)))
    gate_logits = _mm_plain(u, w_gate, None, 0, LANES, LANES, F32)
    b_gate = jnp.pad(nsa_b_gate[b], (0, LANES - n_gate)).reshape(1, LANES)
    o = _nsa_attention(q, qr, gate_logits, b_gate, shared, B, S, G, hpg)
    return _mm_residual(o, nsa_w_o, b, h, 1.0, next_norms)


def _lane_table(values):
    return jnp.asarray(values, F32).reshape(1, LANES)


def kernel(x, positions, ffn1_norm, ffn1_w_gate, ffn1_w_up, ffn1_w_down, mix_norm, ffn2_norm, ffn2_w_gate, ffn2_w_up, ffn2_w_down, mla_w_in, mla_g_cq, mla_g_ckv, mla_w_uq, mla_w_ukv, mla_g_q, mla_g_k, mla_w_o, kv_norm, kv_w, cmp_pos_k, cmp_pos_v, cmp_k_w1, cmp_k_b1, cmp_k_w2, cmp_v_w1, cmp_v_b1, cmp_v_w2, g_k_cmp, g_k_slc, g_k_win, nsa_w_in, nsa_b_gate, nsa_g_q, nsa_w_o):
    B, S, D = x.shape
    T = B * S
    depth = ffn1_norm.shape[0]
    n_a = mla_w_in.shape[0]
    G = kv_w.shape[-1] // (2 * N_BRANCH * NSA_DH)

    pos = positions.reshape(T, 1).astype(F32)
    half_a = MLA_ROPE // 2
    inv_a = jnp.power(ROPE_THETA, -jnp.arange(0, MLA_ROPE, 2, dtype=F32) / MLA_ROPE)
    zeros_a = jnp.zeros((LANES - MLA_ROPE,), F32)
    ones_h = jnp.ones((half_a,), F32)
    tabs_a = _rope_tables(pos, _lane_table(jnp.concatenate([inv_a, inv_a, zeros_a])),
                          _lane_table(jnp.concatenate([ones_h, ones_h, zeros_a])),
                          _lane_table(jnp.concatenate([-ones_h, ones_h, zeros_a])))
    inv_b = jnp.power(ROPE_THETA, -jnp.arange(0, NSA_DH, 2, dtype=F32) / NSA_DH)
    ones_b = jnp.ones((NSA_DH // 2,), F32)
    tabs_b = _rope_tables(pos, _lane_table(jnp.concatenate([inv_b, inv_b])),
                          _lane_table(jnp.concatenate([ones_b, ones_b])),
                          _lane_table(jnp.concatenate([-ones_b, ones_b])))

    h = x.reshape(T, D)
    shared = None
    act = (_rmsnorm(h, ffn1_norm, layer=0), None)
    for layer in range(depth):
        h, (act,) = _ffn(h, act, ffn1_w_gate, ffn1_w_up, ffn1_w_down, layer, [(mix_norm, layer)])
        if layer < n_a:
            h, (act,) = _mla_layer(h, act, layer, B, S, tabs_a, mla_w_in, mla_g_cq, mla_g_ckv, mla_w_uq, mla_w_ukv,
                                   mla_g_q, mla_g_k, mla_w_o, [(ffn2_norm, layer)])
        else:
            h, (act,) = _nsa_layer(h, act, layer - n_a, B, S, G, tabs_b, shared, nsa_w_in, nsa_b_gate, nsa_g_q,
                                   nsa_w_o, [(ffn2_norm, layer)])
        norms = [(ffn1_norm, layer + 1)] if layer + 1 < depth else []
        if layer == n_a - 1:
            norms = norms + [(kv_norm, 0)]
        h, acts = _ffn(h, act, ffn2_w_gate, ffn2_w_up, ffn2_w_down, layer, norms)
        act = acts[0] if acts else None
        if layer == n_a - 1:
            shared = _nsa_shared_kv(acts[-1], B, S, tabs_b, kv_w, cmp_pos_k, cmp_pos_v, cmp_k_w1, cmp_k_b1,
                                    cmp_k_w2, cmp_v_w1, cmp_v_b1, cmp_v_w2, g_k_cmp, g_k_slc, g_k_win)
    return h.reshape(B, S, D)
```

```python
import functools
import math

import jax
import jax.numpy as jnp
from jax import lax
from jax.experimental import pallas as pl
from jax.experimental.pallas import tpu as pltpu

F32 = jnp.float32
BF16 = jnp.bfloat16

ROPE_THETA = 10000.0
EPS = 1e-6
NEG = -1e30
FORCE = 1e6
MLA_NOPE = 128
MLA_ROPE = 64
MLA_V = 128
MLA_QK = MLA_NOPE + MLA_ROPE
MLA_HEAD_PAD = 256
NSA_DH = 128
N_BRANCH = 3
CMP_BLOCK = 32
CMP_STRIDE = 16
SLC_BLOCK = 64
SLC_TOPK = 16
WINDOW = 512
LOG2E = math.log2(math.e)

LANES = 128
MXU_DIM = 256
V7X_VMEM_BYTES = 64 * 1024 * 1024
VMEM_CAP = V7X_VMEM_BYTES - 8 * 1024 * 1024

ATTN_TQ = 256
MLA_SCORES_AHEAD = 2
NSA_HEADS_PER_TRIP = 4


def _vmem_limit(nbytes):
    return int(min(VMEM_CAP, max(32 * 1024 * 1024, nbytes + 8 * 1024 * 1024)))


def _params(sem, nbytes):
    return pltpu.CompilerParams(dimension_semantics=sem, vmem_limit_bytes=_vmem_limit(nbytes))


def _rope_table_kernel(pos_ref, inv_ref, mc_ref, ms_ref, c_ref, s_ref):
    ang = pos_ref[...] * inv_ref[...]
    c_ref[...] = jnp.cos(ang) * mc_ref[...]
    s_ref[...] = jnp.sin(ang) * ms_ref[...]


def _rope_tables(pos, inv, mask_c, mask_s):
    T = pos.shape[0]
    tm = min(T, 1024)
    row = pl.BlockSpec((1, LANES), lambda i: (0, 0))
    out = pl.BlockSpec((tm, LANES), lambda i: (i, 0))
    return pl.pallas_call(
        _rope_table_kernel,
        name="rope_tables",
        grid=(T // tm,),
        in_specs=[pl.BlockSpec((tm, 1), lambda i: (i, 0)), row, row, row],
        out_specs=[out, out],
        out_shape=[jax.ShapeDtypeStruct((T, LANES), F32)] * 2,
        compiler_params=_params(("arbitrary",), 0),
    )(pos, inv, mask_c, mask_s)


def _rope_nsa(x, c, s):
    return x * c + pltpu.roll(x, 64, 1) * s


def _rope_mla(x, c, s):
    return x * c + pltpu.roll(x, 32, 1) * s


def _mxu_row_sum(sq, sel):
    return jnp.dot(sq.astype(BF16), sel, preferred_element_type=F32)


def _sel_matrix(shape, pred):
    r = lax.broadcasted_iota(jnp.int32, shape, 0)
    c = lax.broadcasted_iota(jnp.int32, shape, 1)
    return pred(r, c).astype(BF16)


def _rmsnorm_kernel(x_ref, g_ref, o_ref):
    x = x_ref[...]
    r = lax.rsqrt(jnp.mean(x * x, axis=-1, keepdims=True) + EPS)
    o_ref[...] = (x * r * g_ref[...]).astype(o_ref.dtype)


def _rmsnorm(x, g, *, layer=None):
    T, width = x.shape
    tm = min(T, 512)
    if layer is None:
        g = g.reshape(1, width)
        g_spec = pl.BlockSpec((1, width), lambda i: (0, 0))
    else:
        g = g.reshape(g.shape[0], 1, width)
        g_spec = pl.BlockSpec((None, 1, width), lambda i: (layer, 0, 0))
    return pl.pallas_call(
        _rmsnorm_kernel,
        name="rmsnorm",
        grid=(T // tm,),
        in_specs=[pl.BlockSpec((tm, width), lambda i: (i, 0)), g_spec],
        out_specs=pl.BlockSpec((tm, width), lambda i: (i, 0)),
        out_shape=jax.ShapeDtypeStruct((T, width), BF16),
        compiler_params=_params(("arbitrary",), 6 * tm * width * 4),
    )(x, g)


def _scale_rows(acc, rs):
    return jnp.concatenate([acc[:, k * LANES:(k + 1) * LANES] * rs for k in range(acc.shape[1] // LANES)], axis=1)


def _mm_kernel(n_w, has_rs, w_transposed, n_extra, n_out, epilogue, x_ref, *refs):
    refs = list(refs)
    rs_ref = refs.pop(0) if has_rs else None
    w_refs = refs[:n_w]
    extra = refs[n_w:n_w + n_extra]
    outs = refs[n_w + n_extra:n_w + n_extra + n_out]
    w_bf = refs[n_w + n_extra + n_out:2 * n_w + n_extra + n_out]
    scr = refs[2 * n_w + n_extra + n_out:]

    @pl.when(pl.program_id(1) == 0)
    def _():
        for w, s in zip(w_refs, w_bf):
            s[...] = w[...].astype(BF16)

    if scr:
        @pl.when((pl.program_id(0) == 0) & (pl.program_id(1) == 0))
        def _():
            for s in scr:
                s[...] = jnp.zeros(s.shape, s.dtype)

    x = x_ref[...]
    dims = (((1,), (1 if w_transposed else 0,)), ((), ()))
    accs = [lax.dot_general(x, s[...], dims, preferred_element_type=F32) for s in w_bf]
    if has_rs:
        rs = rs_ref[...]
        accs = [_scale_rows(a, rs) for a in accs]
    epilogue(accs, extra, outs, scr)


def _mm(act, weights, epilogue, extras, outs, *, n_cols, tn, tm=1024, scratch=(), w_transposed=False):
    x, row_scale = act
    M, K = x.shape
    tm = min(tm, M)
    assert M % tm == 0 and n_cols % tn == 0
    in_specs = [pl.BlockSpec((tm, K), lambda j, i: (i, 0))]
    args = [x]
    nbytes = 2 * tm * K * 2 + len(weights) * (2 * K * tn * 4 + K * tn * 2 + 2 * tm * tn * 4)
    if row_scale is not None:
        in_specs.append(pl.BlockSpec((tm, LANES), lambda j, i: (i, 0)))
        args.append(row_scale)
        nbytes += 2 * tm * LANES * 4
    for arr, layer, off in weights:
        if w_transposed:
            assert layer is not None
            in_specs.append(pl.BlockSpec((None, tn, K), lambda j, i, off=off, layer=layer: (layer, j + off, 0)))
        elif layer is None:
            in_specs.append(pl.BlockSpec((K, tn), lambda j, i, off=off: (0, j + off)))
        else:
            in_specs.append(pl.BlockSpec((None, K, tn), lambda j, i, off=off, layer=layer: (layer, 0, j + off)))
        args.append(arr)
    for arr, bs, im in extras:
        in_specs.append(pl.BlockSpec(bs, im))
        args.append(arr)
        nbytes += 2 * 4 * functools.reduce(lambda a, b: a * (b or 1), bs, 1)
    out_specs, out_shapes = [], []
    for sds, bs, im in outs:
        out_specs.append(pl.BlockSpec(bs, im))
        out_shapes.append(sds)
        nbytes += 2 * sds.dtype.itemsize * functools.reduce(lambda a, b: a * (b or 1), bs, 1)
    for shape, dtype in scratch:
        nbytes += jnp.dtype(dtype).itemsize * functools.reduce(lambda a, b: a * b, shape, 1)
    kern = functools.partial(_mm_kernel, len(weights), row_scale is not None, w_transposed, len(extras), len(outs),
                             epilogue)
    res = pl.pallas_call(
        kern,
        name="mm_" + getattr(epilogue, "func", epilogue).__name__[len("_epi_"):],
        grid=(n_cols // tn, M // tm),
        in_specs=in_specs,
        out_specs=out_specs,
        out_shape=out_shapes,
        scratch_shapes=[pltpu.VMEM((tn, K) if w_transposed else (K, tn), BF16) for _ in weights]
                       + [pltpu.VMEM(s, d) for s, d in scratch],
        compiler_params=_params(("arbitrary", "arbitrary"), nbytes),
    )(*args)
    return res


def _epi_plain(accs, extra, outs, scr):
    outs[0][...] = accs[0].astype(outs[0].dtype)


def _epi_swiglu(accs, extra, outs, scr):
    g, u = accs
    outs[0][...] = (g * (1.0 / (1.0 + jnp.exp(-g))) * u).astype(outs[0].dtype)


def _epi_residual(alpha, accs, extra, outs, scr):
    outs[0][...] = extra[0][...] + alpha * accs[0]


def _epi_residual_norm(alpha, width, accs, extra, outs, scr):
    res_ref = extra[0]
    h_o, rs_o = outs[0], outs[-1]
    ss = scr[0]
    i = pl.program_id(1)
    h = res_ref[...] + alpha * accs[0]
    h_o[...] = h
    for g_ref, hg_o in zip(extra[1:], outs[1:-1]):
        hg_o[...] = (h * g_ref[...]).astype(hg_o.dtype)
    sq = h * h
    part = sq[:, :LANES]
    for k in range(1, sq.shape[1] // LANES):
        part = part + sq[:, k * LANES:(k + 1) * LANES]
    tot = ss[i] + part
    ss[i] = tot

    @pl.when(pl.program_id(0) == pl.num_programs(0) - 1)
    def _():
        rs_o[...] = jnp.broadcast_to(lax.rsqrt(jnp.sum(tot, -1, keepdims=True) / width + EPS), rs_o.shape)


def _mm_plain(act, w, layer, col_off, n_cols, tn, dtype, w_transposed=False):
    M = act[0].shape[0]
    tm = min(1024, M)
    out = (jax.ShapeDtypeStruct((M, n_cols), dtype), (tm, tn), lambda j, i: (i, j))
    return _mm(act, [(w, layer, col_off)], _epi_plain, [], [out], n_cols=n_cols, tn=tn, tm=tm,
               w_transposed=w_transposed)[0]


def _mm_residual(x, w, layer, res, alpha, next_norms=()):
    M, N = res.shape
    K = x.shape[1]
    tn = 512 if N % 512 == 0 else 256
    tm = min(1024 if K <= 4096 else 512, M)
    blk = ((tm, tn), lambda j, i: (i, j))
    h_out = (jax.ShapeDtypeStruct((M, N), F32),) + blk
    if not next_norms:
        return _mm((x, None), [(w, layer, 0)], functools.partial(_epi_residual, alpha), [(res,) + blk],
                   [h_out], n_cols=N, tn=tn, tm=tm)[0], []
    g_specs = [(gains.reshape(-1, 1, N), (None, 1, tn), lambda j, i, gl=gl: (gl, 0, j)) for gains, gl in next_norms]
    hg_outs = [(jax.ShapeDtypeStruct((M, N), BF16),) + blk for _ in next_norms]
    last_j = N // tn - 1
    rs_out = (jax.ShapeDtypeStruct((M, LANES), F32), (tm, LANES), lambda j, i: (jnp.where(j == last_j, i, 0), 0))
    res_all = _mm((x, None), [(w, layer, 0)], functools.partial(_epi_residual_norm, alpha, N),
                  [(res,) + blk] + g_specs, [h_out] + hg_outs + [rs_out],
                  n_cols=N, tn=tn, tm=tm, scratch=[((M // tm, tm, LANES), F32)])
    return res_all[0], [(hg, res_all[-1]) for hg in res_all[1:-1]]


def _ffn(h, act, wg, wu, wd, layer, next_norms):
    M, D = h.shape
    F = wg.shape[-1]
    tn = 512 if F % 512 == 0 else 256
    tm = min(512, M)
    hid = _mm(act, [(wg, layer, 0), (wu, layer, 0)], _epi_swiglu, [],
              [(jax.ShapeDtypeStruct((M, F), BF16), (tm, tn), lambda j, i: (i, j))],
              n_cols=F, tn=tn, tm=tm)[0]
    return _mm_residual(hid, wd, layer, h, 0.5, next_norms)


def _epi_mla_q(n_heads, out_scale, accs, extra, outs, scr):
    g = extra[0][...] * out_scale
    c, s = extra[1][...], extra[2][...]
    o = outs[0]
    sel = _sel_matrix((MLA_HEAD_PAD, LANES), lambda r, col: r < MLA_QK)
    acc = accs[0]
    for hh in range(n_heads):
        b0 = hh * MLA_HEAD_PAD
        a = acc[:, b0:b0 + MLA_HEAD_PAD]
        r = lax.rsqrt(_mxu_row_sum(a * a, sel) / MLA_QK + EPS)
        o[:, b0:b0 + LANES] = (a[:, :LANES] * r * g[:, :LANES]).astype(o.dtype)
        o[:, b0 + LANES:b0 + 2 * LANES] = _rope_mla(a[:, LANES:] * r * g[:, LANES:], c, s).astype(o.dtype)


def _epi_mla_kv(n_heads, accs, extra, outs, scr):
    g = extra[0][...]
    c, s = extra[1][...], extra[2][...]
    kr = extra[3][...]
    kr2 = kr * kr
    k_o, v_o = outs
    sel = _sel_matrix((MLA_HEAD_PAD, LANES), lambda r, col: r < MLA_QK)
    acc = accs[0]
    for hh in range(n_heads):
        b0 = hh * MLA_HEAD_PAD
        a = acc[:, b0:b0 + MLA_HEAD_PAD]
        kn = a[:, :LANES]
        ss = _mxu_row_sum(jnp.concatenate([kn * kn, kr2], axis=1), sel)
        r = lax.rsqrt(ss / MLA_QK + EPS)
        k_o[:, b0:b0 + LANES] = (kn * r * g[:, :LANES]).astype(k_o.dtype)
        k_o[:, b0 + LANES:b0 + 2 * LANES] = _rope_mla(kr * r * g[:, LANES:], c, s).astype(k_o.dtype)
        v_o[:, hh * MLA_V:(hh + 1) * MLA_V] = a[:, LANES:].astype(v_o.dtype)


def _pad_gain_mla(g):
    return jnp.concatenate([g, g[MLA_NOPE:]]).reshape(1, MLA_HEAD_PAD)


def _mla_attn_kernel(S, tq, nh, q_ref, k_ref, v_ref, o_ref):
    row = lax.broadcasted_iota(jnp.int32, (tq, tq), 0)
    col = lax.broadcasted_iota(jnp.int32, (tq, tq), 1)
    tri = jnp.where(col <= row, 0.0, NEG)
    nt = (((1,), (1,)), ((), ()))

    def scores(i, hh):
        r0 = i * tq
        ks = slice(hh * MLA_HEAD_PAD, (hh + 1) * MLA_HEAD_PAD)
        q = q_ref[r0:r0 + tq, ks]
        s_d = lax.dot_general(q, k_ref[r0:r0 + tq, ks], nt, preferred_element_type=F32) + tri
        s_o = lax.dot_general(q, k_ref[0:r0, ks], nt, preferred_element_type=F32) if i > 0 else None
        return s_d, s_o

    def finish(i, hh, s_d, s_o):
        r0 = i * tq
        vs = slice(hh * MLA_V, (hh + 1) * MLA_V)
        m = jnp.max(s_d, -1, keepdims=True)
        if i > 0:
            m = jnp.maximum(m, jnp.max(s_o, -1, keepdims=True))
        e_d = jnp.exp2(s_d - m)
        l = jnp.sum(e_d, -1, keepdims=True)
        o = jnp.dot(e_d.astype(BF16), v_ref[r0:r0 + tq, vs], preferred_element_type=F32)
        if i > 0:
            e_o = jnp.exp2(s_o - m)
            l = l + jnp.sum(e_o, -1, keepdims=True)
            o = o + jnp.dot(e_o.astype(BF16), v_ref[0:r0, vs], preferred_element_type=F32)
        o_ref[r0:r0 + tq, vs] = (o / l).astype(o_ref.dtype)

    tiles = [(i, hh) for i in reversed(range(S // tq)) for hh in range(nh)]
    pending = [scores(*t) for t in tiles[:MLA_SCORES_AHEAD]]
    for n, t in enumerate(tiles):
        if n + MLA_SCORES_AHEAD < len(tiles):
            pending.append(scores(*tiles[n + MLA_SCORES_AHEAD]))
        finish(*t, *pending.pop(0))


def _mla_attention(q, k, v, B, S, H):
    T = B * S
    tq = min(ATTN_TQ, S)
    nh = 2 if H % 2 == 0 else 1
    kern = functools.partial(_mla_attn_kernel, S, tq, nh)
    nbytes = 2 * nh * S * (2 * MLA_HEAD_PAD + 2 * MLA_V) * 2 + 12 * tq * S * 4
    return pl.pallas_call(
        kern,
        name="mla_attn",
        grid=(B, H // nh),
        in_specs=[pl.BlockSpec((S, nh * MLA_HEAD_PAD), lambda b, h: (b, h)),
                  pl.BlockSpec((S, nh * MLA_HEAD_PAD), lambda b, h: (b, h)),
                  pl.BlockSpec((S, nh * MLA_V), lambda b, h: (b, h))],
        out_specs=pl.BlockSpec((S, nh * MLA_V), lambda b, h: (b, h)),
        out_shape=jax.ShapeDtypeStruct((T, H * MLA_V), BF16),
        compiler_params=_params(("arbitrary", "arbitrary"), nbytes),
    )(q, k, v)


def _mla_layer(h, act, a, B, S, tabs, mla_w_in, mla_g_cq, mla_g_ckv, mla_w_uq, mla_w_ukv, mla_g_q, mla_g_k, mla_w_o,
               next_norms):
    T, D = h.shape
    q_lora = mla_g_cq.shape[-1]
    kv_lora = mla_g_ckv.shape[-1]
    H = mla_w_ukv.shape[-1] // (MLA_NOPE + MLA_V)
    c_tab, s_tab = tabs
    tm = min(2048, T)
    lat = q_lora + kv_lora
    assert q_lora % kv_lora == 0
    w_in_t = jnp.swapaxes(mla_w_in, 1, 2)
    cq = _mm_plain(act, w_in_t, a, 0, q_lora, min(512, q_lora), F32, w_transposed=True)
    ckv = _mm_plain(act, w_in_t, a, q_lora // kv_lora, kv_lora, kv_lora, F32, w_transposed=True)
    w_kr = lax.slice(mla_w_in, (a, 0, lat), (a + 1, D, lat + MLA_ROPE))[0]
    kr = _mm_plain(act, jnp.concatenate([w_kr, w_kr], axis=1), None, 0, LANES, LANES, F32)
    cq = _rmsnorm(cq, mla_g_cq[a])
    ckv = _rmsnorm(ckv, mla_g_ckv[a])
    w_uq = mla_w_uq[a].reshape(q_lora, H, MLA_QK)
    w_uq = jnp.concatenate([w_uq, w_uq[:, :, MLA_NOPE:]], axis=2).reshape(q_lora, H * MLA_HEAD_PAD)
    hp = 2 if H % 2 == 0 else 1
    tn = hp * MLA_HEAD_PAD
    tab_specs = [(t, (tm, LANES), lambda j, i: (i, 0)) for t in (c_tab, s_tab)]
    gain = lambda g: (_pad_gain_mla(g), (1, MLA_HEAD_PAD), lambda j, i: (0, 0))
    q = _mm((cq, None), [(w_uq, None, 0)], functools.partial(_epi_mla_q, hp, MLA_QK ** -0.5 * LOG2E),
            [gain(mla_g_q[a])] + tab_specs,
            [(jax.ShapeDtypeStruct((T, H * MLA_HEAD_PAD), BF16), (tm, tn), lambda j, i: (i, j))],
            n_cols=H * MLA_HEAD_PAD, tn=tn, tm=tm)[0]
    kr_spec = (kr, (tm, LANES), lambda j, i: (i, 0))
    k, v = _mm((ckv, None), [(mla_w_ukv, a, 0)], functools.partial(_epi_mla_kv, hp),
               [gain(mla_g_k[a])] + tab_specs + [kr_spec],
               [(jax.ShapeDtypeStruct((T, H * MLA_HEAD_PAD), BF16), (tm, tn), lambda j, i: (i, j)),
                (jax.ShapeDtypeStruct((T, H * MLA_V), BF16), (tm, hp * MLA_V), lambda j, i: (i, j))],
               n_cols=H * MLA_HEAD_PAD, tn=tn, tm=tm)
    o = _mla_attention(q, k, v, B, S, H)
    return _mm_residual(o, mla_w_o, a, h, 1.0, next_norms)


def _nsa_head_norm(a, g2):
    sel = _sel_matrix((MXU_DIM, MXU_DIM), lambda r, col: r // NSA_DH == col // NSA_DH)
    r = lax.rsqrt(_mxu_row_sum(a * a, sel) / NSA_DH + EPS)
    return a * r * g2


def _epi_nsa_k(n_pairs, accs, extra, outs, scr):
    g = extra[0][...]
    g2 = jnp.concatenate([g, g], axis=1)
    c, s = extra[1][...], extra[2][...]
    o = outs[0]
    acc = accs[0]
    for pp in range(n_pairs):
        kn = _nsa_head_norm(acc[:, pp * MXU_DIM:(pp + 1) * MXU_DIM], g2)
        for hh in range(2):
            col = (2 * pp + hh) * NSA_DH
            o[:, col:col + NSA_DH] = _rope_nsa(kn[:, hh * NSA_DH:(hh + 1) * NSA_DH], c, s).astype(o.dtype)


def _compress_kernel(n_cmp, k_ref, v_ref, pk_ref, pv_ref, kw1_ref, kb1_ref, kw2_ref,
                     vw1_ref, vb1_ref, vw2_ref, gk_ref, ko_ref, vo_ref):
    half = CMP_STRIDE * NSA_DH
    n_rows = ko_ref.shape[0]
    row = lax.broadcasted_iota(jnp.int32, (n_rows, 1), 0)

    def phi(t_ref, pos_ref, w1_ref, b1_ref, w2_ref):
        slabs = [t_ref[pl.ds(l, n_rows, stride=CMP_STRIDE), :] for l in range(CMP_STRIDE)]
        lo = jnp.concatenate([slabs[l] + pos_ref[:, l * NSA_DH:(l + 1) * NSA_DH]
                              for l in range(CMP_STRIDE)], axis=1).astype(BF16)
        hi = jnp.concatenate([slabs[l] + pos_ref[:, half + l * NSA_DH:half + (l + 1) * NSA_DH]
                              for l in range(CMP_STRIDE)], axis=1).astype(BF16)
        p_lo = jnp.dot(lo, w1_ref[0:half, :].astype(BF16), preferred_element_type=F32)
        p_hi = jnp.dot(hi, w1_ref[half:2 * half, :].astype(BF16), preferred_element_type=F32)
        pre = p_lo + pltpu.roll(p_hi, n_rows - 1, 0) + b1_ref[...]
        hid = pre * (1.0 / (1.0 + jnp.exp(-pre)))
        out = jnp.dot(hid.astype(BF16), w2_ref[...].astype(BF16), preferred_element_type=F32)
        return jnp.where(row < n_cmp, out, 0.0)

    kc = phi(k_ref, pk_ref, kw1_ref, kb1_ref, kw2_ref)
    r = lax.rsqrt(jnp.mean(kc * kc, -1, keepdims=True) + EPS)
    ko_ref[...] = (kc * r * gk_ref[...]).astype(ko_ref.dtype)
    vo_ref[...] = phi(v_ref, pv_ref, vw1_ref, vb1_ref, vw2_ref).astype(vo_ref.dtype)


def _compress(raw, B, S, G, cmp_pos_k, cmp_pos_v, k_w1, k_b1, k_w2, v_w1, v_b1, v_w2, g_k_cmp):
    n_cmp = (S - CMP_BLOCK) // CMP_STRIDE + 1
    n_rows = S // CMP_STRIDE
    hid = k_w1.shape[-1]
    full = lambda shape: pl.BlockSpec(shape, lambda b, g: (0,) * len(shape))
    w_specs = [full((CMP_BLOCK * NSA_DH, hid)), full((1, hid)), full((hid, NSA_DH))]
    out_spec = pl.BlockSpec((None, None, n_rows, NSA_DH), lambda b, g: (b, g, 0, 0))
    out_sds = jax.ShapeDtypeStruct((B, G, n_rows, NSA_DH), BF16)
    nbytes = 4 * (2 * CMP_BLOCK * NSA_DH * hid * 4) + 4 * S * NSA_DH * 4 + 8 * n_rows * CMP_BLOCK * NSA_DH * 4
    return pl.pallas_call(
        functools.partial(_compress_kernel, n_cmp),
        name="nsa_compress",
        grid=(B, G),
        in_specs=[pl.BlockSpec((S, NSA_DH), lambda b, g: (b, g)),
                  pl.BlockSpec((S, NSA_DH), lambda b, g: (b, G + g)),
                  full((1, CMP_BLOCK * NSA_DH)), full((1, CMP_BLOCK * NSA_DH))]
                 + w_specs + w_specs + [full((1, NSA_DH))],
        out_specs=[out_spec, out_spec],
        out_shape=[out_sds, out_sds],
        compiler_params=_params(("arbitrary", "arbitrary"), nbytes),
    )(raw, raw, cmp_pos_k.reshape(1, -1), cmp_pos_v.reshape(1, -1),
      k_w1, k_b1.reshape(1, -1), k_w2, v_w1, v_b1.reshape(1, -1), v_w2, g_k_cmp.reshape(1, -1))


def _nsa_shared_kv(y, B, S, tabs, kv_w, cmp_pos_k, cmp_pos_v, cmp_k_w1, cmp_k_b1, cmp_k_w2,
                   cmp_v_w1, cmp_v_b1, cmp_v_w2, g_k_cmp, g_k_slc, g_k_win):
    T = y[0].shape[0]
    G = kv_w.shape[-1] // (2 * N_BRANCH * NSA_DH)
    assert G % 2 == 0
    part = G * NSA_DH
    tm = min(1024, T)
    c_tab, s_tab = tabs
    raw = _mm_plain(y, kv_w, None, 0, 2 * part, part, F32)
    k_cmp, v_cmp = _compress(raw, B, S, G, cmp_pos_k, cmp_pos_v, cmp_k_w1, cmp_k_b1, cmp_k_w2,
                             cmp_v_w1, cmp_v_b1, cmp_v_w2, g_k_cmp)

    def k_branch(part_idx, gain):
        extras = [(gain.reshape(1, NSA_DH), (1, NSA_DH), lambda j, i: (0, 0)),
                  (c_tab, (tm, LANES), lambda j, i: (i, 0)), (s_tab, (tm, LANES), lambda j, i: (i, 0))]
        return _mm(y, [(kv_w, None, part_idx)], functools.partial(_epi_nsa_k, G // 2), extras,
                   [(jax.ShapeDtypeStruct((T, part), BF16), (tm, part), lambda j, i: (i, j))],
                   n_cols=part, tn=part, tm=tm)[0]

    k_slc = k_branch(2, g_k_slc)
    v_slc = _mm_plain(y, kv_w, None, 3, part, part, BF16)
    k_win = k_branch(4, g_k_win)
    v_win = _mm_plain(y, kv_w, None, 5, part, part, BF16)
    return k_cmp, v_cmp, k_slc, v_slc, k_win, v_win


def _epi_nsa_q(n_pairs, out_scale, accs, extra, outs, scr):
    g = extra[0][...] * out_scale
    g2 = jnp.concatenate([g, g], axis=1)
    c, s = extra[1][...], extra[2][...]
    q_o, qr_o = outs
    acc = accs[0]
    for pp in range(n_pairs):
        qn = _nsa_head_norm(acc[:, pp * MXU_DIM:(pp + 1) * MXU_DIM], g2)
        for hh in range(2):
            x = qn[:, hh * NSA_DH:(hh + 1) * NSA_DH]
            q_o[2 * pp + hh] = x.astype(q_o.dtype)
            qr_o[2 * pp + hh] = _rope_nsa(x, c, s).astype(qr_o.dtype)


def _nsa_attn_kernel(S, tq, hpg, n_cmp,
                     q_ref, qr_ref, gl_ref, bg_ref, kc_ref, vc_ref, ks_ref, vs_ref, kw_ref, vw_ref,
                     o_ref, ocmp_scr, obuf_scr):
    g_idx = pl.program_id(1)
    n_cr = kc_ref.shape[0]
    n_slc = S // SLC_BLOCK
    n_sel = min(SLC_TOPK, n_slc)
    per_trip = NSA_HEADS_PER_TRIP if hpg % NSA_HEADS_PER_TRIP == 0 else 2
    k_cmp = kc_ref[...]
    v_cmp = vc_ref[...]
    lane = lax.broadcasted_iota(jnp.int32, (tq, LANES), 1)
    nt = (((1,), (1,)), ((), ()))

    jj = lax.broadcasted_iota(jnp.int32, (n_slc, n_cr), 0)
    nn = lax.broadcasted_iota(jnp.int32, (n_slc, n_cr), 1)
    cs = nn * CMP_STRIDE
    ss = jj * SLC_BLOCK
    ov = jnp.maximum(jnp.minimum(cs + CMP_BLOCK, ss + SLC_BLOCK) - jnp.maximum(cs, ss), 0)
    agg_t = jnp.where(nn < n_cmp, ov.astype(F32) / CMP_STRIDE, 0.0)

    for i in range(S // tq):
        r0 = i * tq
        nk = r0 + tq
        rows = slice(r0, r0 + tq)

        spos_c = r0 + lax.broadcasted_iota(jnp.int32, (tq, n_cr), 0)
        ncol = lax.broadcasted_iota(jnp.int32, (tq, n_cr), 1)
        cmask = (ncol * CMP_STRIDE + (CMP_BLOCK - 1) <= spos_c) & (ncol < n_cmp)
        cbias = jnp.where(cmask, 0.0, NEG)
        cmask_f = cmask.astype(F32)
        q_all = q_ref[:, rows, :].reshape(hpg * tq, NSA_DH)
        sc = lax.dot_general(q_all, k_cmp, nt, preferred_element_type=F32).reshape(hpg, tq, n_cr) + cbias
        m = jnp.max(sc, -1, keepdims=True)
        e = jnp.exp2(sc - m) * cmask_f
        l = jnp.sum(e, -1, keepdims=True)
        p = e / jnp.where(l > 0.0, l, 1.0)
        ocmp_scr[...] = jnp.dot(p.reshape(hpg * tq, n_cr).astype(BF16), v_cmp,
                                preferred_element_type=F32).reshape(hpg, tq, NSA_DH)
        psum = jnp.sum(p, axis=0)

        imp = lax.dot_general(agg_t, psum, nt, precision=lax.Precision.HIGHEST,
                              preferred_element_type=F32)
        jb = lax.broadcasted_iota(jnp.int32, (n_slc, tq), 0)
        sp = r0 + lax.broadcasted_iota(jnp.int32, (n_slc, tq), 1)
        cur = sp // SLC_BLOCK
        valid = jb * SLC_BLOCK <= sp
        forced = (jb == 0) | (jb == cur) | (jb == cur - 1)
        imp = jnp.where(forced, FORCE, jnp.where(valid, imp, -1.0))
        rank = jnp.zeros((n_slc, tq), jnp.int32)
        for jp in range(n_slc):
            other = imp[jp:jp + 1, :]
            ahead = (other > imp) | ((other == imp) & (jp < jb))
            rank = rank + ahead.astype(jnp.int32)
        sel_t = (rank < n_sel).astype(BF16)

        eb = lax.broadcasted_iota(jnp.int32, (n_slc, nk), 0)
        ek = lax.broadcasted_iota(jnp.int32, (n_slc, nk), 1)
        expand = (ek // SLC_BLOCK == eb).astype(BF16)
        sel_keys = lax.dot_general(sel_t, expand, (((0,), (0,)), ((), ())), preferred_element_type=F32)
        qrow = r0 + lax.broadcasted_iota(jnp.int32, (tq, nk), 0)
        kcol = lax.broadcasted_iota(jnp.int32, (tq, nk), 1)
        sbias = jnp.where((sel_keys > 0.5) & (kcol <= qrow), 0.0, NEG)

        w0 = (max(0, r0 - WINDOW) // tq) * tq
        nw = nk - w0
        wrow = r0 + lax.broadcasted_iota(jnp.int32, (tq, nw), 0)
        wcol = w0 + lax.broadcasted_iota(jnp.int32, (tq, nw), 1)
        wdiff = wrow - wcol
        wbias = jnp.where((wdiff >= 0) & (wdiff < WINDOW), 0.0, NEG)

        z = gl_ref[rows, :] + bg_ref[...]
        gates = 1.0 / (1.0 + jnp.exp(-z))

        k_s = ks_ref[0:nk, :]
        v_s = vs_ref[0:nk, :]
        k_w = kw_ref[w0:nk, :]
        v_w = vw_ref[w0:nk, :]

        def main_body(pp, carry):
            hs = tuple(per_trip * pp + t for t in range(per_trip))
            qs = [qr_ref[hh, rows, :] for hh in hs]
            s_s = [lax.dot_general(q, k_s, nt, preferred_element_type=F32) + sbias for q in qs]
            s_w = [lax.dot_general(q, k_w, nt, preferred_element_type=F32) + wbias for q in qs]
            m_s = [jnp.max(s, -1, keepdims=True) for s in s_s]
            m_w = [jnp.max(s, -1, keepdims=True) for s in s_w]
            e_s = [jnp.exp2(s - m) for s, m in zip(s_s, m_s)]
            e_w = [jnp.exp2(s - m) for s, m in zip(s_w, m_w)]
            l_s = [jnp.sum(e, -1, keepdims=True) for e in e_s]
            l_w = [jnp.sum(e, -1, keepdims=True) for e in e_w]
            o_s = [jnp.dot(e.astype(BF16), v_s, preferred_element_type=F32) / l for e, l in zip(e_s, l_s)]
            o_w = [jnp.dot(e.astype(BF16), v_w, preferred_element_type=F32) / l for e, l in zip(e_w, l_w)]
            for t, hh in enumerate(hs):
                c0 = (g_idx * hpg + hh) * N_BRANCH
                gate = lambda br: jnp.sum(jnp.where(lane == c0 + br, gates, 0.0), -1, keepdims=True)
                o = gate(0) * ocmp_scr[hh] + gate(1) * o_s[t] + gate(2) * o_w[t]
                obuf_scr[hh] = o.astype(obuf_scr.dtype)
            return carry

        lax.fori_loop(0, hpg // per_trip, main_body, 0)
        for hh in range(hpg):
            o_ref[rows, hh * NSA_DH:(hh + 1) * NSA_DH] = obuf_scr[hh]


def _nsa_attention(q, qr, gate_logits, b_gate, shared, B, S, G, hpg):
    k_cmp, v_cmp, k_slc, v_slc, k_win, v_win = shared
    T = B * S
    H = G * hpg
    tq = min(ATTN_TQ, S)
    n_cmp = (S - CMP_BLOCK) // CMP_STRIDE + 1
    n_cr = k_cmp.shape[2]
    kern = functools.partial(_nsa_attn_kernel, S, tq, hpg, n_cmp)
    head_spec = pl.BlockSpec((hpg, S, NSA_DH), lambda b, g: (g, b, 0))
    cmp_spec = pl.BlockSpec((None, None, n_cr, NSA_DH), lambda b, g: (b, g, 0, 0))
    kv_spec = pl.BlockSpec((S, NSA_DH), lambda b, g: (b, g))
    nbytes = (2 * 2 * hpg * S * NSA_DH * 2 + 2 * S * LANES * 4 + 8 * S * NSA_DH * 2
              + 2 * S * hpg * NSA_DH * 2 + 10 * tq * S * 4)
    return pl.pallas_call(
        kern,
        name="nsa_attn",
        grid=(B, G),
        in_specs=[head_spec, head_spec,
                  pl.BlockSpec((S, LANES), lambda b, g: (b, 0)),
                  pl.BlockSpec((1, LANES), lambda b, g: (0, 0)),
                  cmp_spec, cmp_spec, kv_spec, kv_spec, kv_spec, kv_spec],
        out_specs=pl.BlockSpec((S, hpg * NSA_DH), lambda b, g: (b, g)),
        out_shape=jax.ShapeDtypeStruct((T, H * NSA_DH), BF16),
        scratch_shapes=[pltpu.VMEM((hpg, tq, NSA_DH), F32), pltpu.VMEM((hpg, tq, NSA_DH), BF16)],
        compiler_params=_params(("arbitrary", "arbitrary"), nbytes),
    )(q, qr, gate_logits, b_gate, k_cmp, v_cmp, k_slc, v_slc, k_win, v_win)


def _nsa_layer(h, u, b, B, S, G, tabs, shared, nsa_w_in, nsa_b_gate, nsa_g_q, nsa_w_o, next_norms):
    T, D = h.shape
    n_gate = nsa_b_gate.shape[-1]
    H = n_gate // N_BRANCH
    hpg = H // G
    q_width = H * NSA_DH
    assert n_gate <= LANES and hpg % 2 == 0
    c_tab, s_tab = tabs
    tm = min(1024, T)
    hp = min(4, H)
    tn = hp * NSA_DH
    extras = [(nsa_g_q[b].reshape(1, NSA_DH), (1, NSA_DH), lambda j, i: (0, 0)),
              (c_tab, (tm, LANES), lambda j, i: (i, 0)), (s_tab, (tm, LANES), lambda j, i: (i, 0))]
    head_out = (jax.ShapeDtypeStruct((H, T, NSA_DH), BF16), (hp, tm, NSA_DH), lambda j, i: (j, i, 0))
    q, qr = _mm(u, [(jnp.swapaxes(nsa_w_in, 1, 2), b, 0)],
                functools.partial(_epi_nsa_q, hp // 2, NSA_DH ** -0.5 * LOG2E),
                extras, [head_out, head_out], n_cols=q_width, tn=tn, tm=tm, w_transposed=True)
    w_gate = lax.slice(nsa_w_in, (b, 0, q_width), (b + 1, D, q_width + n_gate))[0]
    w_gate = jnp.pad(w_gate, ((0, 0), (0, LANES - n_gate)))
    gate_logits = _mm_plain(u, w_gate, None, 0, LANES, LANES, F32)
    b_gate = jnp.pad(nsa_b_gate[b], (0, LANES - n_gate)).reshape(1, LANES)
    o = _nsa_attention(q, qr, gate_logits, b_gate, shared, B, S, G, hpg)
    return _mm_residual(o, nsa_w_o, b, h, 1.0, next_norms)


def _lane_table(values):
    return jnp.asarray(values, F32).reshape(1, LANES)


def kernel(x, positions, ffn1_norm, ffn1_w_gate, ffn1_w_up, ffn1_w_down, mix_norm, ffn2_norm, ffn2_w_gate, ffn2_w_up, ffn2_w_down, mla_w_in, mla_g_cq, mla_g_ckv, mla_w_uq, mla_w_ukv, mla_g_q, mla_g_k, mla_w_o, kv_norm, kv_w, cmp_pos_k, cmp_pos_v, cmp_k_w1, cmp_k_b1, cmp_k_w2, cmp_v_w1, cmp_v_b1, cmp_v_w2, g_k_cmp, g_k_slc, g_k_win, nsa_w_in, nsa_b_gate, nsa_g_q, nsa_w_o):
    B, S, D = x.shape
    T = B * S
    depth = ffn1_norm.shape[0]
    n_a = mla_w_in.shape[0]
    G = kv_w.shape[-1] // (2 * N_BRANCH * NSA_DH)

    pos = positions.reshape(T, 1).astype(F32)
    half_a = MLA_ROPE // 2
    inv_a = jnp.power(ROPE_THETA, -jnp.arange(0, MLA_ROPE, 2, dtype=F32) / MLA_ROPE)
    zeros_a = jnp.zeros((LANES - MLA_ROPE,), F32)
    ones_h = jnp.ones((half_a,), F32)
    tabs_a = _rope_tables(pos, _lane_table(jnp.concatenate([inv_a, inv_a, zeros_a])),
                          _lane_table(jnp.concatenate([ones_h, ones_h, zeros_a])),
                          _lane_table(jnp.concatenate([-ones_h, ones_h, zeros_a])))
    inv_b = jnp.power(ROPE_THETA, -jnp.arange(0, NSA_DH, 2, dtype=F32) / NSA_DH)
    ones_b = jnp.ones((NSA_DH // 2,), F32)
    tabs_b = _rope_tables(pos, _lane_table(jnp.concatenate([inv_b, inv_b])),
                          _lane_table(jnp.concatenate([ones_b, ones_b])),
                          _lane_table(jnp.concatenate([-ones_b, ones_b])))

    h = x.reshape(T, D)
    shared = None
    act = (_rmsnorm(h, ffn1_norm, layer=0), None)
    for layer in range(depth):
        h, (act,) = _ffn(h, act, ffn1_w_gate, ffn1_w_up, ffn1_w_down, layer, [(mix_norm, layer)])
        if layer < n_a:
            h, (act,) = _mla_layer(h, act, layer, B, S, tabs_a, mla_w_in, mla_g_cq, mla_g_ckv, mla_w_uq, mla_w_ukv,
                                   mla_g_q, mla_g_k, mla_w_o, [(ffn2_norm, layer)])
        else:
            h, (act,) = _nsa_layer(h, act, layer - n_a, B, S, G, tabs_b, shared, nsa_w_in, nsa_b_gate, nsa_g_q,
                                   nsa_w_o, [(ffn2_norm, layer)])
        norms = [(ffn1_norm, layer + 1)] if layer + 1 < depth else []
        if layer == n_a - 1:
            norms = norms + [(kv_norm, 0)]
        h, acts = _ffn(h, act, ffn2_w_gate, ffn2_w_up, ffn2_w_down, layer, norms)
        act = acts[0] if acts else None
        if layer == n_a - 1:
            shared = _nsa_shared_kv(acts[-1], B, S, tabs_b, kv_w, cmp_pos_k, cmp_pos_v, cmp_k_w1, cmp_k_b1,
                                    cmp_k_w2, cmp_v_w1, cmp_v_b1, cmp_v_w2, g_k_cmp, g_k_slc, g_k_win)
    return h.reshape(B, S, D)
```

```python
import functools
import math

import jax
import jax.numpy as jnp
from jax import lax
from jax.experimental import pallas as pl
from jax.experimental.pallas import tpu as pltpu

F32 = jnp.float32
BF16 = jnp.bfloat16

ROPE_THETA = 10000.0
EPS = 1e-6
NEG = -1e30
FORCE = 1e6
MLA_NOPE = 128
MLA_ROPE = 64
MLA_V = 128
MLA_QK = MLA_NOPE + MLA_ROPE
MLA_HEAD_PAD = 256
NSA_DH = 128
N_BRANCH = 3
CMP_BLOCK = 32
CMP_STRIDE = 16
SLC_BLOCK = 64
SLC_TOPK = 16
WINDOW = 512
LOG2E = math.log2(math.e)

LANES = 128
MXU_DIM = 256
V7X_VMEM_BYTES = 64 * 1024 * 1024
VMEM_CAP = V7X_VMEM_BYTES - 8 * 1024 * 1024

ATTN_TQ = 256
MLA_SCORES_AHEAD = 2
NSA_HEADS_PER_TRIP = 4


def _vmem_limit(nbytes):
    return int(min(VMEM_CAP, max(32 * 1024 * 1024, nbytes + 8 * 1024 * 1024)))


def _params(sem, nbytes):
    return pltpu.CompilerParams(dimension_semantics=sem, vmem_limit_bytes=_vmem_limit(nbytes))


def _rope_table_kernel(pos_ref, inv_ref, mc_ref, ms_ref, c_ref, s_ref):
    ang = pos_ref[...] * inv_ref[...]
    c_ref[...] = jnp.cos(ang) * mc_ref[...]
    s_ref[...] = jnp.sin(ang) * ms_ref[...]


def _rope_tables(pos, inv, mask_c, mask_s):
    T = pos.shape[0]
    tm = min(T, 1024)
    row = pl.BlockSpec((1, LANES), lambda i: (0, 0))
    out = pl.BlockSpec((tm, LANES), lambda i: (i, 0))
    return pl.pallas_call(
        _rope_table_kernel,
        name="rope_tables",
        grid=(T // tm,),
        in_specs=[pl.BlockSpec((tm, 1), lambda i: (i, 0)), row, row, row],
        out_specs=[out, out],
        out_shape=[jax.ShapeDtypeStruct((T, LANES), F32)] * 2,
        compiler_params=_params(("arbitrary",), 0),
    )(pos, inv, mask_c, mask_s)


def _rope_nsa(x, c, s):
    return x * c + pltpu.roll(x, 64, 1) * s


def _rope_mla(x, c, s):
    return x * c + pltpu.roll(x, 32, 1) * s


def _mxu_row_sum(sq, sel):
    return jnp.dot(sq.astype(BF16), sel, preferred_element_type=F32)


def _sel_matrix(shape, pred):
    r = lax.broadcasted_iota(jnp.int32, shape, 0)
    c = lax.broadcasted_iota(jnp.int32, shape, 1)
    return pred(r, c).astype(BF16)


def _rmsnorm_kernel(x_ref, g_ref, o_ref):
    x = x_ref[...]
    r = lax.rsqrt(jnp.mean(x * x, axis=-1, keepdims=True) + EPS)
    o_ref[...] = (x * r * g_ref[...]).astype(o_ref.dtype)


def _rmsnorm(x, g, *, layer=None):
    T, width = x.shape
    tm = min(T, 512)
    if layer is None:
        g = g.reshape(1, width)
        g_spec = pl.BlockSpec((1, width), lambda i: (0, 0))
    else:
        g = g.reshape(g.shape[0], 1, width)
        g_spec = pl.BlockSpec((None, 1, width), lambda i: (layer, 0, 0))
    return pl.pallas_call(
        _rmsnorm_kernel,
        name="rmsnorm",
        grid=(T // tm,),
        in_specs=[pl.BlockSpec((tm, width), lambda i: (i, 0)), g_spec],
        out_specs=pl.BlockSpec((tm, width), lambda i: (i, 0)),
        out_shape=jax.ShapeDtypeStruct((T, width), BF16),
        compiler_params=_params(("arbitrary",), 6 * tm * width * 4),
    )(x, g)


def _scale_rows(acc, rs):
    return jnp.concatenate([acc[:, k * LANES:(k + 1) * LANES] * rs for k in range(acc.shape[1] // LANES)], axis=1)


def _mm_kernel(n_w, has_rs, w_transposed, w_where, tn, n_extra, n_out, epilogue, x_ref, *refs):
    refs = list(refs)
    rs_ref = refs.pop(0) if has_rs else None
    w_hbm = refs[:n_w]
    extra = refs[n_w:n_w + n_extra]
    outs = refs[n_w + n_extra:n_w + n_extra + n_out]
    base = n_w + n_extra + n_out
    w_bf = refs[base:base + n_w]
    w_f32 = refs[base + n_w:base + 2 * n_w]
    w_sem = refs[base + 2 * n_w]
    scr = refs[base + 2 * n_w + 1:]
    j = pl.program_id(0)
    n_j = pl.num_programs(0)
    slot = j % 2

    def tile_copy(k, jj, sl):
        layer, off = w_where[k]
        src = w_hbm[k] if layer is None else w_hbm[k].at[layer]
        c0 = pl.multiple_of((jj + off) * tn, tn)
        src = src.at[pl.ds(c0, tn), :] if w_transposed else src.at[:, pl.ds(c0, tn)]
        return pltpu.make_async_copy(src, w_f32[k].at[sl], w_sem.at[k, sl])

    @pl.when(pl.program_id(1) == 0)
    def _():
        @pl.when(j == 0)
        def _():
            for k in range(n_w):
                tile_copy(k, j, slot).start()

        @pl.when(j + 1 < n_j)
        def _():
            for k in range(n_w):
                tile_copy(k, j + 1, 1 - slot).start()

        for k in range(n_w):
            tile_copy(k, j, slot).wait()
            w_bf[k][...] = w_f32[k][slot].astype(BF16)

    if scr:
        @pl.when((pl.program_id(0) == 0) & (pl.program_id(1) == 0))
        def _():
            for s in scr:
                s[...] = jnp.zeros(s.shape, s.dtype)

    x = x_ref[...]
    dims = (((1,), (1 if w_transposed else 0,)), ((), ()))
    accs = [lax.dot_general(x, s[...], dims, preferred_element_type=F32) for s in w_bf]
    if has_rs:
        rs = rs_ref[...]
        accs = [_scale_rows(a, rs) for a in accs]
    epilogue(accs, extra, outs, scr)


def _mm(act, weights, epilogue, extras, outs, *, n_cols, tn, tm=1024, scratch=(), w_transposed=False):
    x, row_scale = act
    M, K = x.shape
    tm = min(tm, M)
    assert M % tm == 0 and n_cols % tn == 0
    in_specs = [pl.BlockSpec((tm, K), lambda j, i: (i, 0))]
    args = [x]
    nbytes = 2 * tm * K * 2 + len(weights) * (2 * K * tn * 4 + K * tn * 2 + 2 * tm * tn * 4)
    if row_scale is not None:
        in_specs.append(pl.BlockSpec((tm, LANES), lambda j, i: (i, 0)))
        args.append(row_scale)
        nbytes += 2 * tm * LANES * 4
    for arr, layer, off in weights:
        in_specs.append(pl.BlockSpec(memory_space=pl.ANY))
        args.append(arr)
    w_tile = (tn, K) if w_transposed else (K, tn)
    for arr, bs, im in extras:
        in_specs.append(pl.BlockSpec(bs, im))
        args.append(arr)
        nbytes += 2 * 4 * functools.reduce(lambda a, b: a * (b or 1), bs, 1)
    out_specs, out_shapes = [], []
    for sds, bs, im in outs:
        out_specs.append(pl.BlockSpec(bs, im))
        out_shapes.append(sds)
        nbytes += 2 * sds.dtype.itemsize * functools.reduce(lambda a, b: a * (b or 1), bs, 1)
    for shape, dtype in scratch:
        nbytes += jnp.dtype(dtype).itemsize * functools.reduce(lambda a, b: a * b, shape, 1)
    kern = functools.partial(_mm_kernel, len(weights), row_scale is not None, w_transposed,
                             [(layer, off) for _, layer, off in weights], tn, len(extras), len(outs), epilogue)
    res = pl.pallas_call(
        kern,
        name="mm_" + getattr(epilogue, "func", epilogue).__name__[len("_epi_"):],
        grid=(n_cols // tn, M // tm),
        in_specs=in_specs,
        out_specs=out_specs,
        out_shape=out_shapes,
        scratch_shapes=[pltpu.VMEM(w_tile, BF16) for _ in weights]
                       + [pltpu.VMEM((2,) + w_tile, F32) for _ in weights]
                       + [pltpu.SemaphoreType.DMA((len(weights), 2))]
                       + [pltpu.VMEM(s, d) for s, d in scratch],
        compiler_params=_params(("arbitrary", "arbitrary"), nbytes),
    )(*args)
    return res


def _epi_plain(accs, extra, outs, scr):
    outs[0][...] = accs[0].astype(outs[0].dtype)


def _epi_swiglu(accs, extra, outs, scr):
    g, u = accs
    outs[0][...] = (g * (1.0 / (1.0 + jnp.exp(-g))) * u).astype(outs[0].dtype)


def _epi_residual(alpha, accs, extra, outs, scr):
    outs[0][...] = extra[0][...] + alpha * accs[0]


def _epi_residual_norm(alpha, width, accs, extra, outs, scr):
    res_ref = extra[0]
    h_o, rs_o = outs[0], outs[-1]
    ss = scr[0]
    i = pl.program_id(1)
    h = res_ref[...] + alpha * accs[0]
    h_o[...] = h
    for g_ref, hg_o in zip(extra[1:], outs[1:-1]):
        hg_o[...] = (h * g_ref[...]).astype(hg_o.dtype)
    sq = h * h
    part = sq[:, :LANES]
    for k in range(1, sq.shape[1] // LANES):
        part = part + sq[:, k * LANES:(k + 1) * LANES]
    tot = ss[i] + part
    ss[i] = tot

    @pl.when(pl.program_id(0) == pl.num_programs(0) - 1)
    def _():
        rs_o[...] = jnp.broadcast_to(lax.rsqrt(jnp.sum(tot, -1, keepdims=True) / width + EPS), rs_o.shape)


def _mm_plain(act, w, layer, col_off, n_cols, tn, dtype, w_transposed=False):
    M = act[0].shape[0]
    tm = min(1024, M)
    out = (jax.ShapeDtypeStruct((M, n_cols), dtype), (tm, tn), lambda j, i: (i, j))
    return _mm(act, [(w, layer, col_off)], _epi_plain, [], [out], n_cols=n_cols, tn=tn, tm=tm,
               w_transposed=w_transposed)[0]


def _mm_residual(x, w, layer, res, alpha, next_norms=()):
    M, N = res.shape
    K = x.shape[1]
    tn = 512 if N % 512 == 0 else 256
    tm = min(1024 if K <= 4096 else 512, M)
    blk = ((tm, tn), lambda j, i: (i, j))
    h_out = (jax.ShapeDtypeStruct((M, N), F32),) + blk
    if not next_norms:
        return _mm((x, None), [(w, layer, 0)], functools.partial(_epi_residual, alpha), [(res,) + blk],
                   [h_out], n_cols=N, tn=tn, tm=tm)[0], []
    g_specs = [(gains.reshape(-1, 1, N), (None, 1, tn), lambda j, i, gl=gl: (gl, 0, j)) for gains, gl in next_norms]
    hg_outs = [(jax.ShapeDtypeStruct((M, N), BF16),) + blk for _ in next_norms]
    last_j = N // tn - 1
    rs_out = (jax.ShapeDtypeStruct((M, LANES), F32), (tm, LANES), lambda j, i: (jnp.where(j == last_j, i, 0), 0))
    res_all = _mm((x, None), [(w, layer, 0)], functools.partial(_epi_residual_norm, alpha, N),
                  [(res,) + blk] + g_specs, [h_out] + hg_outs + [rs_out],
                  n_cols=N, tn=tn, tm=tm, scratch=[((M // tm, tm, LANES), F32)])
    return res_all[0], [(hg, res_all[-1]) for hg in res_all[1:-1]]


def _ffn(h, act, wg, wu, wd, layer, next_norms):
    M, D = h.shape
    F = wg.shape[-1]
    tn = 512 if F % 512 == 0 else 256
    tm = min(512, M)
    hid = _mm(act, [(wg, layer, 0), (wu, layer, 0)], _epi_swiglu, [],
              [(jax.ShapeDtypeStruct((M, F), BF16), (tm, tn), lambda j, i: (i, j))],
              n_cols=F, tn=tn, tm=tm)[0]
    return _mm_residual(hid, wd, layer, h, 0.5, next_norms)


def _epi_mla_q(n_heads, out_scale, accs, extra, outs, scr):
    g = extra[0][...] * out_scale
    c, s = extra[1][...], extra[2][...]
    o = outs[0]
    sel = _sel_matrix((MLA_HEAD_PAD, LANES), lambda r, col: r < MLA_QK)
    acc = accs[0]
    for hh in range(n_heads):
        b0 = hh * MLA_HEAD_PAD
        a = acc[:, b0:b0 + MLA_HEAD_PAD]
        r = lax.rsqrt(_mxu_row_sum(a * a, sel) / MLA_QK + EPS)
        o[:, b0:b0 + LANES] = (a[:, :LANES] * r * g[:, :LANES]).astype(o.dtype)
        o[:, b0 + LANES:b0 + 2 * LANES] = _rope_mla(a[:, LANES:] * r * g[:, LANES:], c, s).astype(o.dtype)


def _epi_mla_kv(n_heads, accs, extra, outs, scr):
    g = extra[0][...]
    c, s = extra[1][...], extra[2][...]
    kr = extra[3][...]
    kr2 = kr * kr
    k_o, v_o = outs
    sel = _sel_matrix((MLA_HEAD_PAD, LANES), lambda r, col: r < MLA_QK)
    acc = accs[0]
    for hh in range(n_heads):
        b0 = hh * MLA_HEAD_PAD
        a = acc[:, b0:b0 + MLA_HEAD_PAD]
        kn = a[:, :LANES]
        ss = _mxu_row_sum(jnp.concatenate([kn * kn, kr2], axis=1), sel)
        r = lax.rsqrt(ss / MLA_QK + EPS)
        k_o[:, b0:b0 + LANES] = (kn * r * g[:, :LANES]).astype(k_o.dtype)
        k_o[:, b0 + LANES:b0 + 2 * LANES] = _rope_mla(kr * r * g[:, LANES:], c, s).astype(k_o.dtype)
        v_o[:, hh * MLA_V:(hh + 1) * MLA_V] = a[:, LANES:].astype(v_o.dtype)


def _pad_gain_mla(g):
    return jnp.concatenate([g, g[MLA_NOPE:]]).reshape(1, MLA_HEAD_PAD)


def _mla_attn_kernel(S, tq, nh, q_ref, k_ref, v_ref, o_ref):
    row = lax.broadcasted_iota(jnp.int32, (tq, tq), 0)
    col = lax.broadcasted_iota(jnp.int32, (tq, tq), 1)
    tri = jnp.where(col <= row, 0.0, NEG)
    nt = (((1,), (1,)), ((), ()))

    def scores(i, hh):
        r0 = i * tq
        ks = slice(hh * MLA_HEAD_PAD, (hh + 1) * MLA_HEAD_PAD)
        q = q_ref[r0:r0 + tq, ks]
        s_d = lax.dot_general(q, k_ref[r0:r0 + tq, ks], nt, preferred_element_type=F32) + tri
        s_o = lax.dot_general(q, k_ref[0:r0, ks], nt, preferred_element_type=F32) if i > 0 else None
        return s_d, s_o

    def finish(i, hh, s_d, s_o):
        r0 = i * tq
        vs = slice(hh * MLA_V, (hh + 1) * MLA_V)
        m = jnp.max(s_d, -1, keepdims=True)
        if i > 0:
            m = jnp.maximum(m, jnp.max(s_o, -1, keepdims=True))
        e_d = jnp.exp2(s_d - m)
        l = jnp.sum(e_d, -1, keepdims=True)
        o = jnp.dot(e_d.astype(BF16), v_ref[r0:r0 + tq, vs], preferred_element_type=F32)
        if i > 0:
            e_o = jnp.exp2(s_o - m)
            l = l + jnp.sum(e_o, -1, keepdims=True)
            o = o + jnp.dot(e_o.astype(BF16), v_ref[0:r0, vs], preferred_element_type=F32)
        o_ref[r0:r0 + tq, vs] = (o / l).astype(o_ref.dtype)

    tiles = [(i, hh) for i in reversed(range(S // tq)) for hh in range(nh)]
    pending = [scores(*t) for t in tiles[:MLA_SCORES_AHEAD]]
    for n, t in enumerate(tiles):
        if n + MLA_SCORES_AHEAD < len(tiles):
            pending.append(scores(*tiles[n + MLA_SCORES_AHEAD]))
        finish(*t, *pending.pop(0))


def _mla_attention(q, k, v, B, S, H):
    T = B * S
    tq = min(ATTN_TQ, S)
    nh = 2 if H % 2 == 0 else 1
    kern = functools.partial(_mla_attn_kernel, S, tq, nh)
    nbytes = 2 * nh * S * (2 * MLA_HEAD_PAD + 2 * MLA_V) * 2 + 12 * tq * S * 4
    return pl.pallas_call(
        kern,
        name="mla_attn",
        grid=(B, H // nh),
        in_specs=[pl.BlockSpec((S, nh * MLA_HEAD_PAD), lambda b, h: (b, h)),
                  pl.BlockSpec((S, nh * MLA_HEAD_PAD), lambda b, h: (b, h)),
                  pl.BlockSpec((S, nh * MLA_V), lambda b, h: (b, h))],
        out_specs=pl.BlockSpec((S, nh * MLA_V), lambda b, h: (b, h)),
        out_shape=jax.ShapeDtypeStruct((T, H * MLA_V), BF16),
        compiler_params=_params(("arbitrary", "arbitrary"), nbytes),
    )(q, k, v)


def _mla_layer(h, act, a, B, S, tabs, mla_w_in, mla_g_cq, mla_g_ckv, mla_w_uq, mla_w_ukv, mla_g_q, mla_g_k, mla_w_o,
               next_norms):
    T, D = h.shape
    q_lora = mla_g_cq.shape[-1]
    kv_lora = mla_g_ckv.shape[-1]
    H = mla_w_ukv.shape[-1] // (MLA_NOPE + MLA_V)
    c_tab, s_tab = tabs
    tm = min(2048, T)
    lat = q_lora + kv_lora
    assert q_lora % kv_lora == 0
    w_in_t = jnp.swapaxes(mla_w_in, 1, 2)
    cq = _mm_plain(act, w_in_t, a, 0, q_lora, min(512, q_lora), F32, w_transposed=True)
    ckv = _mm_plain(act, w_in_t, a, q_lora // kv_lora, kv_lora, kv_lora, F32, w_transposed=True)
    w_kr = lax.slice(mla_w_in, (a, 0, lat), (a + 1, D, lat + MLA_ROPE))[0]
    kr = _mm_plain(act, jnp.concatenate([w_kr, w_kr], axis=1), None, 0, LANES, LANES, F32)
    cq = _rmsnorm(cq, mla_g_cq[a])
    ckv = _rmsnorm(ckv, mla_g_ckv[a])
    w_uq = mla_w_uq[a].reshape(q_lora, H, MLA_QK)
    w_uq = jnp.concatenate([w_uq, w_uq[:, :, MLA_NOPE:]], axis=2).reshape(q_lora, H * MLA_HEAD_PAD)
    hp = 2 if H % 2 == 0 else 1
    tn = hp * MLA_HEAD_PAD
    tab_specs = [(t, (tm, LANES), lambda j, i: (i, 0)) for t in (c_tab, s_tab)]
    gain = lambda g: (_pad_gain_mla(g), (1, MLA_HEAD_PAD), lambda j, i: (0, 0))
    q = _mm((cq, None), [(w_uq, None, 0)], functools.partial(_epi_mla_q, hp, MLA_QK ** -0.5 * LOG2E),
            [gain(mla_g_q[a])] + tab_specs,
            [(jax.ShapeDtypeStruct((T, H * MLA_HEAD_PAD), BF16), (tm, tn), lambda j, i: (i, j))],
            n_cols=H * MLA_HEAD_PAD, tn=tn, tm=tm)[0]
    kr_spec = (kr, (tm, LANES), lambda j, i: (i, 0))
    k, v = _mm((ckv, None), [(mla_w_ukv, a, 0)], functools.partial(_epi_mla_kv, hp),
               [gain(mla_g_k[a])] + tab_specs + [kr_spec],
               [(jax.ShapeDtypeStruct((T, H * MLA_HEAD_PAD), BF16), (tm, tn), lambda j, i: (i, j)),
                (jax.ShapeDtypeStruct((T, H * MLA_V), BF16), (tm, hp * MLA_V), lambda j, i: (i, j))],
               n_cols=H * MLA_HEAD_PAD, tn=tn, tm=tm)
    o = _mla_attention(q, k, v, B, S, H)
    return _mm_residual(o, mla_w_o, a, h, 1.0, next_norms)


def _nsa_head_norm(a, g2):
    sel = _sel_matrix((MXU_DIM, MXU_DIM), lambda r, col: r // NSA_DH == col // NSA_DH)
    r = lax.rsqrt(_mxu_row_sum(a * a, sel) / NSA_DH + EPS)
    return a * r * g2


def _epi_nsa_k(n_pairs, accs, extra, outs, scr):
    g = extra[0][...]
    g2 = jnp.concatenate([g, g], axis=1)
    c, s = extra[1][...], extra[2][...]
    o = outs[0]
    acc = accs[0]
    for pp in range(n_pairs):
        kn = _nsa_head_norm(acc[:, pp * MXU_DIM:(pp + 1) * MXU_DIM], g2)
        for hh in range(2):
            col = (2 * pp + hh) * NSA_DH
            o[:, col:col + NSA_DH] = _rope_nsa(kn[:, hh * NSA_DH:(hh + 1) * NSA_DH], c, s).astype(o.dtype)


def _compress_kernel(n_cmp, k_ref, v_ref, pk_ref, pv_ref, kw1_ref, kb1_ref, kw2_ref,
                     vw1_ref, vb1_ref, vw2_ref, gk_ref, ko_ref, vo_ref):
    half = CMP_STRIDE * NSA_DH
    n_rows = ko_ref.shape[0]
    row = lax.broadcasted_iota(jnp.int32, (n_rows, 1), 0)

    def phi(t_ref, pos_ref, w1_ref, b1_ref, w2_ref):
        slabs = [t_ref[pl.ds(l, n_rows, stride=CMP_STRIDE), :] for l in range(CMP_STRIDE)]
        lo = jnp.concatenate([slabs[l] + pos_ref[:, l * NSA_DH:(l + 1) * NSA_DH]
                              for l in range(CMP_STRIDE)], axis=1).astype(BF16)
        hi = jnp.concatenate([slabs[l] + pos_ref[:, half + l * NSA_DH:half + (l + 1) * NSA_DH]
                              for l in range(CMP_STRIDE)], axis=1).astype(BF16)
        p_lo = jnp.dot(lo, w1_ref[0:half, :].astype(BF16), preferred_element_type=F32)
        p_hi = jnp.dot(hi, w1_ref[half:2 * half, :].astype(BF16), preferred_element_type=F32)
        pre = p_lo + pltpu.roll(p_hi, n_rows - 1, 0) + b1_ref[...]
        hid = pre * (1.0 / (1.0 + jnp.exp(-pre)))
        out = jnp.dot(hid.astype(BF16), w2_ref[...].astype(BF16), preferred_element_type=F32)
        return jnp.where(row < n_cmp, out, 0.0)

    kc = phi(k_ref, pk_ref, kw1_ref, kb1_ref, kw2_ref)
    r = lax.rsqrt(jnp.mean(kc * kc, -1, keepdims=True) + EPS)
    ko_ref[...] = (kc * r * gk_ref[...]).astype(ko_ref.dtype)
    vo_ref[...] = phi(v_ref, pv_ref, vw1_ref, vb1_ref, vw2_ref).astype(vo_ref.dtype)


def _compress(raw, B, S, G, cmp_pos_k, cmp_pos_v, k_w1, k_b1, k_w2, v_w1, v_b1, v_w2, g_k_cmp):
    n_cmp = (S - CMP_BLOCK) // CMP_STRIDE + 1
    n_rows = S // CMP_STRIDE
    hid = k_w1.shape[-1]
    full = lambda shape: pl.BlockSpec(shape, lambda b, g: (0,) * len(shape))
    w_specs = [full((CMP_BLOCK * NSA_DH, hid)), full((1, hid)), full((hid, NSA_DH))]
    out_spec = pl.BlockSpec((None, None, n_rows, NSA_DH), lambda b, g: (b, g, 0, 0))
    out_sds = jax.ShapeDtypeStruct((B, G, n_rows, NSA_DH), BF16)
    nbytes = 4 * (2 * CMP_BLOCK * NSA_DH * hid * 4) + 4 * S * NSA_DH * 4 + 8 * n_rows * CMP_BLOCK * NSA_DH * 4
    return pl.pallas_call(
        functools.partial(_compress_kernel, n_cmp),
        name="nsa_compress",
        grid=(B, G),
        in_specs=[pl.BlockSpec((S, NSA_DH), lambda b, g: (b, g)),
                  pl.BlockSpec((S, NSA_DH), lambda b, g: (b, G + g)),
                  full((1, CMP_BLOCK * NSA_DH)), full((1, CMP_BLOCK * NSA_DH))]
                 + w_specs + w_specs + [full((1, NSA_DH))],
        out_specs=[out_spec, out_spec],
        out_shape=[out_sds, out_sds],
        compiler_params=_params(("arbitrary", "arbitrary"), nbytes),
    )(raw, raw, cmp_pos_k.reshape(1, -1), cmp_pos_v.reshape(1, -1),
      k_w1, k_b1.reshape(1, -1), k_w2, v_w1, v_b1.reshape(1, -1), v_w2, g_k_cmp.reshape(1, -1))


def _nsa_shared_kv(y, B, S, tabs, kv_w, cmp_pos_k, cmp_pos_v, cmp_k_w1, cmp_k_b1, cmp_k_w2,
                   cmp_v_w1, cmp_v_b1, cmp_v_w2, g_k_cmp, g_k_slc, g_k_win):
    T = y[0].shape[0]
    G = kv_w.shape[-1] // (2 * N_BRANCH * NSA_DH)
    assert G % 2 == 0
    part = G * NSA_DH
    tm = min(1024, T)
    c_tab, s_tab = tabs
    raw = _mm_plain(y, kv_w, None, 0, 2 * part, part, F32)
    k_cmp, v_cmp = _compress(raw, B, S, G, cmp_pos_k, cmp_pos_v, cmp_k_w1, cmp_k_b1, cmp_k_w2,
                             cmp_v_w1, cmp_v_b1, cmp_v_w2, g_k_cmp)

    def k_branch(part_idx, gain):
        extras = [(gain.reshape(1, NSA_DH), (1, NSA_DH), lambda j, i: (0, 0)),
                  (c_tab, (tm, LANES), lambda j, i: (i, 0)), (s_tab, (tm, LANES), lambda j, i: (i, 0))]
        return _mm(y, [(kv_w, None, part_idx)], functools.partial(_epi_nsa_k, G // 2), extras,
                   [(jax.ShapeDtypeStruct((T, part), BF16), (tm, part), lambda j, i: (i, j))],
                   n_cols=part, tn=part, tm=tm)[0]

    k_slc = k_branch(2, g_k_slc)
    v_slc = _mm_plain(y, kv_w, None, 3, part, part, BF16)
    k_win = k_branch(4, g_k_win)
    v_win = _mm_plain(y, kv_w, None, 5, part, part, BF16)
    return k_cmp, v_cmp, k_slc, v_slc, k_win, v_win


def _epi_nsa_q(n_pairs, out_scale, accs, extra, outs, scr):
    g = extra[0][...] * out_scale
    g2 = jnp.concatenate([g, g], axis=1)
    c, s = extra[1][...], extra[2][...]
    q_o, qr_o = outs
    acc = accs[0]
    for pp in range(n_pairs):
        qn = _nsa_head_norm(acc[:, pp * MXU_DIM:(pp + 1) * MXU_DIM], g2)
        for hh in range(2):
            x = qn[:, hh * NSA_DH:(hh + 1) * NSA_DH]
            q_o[2 * pp + hh] = x.astype(q_o.dtype)
            qr_o[2 * pp + hh] = _rope_nsa(x, c, s).astype(qr_o.dtype)


def _nsa_attn_kernel(S, tq, hpg, n_cmp,
                     q_ref, qr_ref, gl_ref, bg_ref, kc_ref, vc_ref, ks_ref, vs_ref, kw_ref, vw_ref,
                     o_ref, ocmp_scr, obuf_scr):
    g_idx = pl.program_id(1)
    n_cr = kc_ref.shape[0]
    n_slc = S // SLC_BLOCK
    n_sel = min(SLC_TOPK, n_slc)
    per_trip = NSA_HEADS_PER_TRIP if hpg % NSA_HEADS_PER_TRIP == 0 else 2
    k_cmp = kc_ref[...]
    v_cmp = vc_ref[...]
    lane = lax.broadcasted_iota(jnp.int32, (tq, LANES), 1)
    nt = (((1,), (1,)), ((), ()))

    jj = lax.broadcasted_iota(jnp.int32, (n_slc, n_cr), 0)
    nn = lax.broadcasted_iota(jnp.int32, (n_slc, n_cr), 1)
    cs = nn * CMP_STRIDE
    ss = jj * SLC_BLOCK
    ov = jnp.maximum(jnp.minimum(cs + CMP_BLOCK, ss + SLC_BLOCK) - jnp.maximum(cs, ss), 0)
    agg_t = jnp.where(nn < n_cmp, ov.astype(F32) / CMP_STRIDE, 0.0)

    for i in range(S // tq):
        r0 = i * tq
        nk = r0 + tq
        rows = slice(r0, r0 + tq)

        spos_c = r0 + lax.broadcasted_iota(jnp.int32, (tq, n_cr), 0)
        ncol = lax.broadcasted_iota(jnp.int32, (tq, n_cr), 1)
        cmask = (ncol * CMP_STRIDE + (CMP_BLOCK - 1) <= spos_c) & (ncol < n_cmp)
        cbias = jnp.where(cmask, 0.0, NEG)
        cmask_f = cmask.astype(F32)
        q_all = q_ref[:, rows, :].reshape(hpg * tq, NSA_DH)
        sc = lax.dot_general(q_all, k_cmp, nt, preferred_element_type=F32).reshape(hpg, tq, n_cr) + cbias
        m = jnp.max(sc, -1, keepdims=True)
        e = jnp.exp2(sc - m) * cmask_f
        l = jnp.sum(e, -1, keepdims=True)
        p = e / jnp.where(l > 0.0, l, 1.0)
        ocmp_scr[...] = jnp.dot(p.reshape(hpg * tq, n_cr).astype(BF16), v_cmp,
                                preferred_element_type=F32).reshape(hpg, tq, NSA_DH)
        psum = jnp.sum(p, axis=0)

        imp = lax.dot_general(agg_t, psum, nt, precision=lax.Precision.HIGHEST,
                              preferred_element_type=F32)
        jb = lax.broadcasted_iota(jnp.int32, (n_slc, tq), 0)
        sp = r0 + lax.broadcasted_iota(jnp.int32, (n_slc, tq), 1)
        cur = sp // SLC_BLOCK
        valid = jb * SLC_BLOCK <= sp
        forced = (jb == 0) | (jb == cur) | (jb == cur - 1)
        imp = jnp.where(forced, FORCE, jnp.where(valid, imp, -1.0))
        rank = jnp.zeros((n_slc, tq), jnp.int32)
        for jp in range(n_slc):
            other = imp[jp:jp + 1, :]
            ahead = (other > imp) | ((other == imp) & (jp < jb))
            rank = rank + ahead.astype(jnp.int32)
        sel_t = (rank < n_sel).astype(BF16)

        eb = lax.broadcasted_iota(jnp.int32, (n_slc, nk), 0)
        ek = lax.broadcasted_iota(jnp.int32, (n_slc, nk), 1)
        expand = (ek // SLC_BLOCK == eb).astype(BF16)
        sel_keys = lax.dot_general(sel_t, expand, (((0,), (0,)), ((), ())), preferred_element_type=F32)
        qrow = r0 + lax.broadcasted_iota(jnp.int32, (tq, nk), 0)
        kcol = lax.broadcasted_iota(jnp.int32, (tq, nk), 1)
        sbias = jnp.where((sel_keys > 0.5) & (kcol <= qrow), 0.0, NEG)

        w0 = (max(0, r0 - WINDOW) // tq) * tq
        nw = nk - w0
        wrow = r0 + lax.broadcasted_iota(jnp.int32, (tq, nw), 0)
        wcol = w0 + lax.broadcasted_iota(jnp.int32, (tq, nw), 1)
        wdiff = wrow - wcol
        wbias = jnp.where((wdiff >= 0) & (wdiff < WINDOW), 0.0, NEG)

        z = gl_ref[rows, :] + bg_ref[...]
        gates = 1.0 / (1.0 + jnp.exp(-z))

        k_s = ks_ref[0:nk, :]
        v_s = vs_ref[0:nk, :]
        k_w = kw_ref[w0:nk, :]
        v_w = vw_ref[w0:nk, :]

        def main_body(pp, carry):
            hs = tuple(per_trip * pp + t for t in range(per_trip))
            qs = [qr_ref[hh, rows, :] for hh in hs]
            s_s = [lax.dot_general(q, k_s, nt, preferred_element_type=F32) + sbias for q in qs]
            s_w = [lax.dot_general(q, k_w, nt, preferred_element_type=F32) + wbias for q in qs]
            m_s = [jnp.max(s, -1, keepdims=True) for s in s_s]
            m_w = [jnp.max(s, -1, keepdims=True) for s in s_w]
            e_s = [jnp.exp2(s - m) for s, m in zip(s_s, m_s)]
            e_w = [jnp.exp2(s - m) for s, m in zip(s_w, m_w)]
            l_s = [jnp.sum(e, -1, keepdims=True) for e in e_s]
            l_w = [jnp.sum(e, -1, keepdims=True) for e in e_w]
            o_s = [jnp.dot(e.astype(BF16), v_s, preferred_element_type=F32) / l for e, l in zip(e_s, l_s)]
            o_w = [jnp.dot(e.astype(BF16), v_w, preferred_element_type=F32) / l for e, l in zip(e_w, l_w)]
            for t, hh in enumerate(hs):
                c0 = (g_idx * hpg + hh) * N_BRANCH
                gate = lambda br: jnp.sum(jnp.where(lane == c0 + br, gates, 0.0), -1, keepdims=True)
                o = gate(0) * ocmp_scr[hh] + gate(1) * o_s[t] + gate(2) * o_w[t]
                obuf_scr[hh] = o.astype(obuf_scr.dtype)
            return carry

        lax.fori_loop(0, hpg // per_trip, main_body, 0)
        for hh in range(hpg):
            o_ref[rows, hh * NSA_DH:(hh + 1) * NSA_DH] = obuf_scr[hh]


def _nsa_attention(q, qr, gate_logits, b_gate, shared, B, S, G, hpg):
    k_cmp, v_cmp, k_slc, v_slc, k_win, v_win = shared
    T = B * S
    H = G * hpg
    tq = min(ATTN_TQ, S)
    n_cmp = (S - CMP_BLOCK) // CMP_STRIDE + 1
    n_cr = k_cmp.shape[2]
    kern = functools.partial(_nsa_attn_kernel, S, tq, hpg, n_cmp)
    head_spec = pl.BlockSpec((hpg, S, NSA_DH), lambda b, g: (g, b, 0))
    cmp_spec = pl.BlockSpec((None, None, n_cr, NSA_DH), lambda b, g: (b, g, 0, 0))
    kv_spec = pl.BlockSpec((S, NSA_DH), lambda b, g: (b, g))
    nbytes = (2 * 2 * hpg * S * NSA_DH * 2 + 2 * S * LANES * 4 + 8 * S * NSA_DH * 2
              + 2 * S * hpg * NSA_DH * 2 + 10 * tq * S * 4)
    return pl.pallas_call(
        kern,
        name="nsa_attn",
        grid=(B, G),
        in_specs=[head_spec, head_spec,
                  pl.BlockSpec((S, LANES), lambda b, g: (b, 0)),
                  pl.BlockSpec((1, LANES), lambda b, g: (0, 0)),
                  cmp_spec, cmp_spec, kv_spec, kv_spec, kv_spec, kv_spec],
        out_specs=pl.BlockSpec((S, hpg * NSA_DH), lambda b, g: (b, g)),
        out_shape=jax.ShapeDtypeStruct((T, H * NSA_DH), BF16),
        scratch_shapes=[pltpu.VMEM((hpg, tq, NSA_DH), F32), pltpu.VMEM((hpg, tq, NSA_DH), BF16)],
        compiler_params=_params(("arbitrary", "arbitrary"), nbytes),
    )(q, qr, gate_logits, b_gate, k_cmp, v_cmp, k_slc, v_slc, k_win, v_win)


def _nsa_layer(h, u, b, B, S, G, tabs, shared, nsa_w_in, nsa_b_gate, nsa_g_q, nsa_w_o, next_norms):
    T, D = h.shape
    n_gate = nsa_b_gate.shape[-1]
    H = n_gate // N_BRANCH
    hpg = H // G
    q_width = H * NSA_DH
    assert n_gate <= LANES and hpg % 2 == 0
    c_tab, s_tab = tabs
    tm = min(1024, T)
    hp = min(4, H)
    tn = hp * NSA_DH
    extras = [(nsa_g_q[b].reshape(1, NSA_DH), (1, NSA_DH), lambda j, i: (0, 0)),
              (c_tab, (tm, LANES), lambda j, i: (i, 0)), (s_tab, (tm, LANES), lambda j, i: (i, 0))]
    head_out = (jax.ShapeDtypeStruct((H, T, NSA_DH), BF16), (hp, tm, NSA_DH), lambda j, i: (j, i, 0))
    q, qr = _mm(u, [(jnp.swapaxes(nsa_w_in, 1, 2), b, 0)],
                functools.partial(_epi_nsa_q, hp // 2, NSA_DH ** -0.5 * LOG2E),
                extras, [head_out, head_out], n_cols=q_width, tn=tn, tm=tm, w_transposed=True)
    w_gate = lax.slice(nsa_w_in, (b, 0, q_width), (b + 1, D, q_width + n_gate))[0]
    w_gate = jnp.pad(w_gate, ((0, 0), (0, LANES - n_gate)))
    gate_logits = _mm_plain(u, w_gate, None, 0, LANES, LANES, F32)
    b_gate = jnp.pad(nsa_b_gate[b], (0, LANES - n_gate)).reshape(1, LANES)
    o = _nsa_attention(q, qr, gate_logits, b_gate, shared, B, S, G, hpg)
    return _mm_residual(o, nsa_w_o, b, h, 1.0, next_norms)


def _lane_table(values):
    return jnp.asarray(values, F32).reshape(1, LANES)


def kernel(x, positions, ffn1_norm, ffn1_w_gate, ffn1_w_up, ffn1_w_down, mix_norm, ffn2_norm, ffn2_w_gate, ffn2_w_up, ffn2_w_down, mla_w_in, mla_g_cq, mla_g_ckv, mla_w_uq, mla_w_ukv, mla_g_q, mla_g_k, mla_w_o, kv_norm, kv_w, cmp_pos_k, cmp_pos_v, cmp_k_w1, cmp_k_b1, cmp_k_w2, cmp_v_w1, cmp_v_b1, cmp_v_w2, g_k_cmp, g_k_slc, g_k_win, nsa_w_in, nsa_b_gate, nsa_g_q, nsa_w_o):
    B, S, D = x.shape
    T = B * S
    depth = ffn1_norm.shape[0]
    n_a = mla_w_in.shape[0]
    G = kv_w.shape[-1] // (2 * N_BRANCH * NSA_DH)

    pos = positions.reshape(T, 1).astype(F32)
    half_a = MLA_ROPE // 2
    inv_a = jnp.power(ROPE_THETA, -jnp.arange(0, MLA_ROPE, 2, dtype=F32) / MLA_ROPE)
    zeros_a = jnp.zeros((LANES - MLA_ROPE,), F32)
    ones_h = jnp.ones((half_a,), F32)
    tabs_a = _rope_tables(pos, _lane_table(jnp.concatenate([inv_a, inv_a, zeros_a])),
                          _lane_table(jnp.concatenate([ones_h, ones_h, zeros_a])),
                          _lane_table(jnp.concatenate([-ones_h, ones_h, zeros_a])))
    inv_b = jnp.power(ROPE_THETA, -jnp.arange(0, NSA_DH, 2, dtype=F32) / NSA_DH)
    ones_b = jnp.ones((NSA_DH // 2,), F32)
    tabs_b = _rope_tables(pos, _lane_table(jnp.concatenate([inv_b, inv_b])),
                          _lane_table(jnp.concatenate([ones_b, ones_b])),
                          _lane_table(jnp.concatenate([-ones_b, ones_b])))

    h = x.reshape(T, D)
    shared = None
    act = (_rmsnorm(h, ffn1_norm, layer=0), None)
    for layer in range(depth):
        h, (act,) = _ffn(h, act, ffn1_w_gate, ffn1_w_up, ffn1_w_down, layer, [(mix_norm, layer)])
        if layer < n_a:
            h, (act,) = _mla_layer(h, act, layer, B, S, tabs_a, mla_w_in, mla_g_cq, mla_g_ckv, mla_w_uq, mla_w_ukv,
                                   mla_g_q, mla_g_k, mla_w_o, [(ffn2_norm, layer)])
        else:
            h, (act,) = _nsa_layer(h, act, layer - n_a, B, S, G, tabs_b, shared, nsa_w_in, nsa_b_gate, nsa_g_q,
                                   nsa_w_o, [(ffn2_norm, layer)])
        norms = [(ffn1_norm, layer + 1)] if layer + 1 < depth else []
        if layer == n_a - 1:
            norms = norms + [(kv_norm, 0)]
        h, acts = _ffn(h, act, ffn2_w_gate, ffn2_w_up, ffn2_w_down, layer, norms)
        act = acts[0] if acts else None
        if layer == n_a - 1:
            shared = _nsa_shared_kv(acts[-1], B, S, tabs_b, kv_w, cmp_pos_k, cmp_pos_v, cmp_k_w1, cmp_k_b1,
                                    cmp_k_w2, cmp_v_w1, cmp_v_b1, cmp_v_w2, g_k_cmp, g_k_slc, g_k_win)
    return h.reshape(B, S, D)
```

```python
import functools
import math

import jax
import jax.numpy as jnp
from jax import lax
from jax.experimental import pallas as pl
from jax.experimental.pallas import tpu as pltpu

F32 = jnp.float32
BF16 = jnp.bfloat16

ROPE_THETA = 10000.0
EPS = 1e-6
NEG = -1e30
FORCE = 1e6
MLA_NOPE = 128
MLA_ROPE = 64
MLA_V = 128
MLA_QK = MLA_NOPE + MLA_ROPE
MLA_HEAD_PAD = 256
NSA_DH = 128
N_BRANCH = 3
CMP_BLOCK = 32
CMP_STRIDE = 16
SLC_BLOCK = 64
SLC_TOPK = 16
WINDOW = 512
LOG2E = math.log2(math.e)

LANES = 128
MXU_DIM = 256
V7X_VMEM_BYTES = 64 * 1024 * 1024
VMEM_CAP = V7X_VMEM_BYTES - 8 * 1024 * 1024

ATTN_TQ = 256
MLA_SCORES_AHEAD = 2
NSA_HEADS_PER_TRIP = 4


def _vmem_limit(nbytes):
    return int(min(VMEM_CAP, max(32 * 1024 * 1024, nbytes + 8 * 1024 * 1024)))


def _params(sem, nbytes):
    return pltpu.CompilerParams(dimension_semantics=sem, vmem_limit_bytes=_vmem_limit(nbytes))


def _rope_table_kernel(pos_ref, inv_ref, mc_ref, ms_ref, c_ref, s_ref):
    ang = pos_ref[...] * inv_ref[...]
    c_ref[...] = jnp.cos(ang) * mc_ref[...]
    s_ref[...] = jnp.sin(ang) * ms_ref[...]


def _rope_tables(pos, inv, mask_c, mask_s):
    T = pos.shape[0]
    tm = min(T, 1024)
    row = pl.BlockSpec((1, LANES), lambda i: (0, 0))
    out = pl.BlockSpec((tm, LANES), lambda i: (i, 0))
    return pl.pallas_call(
        _rope_table_kernel,
        name="rope_tables",
        grid=(T // tm,),
        in_specs=[pl.BlockSpec((tm, 1), lambda i: (i, 0)), row, row, row],
        out_specs=[out, out],
        out_shape=[jax.ShapeDtypeStruct((T, LANES), F32)] * 2,
        compiler_params=_params(("arbitrary",), 0),
    )(pos, inv, mask_c, mask_s)


def _rope_nsa(x, c, s):
    return x * c + pltpu.roll(x, 64, 1) * s


def _rope_mla(x, c, s):
    return x * c + pltpu.roll(x, 32, 1) * s


def _mxu_row_sum(sq, sel):
    return jnp.dot(sq.astype(BF16), sel, preferred_element_type=F32)


def _sel_matrix(shape, pred):
    r = lax.broadcasted_iota(jnp.int32, shape, 0)
    c = lax.broadcasted_iota(jnp.int32, shape, 1)
    return pred(r, c).astype(BF16)


def _rmsnorm_kernel(x_ref, g_ref, o_ref):
    x = x_ref[...]
    r = lax.rsqrt(jnp.mean(x * x, axis=-1, keepdims=True) + EPS)
    o_ref[...] = (x * r * g_ref[...]).astype(o_ref.dtype)


def _rmsnorm(x, g, *, layer=None):
    T, width = x.shape
    tm = min(T, 512)
    if layer is None:
        g = g.reshape(1, width)
        g_spec = pl.BlockSpec((1, width), lambda i: (0, 0))
    else:
        g = g.reshape(g.shape[0], 1, width)
        g_spec = pl.BlockSpec((None, 1, width), lambda i: (layer, 0, 0))
    return pl.pallas_call(
        _rmsnorm_kernel,
        name="rmsnorm",
        grid=(T // tm,),
        in_specs=[pl.BlockSpec((tm, width), lambda i: (i, 0)), g_spec],
        out_specs=pl.BlockSpec((tm, width), lambda i: (i, 0)),
        out_shape=jax.ShapeDtypeStruct((T, width), BF16),
        compiler_params=_params(("arbitrary",), 6 * tm * width * 4),
    )(x, g)


def _scale_rows(acc, rs):
    return jnp.concatenate([acc[:, k * LANES:(k + 1) * LANES] * rs for k in range(acc.shape[1] // LANES)], axis=1)


def _mm_kernel(n_w, has_rs, w_transposed, w_where, tn, n_extra, n_out, epilogue, x_ref, *refs):
    refs = list(refs)
    rs_ref = refs.pop(0) if has_rs else None
    w_hbm = refs[:n_w]
    extra = refs[n_w:n_w + n_extra]
    outs = refs[n_w + n_extra:n_w + n_extra + n_out]
    base = n_w + n_extra + n_out
    w_bf = refs[base:base + n_w]
    w_f32 = refs[base + n_w:base + 2 * n_w]
    w_sem = refs[base + 2 * n_w]
    scr = refs[base + 2 * n_w + 1:]
    j = pl.program_id(0)
    n_j = pl.num_programs(0)
    slot = j % 2

    def tile_copy(k, jj, sl):
        layer, off = w_where[k]
        src = w_hbm[k] if layer is None else w_hbm[k].at[layer]
        c0 = pl.multiple_of((jj + off) * tn, tn)
        src = src.at[pl.ds(c0, tn), :] if w_transposed else src.at[:, pl.ds(c0, tn)]
        return pltpu.make_async_copy(src, w_f32[k].at[sl], w_sem.at[k, sl])

    @pl.when(pl.program_id(1) == 0)
    def _():
        @pl.when(j == 0)
        def _():
            for k in range(n_w):
                tile_copy(k, j, slot).start()

        @pl.when(j + 1 < n_j)
        def _():
            for k in range(n_w):
                tile_copy(k, j + 1, 1 - slot).start(priority=1)

        for k in range(n_w):
            tile_copy(k, j, slot).wait()
            w_bf[k][...] = w_f32[k][slot].astype(BF16)

    if scr:
        @pl.when((pl.program_id(0) == 0) & (pl.program_id(1) == 0))
        def _():
            for s in scr:
                s[...] = jnp.zeros(s.shape, s.dtype)

    x = x_ref[...]
    dims = (((1,), (1 if w_transposed else 0,)), ((), ()))
    accs = [lax.dot_general(x, s[...], dims, preferred_element_type=F32) for s in w_bf]
    if has_rs:
        rs = rs_ref[...]
        accs = [_scale_rows(a, rs) for a in accs]
    epilogue(accs, extra, outs, scr)


def _mm(act, weights, epilogue, extras, outs, *, n_cols, tn, tm=1024, scratch=(), w_transposed=False):
    x, row_scale = act
    M, K = x.shape
    tm = min(tm, M)
    assert M % tm == 0 and n_cols % tn == 0
    in_specs = [pl.BlockSpec((tm, K), lambda j, i: (i, 0))]
    args = [x]
    nbytes = 2 * tm * K * 2 + len(weights) * (2 * K * tn * 4 + K * tn * 2 + 2 * tm * tn * 4)
    if row_scale is not None:
        in_specs.append(pl.BlockSpec((tm, LANES), lambda j, i: (i, 0)))
        args.append(row_scale)
        nbytes += 2 * tm * LANES * 4
    for arr, layer, off in weights:
        in_specs.append(pl.BlockSpec(memory_space=pl.ANY))
        args.append(arr)
    w_tile = (tn, K) if w_transposed else (K, tn)
    for arr, bs, im in extras:
        in_specs.append(pl.BlockSpec(bs, im))
        args.append(arr)
        nbytes += 2 * 4 * functools.reduce(lambda a, b: a * (b or 1), bs, 1)
    out_specs, out_shapes = [], []
    for sds, bs, im in outs:
        out_specs.append(pl.BlockSpec(bs, im))
        out_shapes.append(sds)
        nbytes += 2 * sds.dtype.itemsize * functools.reduce(lambda a, b: a * (b or 1), bs, 1)
    for shape, dtype in scratch:
        nbytes += jnp.dtype(dtype).itemsize * functools.reduce(lambda a, b: a * b, shape, 1)
    kern = functools.partial(_mm_kernel, len(weights), row_scale is not None, w_transposed,
                             [(layer, off) for _, layer, off in weights], tn, len(extras), len(outs), epilogue)
    res = pl.pallas_call(
        kern,
        name="mm_" + getattr(epilogue, "func", epilogue).__name__[len("_epi_"):],
        grid=(n_cols // tn, M // tm),
        in_specs=in_specs,
        out_specs=out_specs,
        out_shape=out_shapes,
        scratch_shapes=[pltpu.VMEM(w_tile, BF16) for _ in weights]
                       + [pltpu.VMEM((2,) + w_tile, F32) for _ in weights]
                       + [pltpu.SemaphoreType.DMA((len(weights), 2))]
                       + [pltpu.VMEM(s, d) for s, d in scratch],
        compiler_params=_params(("arbitrary", "arbitrary"), nbytes),
    )(*args)
    return res


def _epi_plain(accs, extra, outs, scr):
    outs[0][...] = accs[0].astype(outs[0].dtype)


def _epi_swiglu(accs, extra, outs, scr):
    g, u = accs
    outs[0][...] = (g * (1.0 / (1.0 + jnp.exp(-g))) * u).astype(outs[0].dtype)


def _epi_residual(alpha, accs, extra, outs, scr):
    outs[0][...] = extra[0][...] + alpha * accs[0]


def _epi_residual_norm(alpha, width, accs, extra, outs, scr):
    res_ref = extra[0]
    h_o, rs_o = outs[0], outs[-1]
    ss = scr[0]
    i = pl.program_id(1)
    h = res_ref[...] + alpha * accs[0]
    h_o[...] = h
    for g_ref, hg_o in zip(extra[1:], outs[1:-1]):
        hg_o[...] = (h * g_ref[...]).astype(hg_o.dtype)
    sq = h * h
    part = sq[:, :LANES]
    for k in range(1, sq.shape[1] // LANES):
        part = part + sq[:, k * LANES:(k + 1) * LANES]
    tot = ss[i] + part
    ss[i] = tot

    @pl.when(pl.program_id(0) == pl.num_programs(0) - 1)
    def _():
        rs_o[...] = jnp.broadcast_to(lax.rsqrt(jnp.sum(tot, -1, keepdims=True) / width + EPS), rs_o.shape)


def _mm_plain(act, w, layer, col_off, n_cols, tn, dtype, w_transposed=False):
    M = act[0].shape[0]
    tm = min(1024, M)
    out = (jax.ShapeDtypeStruct((M, n_cols), dtype), (tm, tn), lambda j, i: (i, j))
    return _mm(act, [(w, layer, col_off)], _epi_plain, [], [out], n_cols=n_cols, tn=tn, tm=tm,
               w_transposed=w_transposed)[0]


def _mm_residual(x, w, layer, res, alpha, next_norms=()):
    M, N = res.shape
    K = x.shape[1]
    tn = 512 if N % 512 == 0 else 256
    tm = min(1024 if K <= 4096 else 512, M)
    blk = ((tm, tn), lambda j, i: (i, j))
    h_out = (jax.ShapeDtypeStruct((M, N), F32),) + blk
    if not next_norms:
        return _mm((x, None), [(w, layer, 0)], functools.partial(_epi_residual, alpha), [(res,) + blk],
                   [h_out], n_cols=N, tn=tn, tm=tm)[0], []
    g_specs = [(gains.reshape(-1, 1, N), (None, 1, tn), lambda j, i, gl=gl: (gl, 0, j)) for gains, gl in next_norms]
    hg_outs = [(jax.ShapeDtypeStruct((M, N), BF16),) + blk for _ in next_norms]
    last_j = N // tn - 1
    rs_out = (jax.ShapeDtypeStruct((M, LANES), F32), (tm, LANES), lambda j, i: (jnp.where(j == last_j, i, 0), 0))
    res_all = _mm((x, None), [(w, layer, 0)], functools.partial(_epi_residual_norm, alpha, N),
                  [(res,) + blk] + g_specs, [h_out] + hg_outs + [rs_out],
                  n_cols=N, tn=tn, tm=tm, scratch=[((M // tm, tm, LANES), F32)])
    return res_all[0], [(hg, res_all[-1]) for hg in res_all[1:-1]]


def _ffn(h, act, wg, wu, wd, layer, next_norms):
    M, D = h.shape
    F = wg.shape[-1]
    tn = 512 if F % 512 == 0 else 256
    tm = min(512, M)
    hid = _mm(act, [(wg, layer, 0), (wu, layer, 0)], _epi_swiglu, [],
              [(jax.ShapeDtypeStruct((M, F), BF16), (tm, tn), lambda j, i: (i, j))],
              n_cols=F, tn=tn, tm=tm)[0]
    return _mm_residual(hid, wd, layer, h, 0.5, next_norms)


def _epi_mla_q(n_heads, out_scale, accs, extra, outs, scr):
    g = extra[0][...] * out_scale
    c, s = extra[1][...], extra[2][...]
    o = outs[0]
    sel = _sel_matrix((MLA_HEAD_PAD, LANES), lambda r, col: r < MLA_QK)
    acc = accs[0]
    for hh in range(n_heads):
        b0 = hh * MLA_HEAD_PAD
        a = acc[:, b0:b0 + MLA_HEAD_PAD]
        r = lax.rsqrt(_mxu_row_sum(a * a, sel) / MLA_QK + EPS)
        o[:, b0:b0 + LANES] = (a[:, :LANES] * r * g[:, :LANES]).astype(o.dtype)
        o[:, b0 + LANES:b0 + 2 * LANES] = _rope_mla(a[:, LANES:] * r * g[:, LANES:], c, s).astype(o.dtype)


def _epi_mla_kv(n_heads, accs, extra, outs, scr):
    g = extra[0][...]
    c, s = extra[1][...], extra[2][...]
    kr = extra[3][...]
    kr2 = kr * kr
    k_o, v_o = outs
    sel = _sel_matrix((MLA_HEAD_PAD, LANES), lambda r, col: r < MLA_QK)
    acc = accs[0]
    for hh in range(n_heads):
        b0 = hh * MLA_HEAD_PAD
        a = acc[:, b0:b0 + MLA_HEAD_PAD]
        kn = a[:, :LANES]
        ss = _mxu_row_sum(jnp.concatenate([kn * kn, kr2], axis=1), sel)
        r = lax.rsqrt(ss / MLA_QK + EPS)
        k_o[:, b0:b0 + LANES] = (kn * r * g[:, :LANES]).astype(k_o.dtype)
        k_o[:, b0 + LANES:b0 + 2 * LANES] = _rope_mla(kr * r * g[:, LANES:], c, s).astype(k_o.dtype)
        v_o[:, hh * MLA_V:(hh + 1) * MLA_V] = a[:, LANES:].astype(v_o.dtype)


def _pad_gain_mla(g):
    return jnp.concatenate([g, g[MLA_NOPE:]]).reshape(1, MLA_HEAD_PAD)


def _mla_attn_kernel(S, tq, nh, q_ref, k_ref, v_ref, o_ref):
    row = lax.broadcasted_iota(jnp.int32, (tq, tq), 0)
    col = lax.broadcasted_iota(jnp.int32, (tq, tq), 1)
    tri = jnp.where(col <= row, 0.0, NEG)
    nt = (((1,), (1,)), ((), ()))

    def scores(i, hh):
        r0 = i * tq
        ks = slice(hh * MLA_HEAD_PAD, (hh + 1) * MLA_HEAD_PAD)
        q = q_ref[r0:r0 + tq, ks]
        s_d = lax.dot_general(q, k_ref[r0:r0 + tq, ks], nt, preferred_element_type=F32) + tri
        s_o = lax.dot_general(q, k_ref[0:r0, ks], nt, preferred_element_type=F32) if i > 0 else None
        return s_d, s_o

    def finish(i, hh, s_d, s_o):
        r0 = i * tq
        vs = slice(hh * MLA_V, (hh + 1) * MLA_V)
        m = jnp.max(s_d, -1, keepdims=True)
        if i > 0:
            m = jnp.maximum(m, jnp.max(s_o, -1, keepdims=True))
        e_d = jnp.exp2(s_d - m)
        l = jnp.sum(e_d, -1, keepdims=True)
        o = jnp.dot(e_d.astype(BF16), v_ref[r0:r0 + tq, vs], preferred_element_type=F32)
        if i > 0:
            e_o = jnp.exp2(s_o - m)
            l = l + jnp.sum(e_o, -1, keepdims=True)
            o = o + jnp.dot(e_o.astype(BF16), v_ref[0:r0, vs], preferred_element_type=F32)
        o_ref[r0:r0 + tq, vs] = (o / l).astype(o_ref.dtype)

    tiles = [(i, hh) for i in reversed(range(S // tq)) for hh in range(nh)]
    pending = [scores(*t) for t in tiles[:MLA_SCORES_AHEAD]]
    for n, t in enumerate(tiles):
        if n + MLA_SCORES_AHEAD < len(tiles):
            pending.append(scores(*tiles[n + MLA_SCORES_AHEAD]))
        finish(*t, *pending.pop(0))


def _mla_attention(q, k, v, B, S, H):
    T = B * S
    tq = min(ATTN_TQ, S)
    nh = 2 if H % 2 == 0 else 1
    kern = functools.partial(_mla_attn_kernel, S, tq, nh)
    nbytes = 2 * nh * S * (2 * MLA_HEAD_PAD + 2 * MLA_V) * 2 + 12 * tq * S * 4
    return pl.pallas_call(
        kern,
        name="mla_attn",
        grid=(B, H // nh),
        in_specs=[pl.BlockSpec((S, nh * MLA_HEAD_PAD), lambda b, h: (b, h)),
                  pl.BlockSpec((S, nh * MLA_HEAD_PAD), lambda b, h: (b, h)),
                  pl.BlockSpec((S, nh * MLA_V), lambda b, h: (b, h))],
        out_specs=pl.BlockSpec((S, nh * MLA_V), lambda b, h: (b, h)),
        out_shape=jax.ShapeDtypeStruct((T, H * MLA_V), BF16),
        compiler_params=_params(("arbitrary", "arbitrary"), nbytes),
    )(q, k, v)


def _mla_layer(h, act, a, B, S, tabs, mla_w_in, mla_g_cq, mla_g_ckv, mla_w_uq, mla_w_ukv, mla_g_q, mla_g_k, mla_w_o,
               next_norms):
    T, D = h.shape
    q_lora = mla_g_cq.shape[-1]
    kv_lora = mla_g_ckv.shape[-1]
    H = mla_w_ukv.shape[-1] // (MLA_NOPE + MLA_V)
    c_tab, s_tab = tabs
    tm = min(2048, T)
    lat = q_lora + kv_lora
    assert q_lora % kv_lora == 0
    w_in_t = jnp.swapaxes(mla_w_in, 1, 2)
    cq = _mm_plain(act, w_in_t, a, 0, q_lora, min(512, q_lora), F32, w_transposed=True)
    ckv = _mm_plain(act, w_in_t, a, q_lora // kv_lora, kv_lora, kv_lora, F32, w_transposed=True)
    w_kr = lax.slice(mla_w_in, (a, 0, lat), (a + 1, D, lat + MLA_ROPE))[0]
    kr = _mm_plain(act, jnp.concatenate([w_kr, w_kr], axis=1), None, 0, LANES, LANES, F32)
    cq = _rmsnorm(cq, mla_g_cq[a])
    ckv = _rmsnorm(ckv, mla_g_ckv[a])
    w_uq = mla_w_uq[a].reshape(q_lora, H, MLA_QK)
    w_uq = jnp.concatenate([w_uq, w_uq[:, :, MLA_NOPE:]], axis=2).reshape(q_lora, H * MLA_HEAD_PAD)
    hp = 2 if H % 2 == 0 else 1
    tn = hp * MLA_HEAD_PAD
    tab_specs = [(t, (tm, LANES), lambda j, i: (i, 0)) for t in (c_tab, s_tab)]
    gain = lambda g: (_pad_gain_mla(g), (1, MLA_HEAD_PAD), lambda j, i: (0, 0))
    q = _mm((cq, None), [(w_uq, None, 0)], functools.partial(_epi_mla_q, hp, MLA_QK ** -0.5 * LOG2E),
            [gain(mla_g_q[a])] + tab_specs,
            [(jax.ShapeDtypeStruct((T, H * MLA_HEAD_PAD), BF16), (tm, tn), lambda j, i: (i, j))],
            n_cols=H * MLA_HEAD_PAD, tn=tn, tm=tm)[0]
    kr_spec = (kr, (tm, LANES), lambda j, i: (i, 0))
    k, v = _mm((ckv, None), [(mla_w_ukv, a, 0)], functools.partial(_epi_mla_kv, hp),
               [gain(mla_g_k[a])] + tab_specs + [kr_spec],
               [(jax.ShapeDtypeStruct((T, H * MLA_HEAD_PAD), BF16), (tm, tn), lambda j, i: (i, j)),
                (jax.ShapeDtypeStruct((T, H * MLA_V), BF16), (tm, hp * MLA_V), lambda j, i: (i, j))],
               n_cols=H * MLA_HEAD_PAD, tn=tn, tm=tm)
    o = _mla_attention(q, k, v, B, S, H)
    return _mm_residual(o, mla_w_o, a, h, 1.0, next_norms)


def _nsa_head_norm(a, g2):
    sel = _sel_matrix((MXU_DIM, MXU_DIM), lambda r, col: r // NSA_DH == col // NSA_DH)
    r = lax.rsqrt(_mxu_row_sum(a * a, sel) / NSA_DH + EPS)
    return a * r * g2


def _epi_nsa_k(n_pairs, accs, extra, outs, scr):
    g = extra[0][...]
    g2 = jnp.concatenate([g, g], axis=1)
    c, s = extra[1][...], extra[2][...]
    o = outs[0]
    acc = accs[0]
    for pp in range(n_pairs):
        kn = _nsa_head_norm(acc[:, pp * MXU_DIM:(pp + 1) * MXU_DIM], g2)
        for hh in range(2):
            col = (2 * pp + hh) * NSA_DH
            o[:, col:col + NSA_DH] = _rope_nsa(kn[:, hh * NSA_DH:(hh + 1) * NSA_DH], c, s).astype(o.dtype)


def _compress_kernel(n_cmp, k_ref, v_ref, pk_ref, pv_ref, kw1_ref, kb1_ref, kw2_ref,
                     vw1_ref, vb1_ref, vw2_ref, gk_ref, ko_ref, vo_ref):
    half = CMP_STRIDE * NSA_DH
    n_rows = ko_ref.shape[0]
    row = lax.broadcasted_iota(jnp.int32, (n_rows, 1), 0)

    def phi(t_ref, pos_ref, w1_ref, b1_ref, w2_ref):
        slabs = [t_ref[pl.ds(l, n_rows, stride=CMP_STRIDE), :] for l in range(CMP_STRIDE)]
        lo = jnp.concatenate([slabs[l] + pos_ref[:, l * NSA_DH:(l + 1) * NSA_DH]
                              for l in range(CMP_STRIDE)], axis=1).astype(BF16)
        hi = jnp.concatenate([slabs[l] + pos_ref[:, half + l * NSA_DH:half + (l + 1) * NSA_DH]
                              for l in range(CMP_STRIDE)], axis=1).astype(BF16)
        p_lo = jnp.dot(lo, w1_ref[0:half, :].astype(BF16), preferred_element_type=F32)
        p_hi = jnp.dot(hi, w1_ref[half:2 * half, :].astype(BF16), preferred_element_type=F32)
        pre = p_lo + pltpu.roll(p_hi, n_rows - 1, 0) + b1_ref[...]
        hid = pre * (1.0 / (1.0 + jnp.exp(-pre)))
        out = jnp.dot(hid.astype(BF16), w2_ref[...].astype(BF16), preferred_element_type=F32)
        return jnp.where(row < n_cmp, out, 0.0)

    kc = phi(k_ref, pk_ref, kw1_ref, kb1_ref, kw2_ref)
    r = lax.rsqrt(jnp.mean(kc * kc, -1, keepdims=True) + EPS)
    ko_ref[...] = (kc * r * gk_ref[...]).astype(ko_ref.dtype)
    vo_ref[...] = phi(v_ref, pv_ref, vw1_ref, vb1_ref, vw2_ref).astype(vo_ref.dtype)


def _compress(raw, B, S, G, cmp_pos_k, cmp_pos_v, k_w1, k_b1, k_w2, v_w1, v_b1, v_w2, g_k_cmp):
    n_cmp = (S - CMP_BLOCK) // CMP_STRIDE + 1
    n_rows = S // CMP_STRIDE
    hid = k_w1.shape[-1]
    full = lambda shape: pl.BlockSpec(shape, lambda b, g: (0,) * len(shape))
    w_specs = [full((CMP_BLOCK * NSA_DH, hid)), full((1, hid)), full((hid, NSA_DH))]
    out_spec = pl.BlockSpec((None, None, n_rows, NSA_DH), lambda b, g: (b, g, 0, 0))
    out_sds = jax.ShapeDtypeStruct((B, G, n_rows, NSA_DH), BF16)
    nbytes = 4 * (2 * CMP_BLOCK * NSA_DH * hid * 4) + 4 * S * NSA_DH * 4 + 8 * n_rows * CMP_BLOCK * NSA_DH * 4
    return pl.pallas_call(
        functools.partial(_compress_kernel, n_cmp),
        name="nsa_compress",
        grid=(B, G),
        in_specs=[pl.BlockSpec((S, NSA_DH), lambda b, g: (b, g)),
                  pl.BlockSpec((S, NSA_DH), lambda b, g: (b, G + g)),
                  full((1, CMP_BLOCK * NSA_DH)), full((1, CMP_BLOCK * NSA_DH))]
                 + w_specs + w_specs + [full((1, NSA_DH))],
        out_specs=[out_spec, out_spec],
        out_shape=[out_sds, out_sds],
        compiler_params=_params(("arbitrary", "arbitrary"), nbytes),
    )(raw, raw, cmp_pos_k.reshape(1, -1), cmp_pos_v.reshape(1, -1),
      k_w1, k_b1.reshape(1, -1), k_w2, v_w1, v_b1.reshape(1, -1), v_w2, g_k_cmp.reshape(1, -1))


def _nsa_shared_kv(y, B, S, tabs, kv_w, cmp_pos_k, cmp_pos_v, cmp_k_w1, cmp_k_b1, cmp_k_w2,
                   cmp_v_w1, cmp_v_b1, cmp_v_w2, g_k_cmp, g_k_slc, g_k_win):
    T = y[0].shape[0]
    G = kv_w.shape[-1] // (2 * N_BRANCH * NSA_DH)
    assert G % 2 == 0
    part = G * NSA_DH
    tm = min(1024, T)
    c_tab, s_tab = tabs
    raw = _mm_plain(y, kv_w, None, 0, 2 * part, part, F32)
    k_cmp, v_cmp = _compress(raw, B, S, G, cmp_pos_k, cmp_pos_v, cmp_k_w1, cmp_k_b1, cmp_k_w2,
                             cmp_v_w1, cmp_v_b1, cmp_v_w2, g_k_cmp)

    def k_branch(part_idx, gain):
        extras = [(gain.reshape(1, NSA_DH), (1, NSA_DH), lambda j, i: (0, 0)),
                  (c_tab, (tm, LANES), lambda j, i: (i, 0)), (s_tab, (tm, LANES), lambda j, i: (i, 0))]
        return _mm(y, [(kv_w, None, part_idx)], functools.partial(_epi_nsa_k, G // 2), extras,
                   [(jax.ShapeDtypeStruct((T, part), BF16), (tm, part), lambda j, i: (i, j))],
                   n_cols=part, tn=part, tm=tm)[0]

    k_slc = k_branch(2, g_k_slc)
    v_slc = _mm_plain(y, kv_w, None, 3, part, part, BF16)
    k_win = k_branch(4, g_k_win)
    v_win = _mm_plain(y, kv_w, None, 5, part, part, BF16)
    return k_cmp, v_cmp, k_slc, v_slc, k_win, v_win


def _epi_nsa_q(n_pairs, out_scale, accs, extra, outs, scr):
    g = extra[0][...] * out_scale
    g2 = jnp.concatenate([g, g], axis=1)
    c, s = extra[1][...], extra[2][...]
    q_o, qr_o = outs
    acc = accs[0]
    for pp in range(n_pairs):
        qn = _nsa_head_norm(acc[:, pp * MXU_DIM:(pp + 1) * MXU_DIM], g2)
        for hh in range(2):
            x = qn[:, hh * NSA_DH:(hh + 1) * NSA_DH]
            q_o[2 * pp + hh] = x.astype(q_o.dtype)
            qr_o[2 * pp + hh] = _rope_nsa(x, c, s).astype(qr_o.dtype)


def _nsa_attn_kernel(S, tq, hpg, n_cmp,
                     q_ref, qr_ref, gl_ref, bg_ref, kc_ref, vc_ref, ks_ref, vs_ref, kw_ref, vw_ref,
                     o_ref, ocmp_scr, obuf_scr):
    g_idx = pl.program_id(1)
    n_cr = kc_ref.shape[0]
    n_slc = S // SLC_BLOCK
    n_sel = min(SLC_TOPK, n_slc)
    per_trip = NSA_HEADS_PER_TRIP if hpg % NSA_HEADS_PER_TRIP == 0 else 2
    k_cmp = kc_ref[...]
    v_cmp = vc_ref[...]
    lane = lax.broadcasted_iota(jnp.int32, (tq, LANES), 1)
    nt = (((1,), (1,)), ((), ()))

    jj = lax.broadcasted_iota(jnp.int32, (n_slc, n_cr), 0)
    nn = lax.broadcasted_iota(jnp.int32, (n_slc, n_cr), 1)
    cs = nn * CMP_STRIDE
    ss = jj * SLC_BLOCK
    ov = jnp.maximum(jnp.minimum(cs + CMP_BLOCK, ss + SLC_BLOCK) - jnp.maximum(cs, ss), 0)
    agg_t = jnp.where(nn < n_cmp, ov.astype(F32) / CMP_STRIDE, 0.0)

    for i in range(S // tq):
        r0 = i * tq
        nk = r0 + tq
        rows = slice(r0, r0 + tq)

        spos_c = r0 + lax.broadcasted_iota(jnp.int32, (tq, n_cr), 0)
        ncol = lax.broadcasted_iota(jnp.int32, (tq, n_cr), 1)
        cmask = (ncol * CMP_STRIDE + (CMP_BLOCK - 1) <= spos_c) & (ncol < n_cmp)
        cbias = jnp.where(cmask, 0.0, NEG)
        cmask_f = cmask.astype(F32)
        q_all = q_ref[:, rows, :].reshape(hpg * tq, NSA_DH)
        sc = lax.dot_general(q_all, k_cmp, nt, preferred_element_type=F32).reshape(hpg, tq, n_cr) + cbias
        m = jnp.max(sc, -1, keepdims=True)
        e = jnp.exp2(sc - m) * cmask_f
        l = jnp.sum(e, -1, keepdims=True)
        p = e / jnp.where(l > 0.0, l, 1.0)
        ocmp_scr[...] = jnp.dot(p.reshape(hpg * tq, n_cr).astype(BF16), v_cmp,
                                preferred_element_type=F32).reshape(hpg, tq, NSA_DH)
        psum = jnp.sum(p, axis=0)

        imp = lax.dot_general(agg_t, psum, nt, precision=lax.Precision.HIGHEST,
                              preferred_element_type=F32)
        jb = lax.broadcasted_iota(jnp.int32, (n_slc, tq), 0)
        sp = r0 + lax.broadcasted_iota(jnp.int32, (n_slc, tq), 1)
        cur = sp // SLC_BLOCK
        valid = jb * SLC_BLOCK <= sp
        forced = (jb == 0) | (jb == cur) | (jb == cur - 1)
        imp = jnp.where(forced, FORCE, jnp.where(valid, imp, -1.0))
        rank = jnp.zeros((n_slc, tq), jnp.int32)
        for jp in range(n_slc):
            other = imp[jp:jp + 1, :]
            ahead = (other > imp) | ((other == imp) & (jp < jb))
            rank = rank + ahead.astype(jnp.int32)
        sel_t = (rank < n_sel).astype(BF16)

        eb = lax.broadcasted_iota(jnp.int32, (n_slc, nk), 0)
        ek = lax.broadcasted_iota(jnp.int32, (n_slc, nk), 1)
        expand = (ek // SLC_BLOCK == eb).astype(BF16)
        sel_keys = lax.dot_general(sel_t, expand, (((0,), (0,)), ((), ())), preferred_element_type=F32)
        qrow = r0 + lax.broadcasted_iota(jnp.int32, (tq, nk), 0)
        kcol = lax.broadcasted_iota(jnp.int32, (tq, nk), 1)
        sbias = jnp.where((sel_keys > 0.5) & (kcol <= qrow), 0.0, NEG)

        w0 = (max(0, r0 - WINDOW) // tq) * tq
        nw = nk - w0
        wrow = r0 + lax.broadcasted_iota(jnp.int32, (tq, nw), 0)
        wcol = w0 + lax.broadcasted_iota(jnp.int32, (tq, nw), 1)
        wdiff = wrow - wcol
        wbias = jnp.where((wdiff >= 0) & (wdiff < WINDOW), 0.0, NEG)

        z = gl_ref[rows, :] + bg_ref[...]
        gates = 1.0 / (1.0 + jnp.exp(-z))

        k_s = ks_ref[0:nk, :]
        v_s = vs_ref[0:nk, :]
        k_w = kw_ref[w0:nk, :]
        v_w = vw_ref[w0:nk, :]

        def main_body(pp, carry):
            hs = tuple(per_trip * pp + t for t in range(per_trip))
            qs = [qr_ref[hh, rows, :] for hh in hs]
            s_s = [lax.dot_general(q, k_s, nt, preferred_element_type=F32) + sbias for q in qs]
            s_w = [lax.dot_general(q, k_w, nt, preferred_element_type=F32) + wbias for q in qs]
            m_s = [jnp.max(s, -1, keepdims=True) for s in s_s]
            m_w = [jnp.max(s, -1, keepdims=True) for s in s_w]
            e_s = [jnp.exp2(s - m) for s, m in zip(s_s, m_s)]
            e_w = [jnp.exp2(s - m) for s, m in zip(s_w, m_w)]
            l_s = [jnp.sum(e, -1, keepdims=True) for e in e_s]
            l_w = [jnp.sum(e, -1, keepdims=True) for e in e_w]
            o_s = [jnp.dot(e.astype(BF16), v_s, preferred_element_type=F32) / l for e, l in zip(e_s, l_s)]
            o_w = [jnp.dot(e.astype(BF16), v_w, preferred_element_type=F32) / l for e, l in zip(e_w, l_w)]
            for t, hh in enumerate(hs):
                c0 = (g_idx * hpg + hh) * N_BRANCH
                gate = lambda br: jnp.sum(jnp.where(lane == c0 + br, gates, 0.0), -1, keepdims=True)
                o = gate(0) * ocmp_scr[hh] + gate(1) * o_s[t] + gate(2) * o_w[t]
                obuf_scr[hh] = o.astype(obuf_scr.dtype)
            return carry

        lax.fori_loop(0, hpg // per_trip, main_body, 0)
        for hh in range(hpg):
            o_ref[rows, hh * NSA_DH:(hh + 1) * NSA_DH] = obuf_scr[hh]


def _nsa_attention(q, qr, gate_logits, b_gate, shared, B, S, G, hpg):
    k_cmp, v_cmp, k_slc, v_slc, k_win, v_win = shared
    T = B * S
    H = G * hpg
    tq = min(ATTN_TQ, S)
    n_cmp = (S - CMP_BLOCK) // CMP_STRIDE + 1
    n_cr = k_cmp.shape[2]
    kern = functools.partial(_nsa_attn_kernel, S, tq, hpg, n_cmp)
    head_spec = pl.BlockSpec((hpg, S, NSA_DH), lambda b, g: (g, b, 0))
    cmp_spec = pl.BlockSpec((None, None, n_cr, NSA_DH), lambda b, g: (b, g, 0, 0))
    kv_spec = pl.BlockSpec((S, NSA_DH), lambda b, g: (b, g))
    nbytes = (2 * 2 * hpg * S * NSA_DH * 2 + 2 * S * LANES * 4 + 8 * S * NSA_DH * 2
              + 2 * S * hpg * NSA_DH * 2 + 10 * tq * S * 4)
    return pl.pallas_call(
        kern,
        name="nsa_attn",
        grid=(B, G),
        in_specs=[head_spec, head_spec,
                  pl.BlockSpec((S, LANES), lambda b, g: (b, 0)),
                  pl.BlockSpec((1, LANES), lambda b, g: (0, 0)),
                  cmp_spec, cmp_spec, kv_spec, kv_spec, kv_spec, kv_spec],
        out_specs=pl.BlockSpec((S, hpg * NSA_DH), lambda b, g: (b, g)),
        out_shape=jax.ShapeDtypeStruct((T, H * NSA_DH), BF16),
        scratch_shapes=[pltpu.VMEM((hpg, tq, NSA_DH), F32), pltpu.VMEM((hpg, tq, NSA_DH), BF16)],
        compiler_params=_params(("arbitrary", "arbitrary"), nbytes),
    )(q, qr, gate_logits, b_gate, k_cmp, v_cmp, k_slc, v_slc, k_win, v_win)


def _nsa_layer(h, u, b, B, S, G, tabs, shared, nsa_w_in, nsa_b_gate, nsa_g_q, nsa_w_o, next_norms):
    T, D = h.shape
    n_gate = nsa_b_gate.shape[-1]
    H = n_gate // N_BRANCH
    hpg = H // G
    q_width = H * NSA_DH
    assert n_gate <= LANES and hpg % 2 == 0
    c_tab, s_tab = tabs
    tm = min(1024, T)
    hp = min(4, H)
    tn = hp * NSA_DH
    extras = [(nsa_g_q[b].reshape(1, NSA_DH), (1, NSA_DH), lambda j, i: (0, 0)),
              (c_tab, (tm, LANES), lambda j, i: (i, 0)), (s_tab, (tm, LANES), lambda j, i: (i, 0))]
    head_out = (jax.ShapeDtypeStruct((H, T, NSA_DH), BF16), (hp, tm, NSA_DH), lambda j, i: (j, i, 0))
    q, qr = _mm(u, [(jnp.swapaxes(nsa_w_in, 1, 2), b, 0)],
                functools.partial(_epi_nsa_q, hp // 2, NSA_DH ** -0.5 * LOG2E),
                extras, [head_out, head_out], n_cols=q_width, tn=tn, tm=tm, w_transposed=True)
    w_gate = lax.slice(nsa_w_in, (b, 0, q_width), (b + 1, D, q_width + n_gate))[0]
    w_gate = jnp.pad(w_gate, ((0, 0), (0, LANES - n_gate)))
    gate_logits = _mm_plain(u, w_gate, None, 0, LANES, LANES, F32)
    b_gate = jnp.pad(nsa_b_gate[b], (0, LANES - n_gate)).reshape(1, LANES)
    o = _nsa_attention(q, qr, gate_logits, b_gate, shared, B, S, G, hpg)
    return _mm_residual(o, nsa_w_o, b, h, 1.0, next_norms)


def _lane_table(values):
    return jnp.asarray(values, F32).reshape(1, LANES)


def kernel(x, positions, ffn1_norm, ffn1_w_gate, ffn1_w_up, ffn1_w_down, mix_norm, ffn2_norm, ffn2_w_gate, ffn2_w_up, ffn2_w_down, mla_w_in, mla_g_cq, mla_g_ckv, mla_w_uq, mla_w_ukv, mla_g_q, mla_g_k, mla_w_o, kv_norm, kv_w, cmp_pos_k, cmp_pos_v, cmp_k_w1, cmp_k_b1, cmp_k_w2, cmp_v_w1, cmp_v_b1, cmp_v_w2, g_k_cmp, g_k_slc, g_k_win, nsa_w_in, nsa_b_gate, nsa_g_q, nsa_w_o):
    B, S, D = x.shape
    T = B * S
    depth = ffn1_norm.shape[0]
    n_a = mla_w_in.shape[0]
    G = kv_w.shape[-1] // (2 * N_BRANCH * NSA_DH)

    pos = positions.reshape(T, 1).astype(F32)
    half_a = MLA_ROPE // 2
    inv_a = jnp.power(ROPE_THETA, -jnp.arange(0, MLA_ROPE, 2, dtype=F32) / MLA_ROPE)
    zeros_a = jnp.zeros((LANES - MLA_ROPE,), F32)
    ones_h = jnp.ones((half_a,), F32)
    tabs_a = _rope_tables(pos, _lane_table(jnp.concatenate([inv_a, inv_a, zeros_a])),
                          _lane_table(jnp.concatenate([ones_h, ones_h, zeros_a])),
                          _lane_table(jnp.concatenate([-ones_h, ones_h, zeros_a])))
    inv_b = jnp.power(ROPE_THETA, -jnp.arange(0, NSA_DH, 2, dtype=F32) / NSA_DH)
    ones_b = jnp.ones((NSA_DH // 2,), F32)
    tabs_b = _rope_tables(pos, _lane_table(jnp.concatenate([inv_b, inv_b])),
                          _lane_table(jnp.concatenate([ones_b, ones_b])),
                          _lane_table(jnp.concatenate([-ones_b, ones_b])))

    h = x.reshape(T, D)
    shared = None
    act = (_rmsnorm(h, ffn1_norm, layer=0), None)
    for layer in range(depth):
        h, (act,) = _ffn(h, act, ffn1_w_gate, ffn1_w_up, ffn1_w_down, layer, [(mix_norm, layer)])
        if layer < n_a:
            h, (act,) = _mla_layer(h, act, layer, B, S, tabs_a, mla_w_in, mla_g_cq, mla_g_ckv, mla_w_uq, mla_w_ukv,
                                   mla_g_q, mla_g_k, mla_w_o, [(ffn2_norm, layer)])
        else:
            h, (act,) = _nsa_layer(h, act, layer - n_a, B, S, G, tabs_b, shared, nsa_w_in, nsa_b_gate, nsa_g_q,
                                   nsa_w_o, [(ffn2_norm, layer)])
        norms = [(ffn1_norm, layer + 1)] if layer + 1 < depth else []
        if layer == n_a - 1:
            norms = norms + [(kv_norm, 0)]
        h, acts = _ffn(h, act, ffn2_w_gate, ffn2_w_up, ffn2_w_down, layer, norms)
        act = acts[0] if acts else None
        if layer == n_a - 1:
            shared = _nsa_shared_kv(acts[-1], B, S, tabs_b, kv_w, cmp_pos_k, cmp_pos_v, cmp_k_w1, cmp_k_b1,
                                    cmp_k_w2, cmp_v_w1, cmp_v_b1, cmp_v_w2, g_k_cmp, g_k_slc, g_k_win)
    return h.reshape(B, S, D)
```
